```python
import math
import jax, jax.numpy as jnp
from jax import lax
import numpy as np

D_MODEL = 2048
BATCH = 16
SEQ = 256
DEPTH = 1
DEC_BATCH = 2
DEC_SEQ = 4096
PAST_LEN = 256

GRID_W = 64
NK_HEADS = 16
NV_HEADS = 32
DK = 128
DV = 128
QK_W = NK_HEADS * DK
V_W = NV_HEADS * DV
QKV_W = 2 * QK_W + V_W
QKV_CONV = 3
CHUNK = 64
CONV_CH = D_MODEL
GCONV = 3
IN_SIZES = (QKV_W, V_W, 2 * NV_HEADS, 2 * NV_HEADS, CONV_CH, CONV_CH, CONV_CH, D_MODEL, D_MODEL)
IN_W = sum(IN_SIZES)
N_EXPERTS = 32
TOP_K = 4
D_FF = D_MODEL
SWIGLU_LIMIT = 7.0
SWIGLU_ALPHA = 1.702
MOE_BLOCK = 128
DEEPNORM_ALPHA = (2 * DEPTH) ** 0.25
DEEPNORM_BETA = (8 * DEPTH) ** -0.25
EPS = 1e-6
POS_BASE = 10000.0

kernel_name = 'hybrid_deltanet_shortconv_moe_diffusion_step'


def layer_norm(x, g, b):
    xf = x.astype(jnp.float32)
    mu = xf.mean(-1, keepdims=True)
    var = jnp.square(xf - mu).mean(-1, keepdims=True)
    return ((xf - mu) * lax.rsqrt(var + EPS) * g.astype(jnp.float32) + b.astype(jnp.float32)).astype(x.dtype)


def l2norm(t):
    return t * lax.rsqrt(jnp.sum(t * t, -1, keepdims=True) + EPS)


def dwconv_centred(x, w):
    k = w.shape[0]
    return lax.conv_general_dilated(x, w[:, None, :].astype(x.dtype), (1,), [(k // 2, k // 2)],
                                    dimension_numbers=('NWC', 'WIO', 'NWC'), feature_group_count=x.shape[-1])


def grid_pos_embed(rows, dim):
    quarter = dim // 4
    freqs = 1.0 / (POS_BASE ** (jnp.arange(quarter, dtype=jnp.float32) / quarter))
    def axis_embed(n):
        ang = jnp.arange(n, dtype=jnp.float32)[:, None] * freqs[None, :]
        return jnp.concatenate([jnp.sin(ang), jnp.cos(ang)], axis=-1)
    er = axis_embed(rows)
    ec = axis_embed(GRID_W)
    pos = jnp.concatenate([jnp.broadcast_to(er[:, None, :], (rows, GRID_W, dim // 2)),
                           jnp.broadcast_to(ec[None, :, :], (rows, GRID_W, dim // 2))], axis=-1)
    return pos.reshape(rows * GRID_W, dim)


def gated_delta_chunked(q, k, v, g, beta, s0):
    bsz, t, h, _ = q.shape
    n = t // CHUNK
    def chunks(a):
        return a.reshape((bsz, n, CHUNK, h) + a.shape[3:]).transpose((1, 0, 3, 2) + tuple(range(4, a.ndim + 1)))
    qc = chunks(q * DK ** -0.5)
    kc = chunks(k)
    vc = chunks(v)
    gc = jnp.cumsum(chunks(g), axis=-1)
    bc = chunks(beta)
    causal = jnp.tril(jnp.ones((CHUNK, CHUNK), bool))
    strict = jnp.tril(jnp.ones((CHUNK, CHUNK), bool), -1)
    decay = jnp.exp(jnp.where(causal, gc[..., :, None] - gc[..., None, :], -jnp.inf))
    kb = kc * bc[..., None]
    lmat = jnp.where(strict, jnp.einsum('nbhid,nbhjd->nbhij', kb, kc) * decay, 0.0)
    rhs = jnp.concatenate([vc * bc[..., None], kb * jnp.exp(gc)[..., None]], axis=-1)
    sol = lax.linalg.triangular_solve(lmat, rhs, left_side=True, lower=True, unit_diagonal=True)
    u, w = sol[..., :DV], sol[..., DV:]
    qk = jnp.where(causal, jnp.einsum('nbhid,nbhjd->nbhij', qc, kc) * decay, 0.0)
    def step(s, inp):
        q_i, k_i, u_i, w_i, g_i, a_i = inp
        v_new = u_i - jnp.einsum('bhck,bhkv->bhcv', w_i, s)
        o = (jnp.einsum('bhck,bhkv->bhcv', q_i * jnp.exp(g_i)[..., None], s)
             + jnp.einsum('bhij,bhjv->bhiv', a_i, v_new))
        g_last = g_i[..., -1:]
        s = (s * jnp.exp(g_last)[..., None]
             + jnp.einsum('bhck,bhcv->bhkv', k_i * jnp.exp(g_last - g_i)[..., None], v_new))
        return s, o
    s, o = lax.scan(step, s0, (qc, kc, u, w, gc, qk))
    return o.transpose(1, 0, 3, 2, 4).reshape(bsz, t, h, -1), s


def deltanet_branch(qkv, z, b_fb, a_fb, conv_w, a_log, dt_bias, norm_w, w_out, s_f, s_b):
    bsz, t, _ = qkv.shape
    dt = qkv.dtype
    f32 = jnp.float32
    qkv = jax.nn.silu(dwconv_centred(qkv, conv_w)).astype(f32)
    rep = NV_HEADS // NK_HEADS
    q = jnp.repeat(l2norm(qkv[..., :QK_W].reshape(bsz, t, NK_HEADS, DK)), rep, axis=2)
    k = jnp.repeat(l2norm(qkv[..., QK_W:2 * QK_W].reshape(bsz, t, NK_HEADS, DK)), rep, axis=2)
    v = qkv[..., 2 * QK_W:].reshape(bsz, t, NV_HEADS, DV)
    beta = jax.nn.sigmoid(b_fb.astype(f32)).reshape(bsz, t, 2, NV_HEADS)
    g = -jnp.exp(a_log.astype(f32)) * jax.nn.softplus(a_fb.astype(f32).reshape(bsz, t, 2, NV_HEADS) + dt_bias.astype(f32))
    o_f, s_f = gated_delta_chunked(q, k, v, g[:, :, 0], beta[:, :, 0], s_f.astype(f32))
    o_b, s_b = gated_delta_chunked(jnp.flip(q, 1), jnp.flip(k, 1), jnp.flip(v, 1),
                                   jnp.flip(g[:, :, 1], 1), jnp.flip(beta[:, :, 1], 1), s_b.astype(f32))
    o = o_f + jnp.flip(o_b, 1)
    o = (o * lax.rsqrt(jnp.mean(o * o, -1, keepdims=True) + EPS) * norm_w.astype(f32)
         * jax.nn.silu(z.astype(f32).reshape(bsz, t, NV_HEADS, DV)))
    return o.reshape(bsz, t, V_W).astype(dt) @ w_out, s_f, s_b


def routed_experts(h, router_w, router_b, w_gu, b_gu, w_down, b_down):
    n_tok, d = h.shape
    n_assign = n_tok * TOP_K
    logits = (h @ router_w).astype(jnp.float32) + router_b.astype(jnp.float32)
    top_logit, top_e = lax.top_k(logits, TOP_K)
    top_p = jax.nn.softmax(top_logit, axis=-1)
    flat_e = top_e.reshape(-1)
    order = jnp.argsort(flat_e)
    sorted_e = flat_e[order]
    counts = jnp.bincount(flat_e, length=N_EXPERTS)
    padded = (counts + MOE_BLOCK - 1) // MOE_BLOCK * MOE_BLOCK
    pad_end = jnp.cumsum(padded)
    pad_start = pad_end - padded
    grp_start = jnp.cumsum(counts) - counts
    slot = pad_start[sorted_e] + jnp.arange(n_assign) - grp_start[sorted_e]
    n_blocks = -(-(n_assign + N_EXPERTS * (MOE_BLOCK - 1)) // MOE_BLOCK)
    n_slots = n_blocks * MOE_BLOCK
    slot_tok = jnp.full((n_slots,), n_tok, jnp.int32).at[slot].set((order // TOP_K).astype(jnp.int32))
    slot_p = jnp.zeros((n_slots,), jnp.float32).at[slot].set(top_p.reshape(-1)[order])
    block_e = jnp.minimum(jnp.searchsorted(pad_end, jnp.arange(n_blocks) * MOE_BLOCK, side='right'), N_EXPERTS - 1)
    h_pad = jnp.concatenate([h, jnp.zeros((1, d), h.dtype)], axis=0)
    xb = h_pad[slot_tok].reshape(n_blocks, MOE_BLOCK, d)
    def expert_block(args):
        xblk, e = args
        gu = xblk @ w_gu[e] + b_gu[e]
        gate = jnp.minimum(gu[:, :D_FF], SWIGLU_LIMIT)
        up = jnp.clip(gu[:, D_FF:], -SWIGLU_LIMIT, SWIGLU_LIMIT)
        act = (up + 1.0) * gate * jax.nn.sigmoid(SWIGLU_ALPHA * gate)
        return act @ w_down[e] + b_down[e]
    yb = lax.map(expert_block, (xb, block_e))
    y = yb.reshape(n_slots, d) * slot_p[:, None].astype(h.dtype)
    return jax.ops.segment_sum(y, slot_tok, num_segments=n_tok + 1)[:n_tok]


def trunk_layer(x, mod, s_f, s_b, w_in, conv_qkv, a_log, dt_bias, norm_o, w_a_out, conv_b, w_b_out, w_o,
                ln1_g, ln1_b, router_w, router_b, w_gu, b_gu, w_down, b_down, ln2_g, ln2_b):
    shift1, scale1, gate1, shift2, scale2, gate2 = jnp.split(mod, 6, axis=-1)
    h = x * (1.0 + scale1) + shift1
    proj = h @ w_in
    offs = tuple(int(o) for o in np.cumsum(IN_SIZES)[:-1])
    qkv, z, b_fb, a_fb, u_b, u_c, u_x, r_a, r_b = jnp.split(proj, offs, axis=-1)
    y_a, s_f, s_b = deltanet_branch(qkv, z, b_fb, a_fb, conv_qkv, a_log, dt_bias, norm_o, w_a_out, s_f, s_b)
    y_b = (u_b * dwconv_centred(u_c * u_x, conv_b)) @ w_b_out
    mixed = (jax.nn.sigmoid(r_a) * y_a + jax.nn.sigmoid(r_b) * y_b) @ w_o
    x = layer_norm(DEEPNORM_ALPHA * x + gate1 * mixed, ln1_g, ln1_b)
    bsz, t, d = x.shape
    h2 = x * (1.0 + scale2) + shift2
    ff = routed_experts(h2.reshape(bsz * t, d), router_w, router_b, w_gu, b_gu, w_down, b_down).reshape(bsz, t, d)
    x = layer_norm(DEEPNORM_ALPHA * x + gate2 * ff, ln2_g, ln2_b)
    return x, s_f, s_b


def setup_inputs(seed: int = 0) -> dict:
    key = jax.random.key(seed)
    ks = jax.random.split(key, 32)
    f32 = jnp.float32
    def nrm(k, shape, s):
        return jax.random.normal(k, shape, f32) * s
    dt = jnp.exp(jax.random.uniform(ks[11], (DEPTH, 2, NV_HEADS), f32, math.log(1e-3), math.log(1e-1)))
    return {
        'x_prompt': nrm(ks[0], (BATCH, SEQ, D_MODEL), 1.0),
        'x_sample': nrm(ks[1], (DEC_BATCH, DEC_SEQ, D_MODEL), 1.0),
        'c': nrm(ks[2], (DEC_BATCH, D_MODEL), 1.0),
        'state_fwd': nrm(ks[3], (DEC_BATCH, DEPTH, NV_HEADS, DK, DV), 0.3),
        'state_bwd': nrm(ks[4], (DEC_BATCH, DEPTH, NV_HEADS, DK, DV), 0.3),
        'c_ctx': nrm(ks[5], (D_MODEL,), 1.0),
        'w_mod': nrm(ks[6], (DEPTH, D_MODEL, 6 * D_MODEL), D_MODEL ** -0.5),
        'b_mod': nrm(ks[7], (DEPTH, 6 * D_MODEL), 0.02),
        'w_in': nrm(ks[8], (DEPTH, D_MODEL, IN_W), D_MODEL ** -0.5),
        'conv_qkv': nrm(ks[9], (DEPTH, QKV_CONV, QKV_W), QKV_CONV ** -0.5),
        'a_log': jnp.log(jax.random.uniform(ks[10], (DEPTH, 2, NV_HEADS), f32, 1.0, 16.0)),
        'dt_bias': dt + jnp.log(-jnp.expm1(-dt)),
        'norm_o': 1.0 + nrm(ks[12], (DEPTH, DV), 0.02),
        'w_a_out': nrm(ks[13], (DEPTH, V_W, D_MODEL), V_W ** -0.5),
        'conv_b': nrm(ks[14], (DEPTH, GCONV, CONV_CH), GCONV ** -0.5),
        'w_b_out': nrm(ks[15], (DEPTH, CONV_CH, D_MODEL), CONV_CH ** -0.5),
        'w_o': nrm(ks[16], (DEPTH, D_MODEL, D_MODEL), DEEPNORM_BETA * D_MODEL ** -0.5),
        'ln1_g': 1.0 + nrm(ks[17], (DEPTH, D_MODEL), 0.02),
        'ln1_b': nrm(ks[18], (DEPTH, D_MODEL), 0.02),
        'router_w': nrm(ks[19], (DEPTH, D_MODEL, N_EXPERTS), D_MODEL ** -0.5),
        'router_b': nrm(ks[20], (DEPTH, N_EXPERTS), 0.01),
        'w_gu': nrm(ks[21], (DEPTH, N_EXPERTS, D_MODEL, 2 * D_FF), D_MODEL ** -0.5),
        'b_gu': nrm(ks[22], (DEPTH, N_EXPERTS, 2 * D_FF), 0.02),
        'w_down': nrm(ks[23], (DEPTH, N_EXPERTS, D_FF, D_MODEL), DEEPNORM_BETA * D_FF ** -0.5),
        'b_down': nrm(ks[24], (DEPTH, N_EXPERTS, D_MODEL), 0.02),
        'ln2_g': 1.0 + nrm(ks[25], (DEPTH, D_MODEL), 0.02),
        'ln2_b': nrm(ks[26], (DEPTH, D_MODEL), 0.02),
    }


def reference(x_prompt, x_sample, c, state_fwd, state_bwd, c_ctx, w_mod, b_mod, w_in, conv_qkv, a_log, dt_bias,
              norm_o, w_a_out, conv_b, w_b_out, w_o, ln1_g, ln1_b, router_w, router_b, w_gu, b_gu, w_down, b_down,
              ln2_g, ln2_b):
    zero_state = jnp.zeros((x_prompt.shape[0], NV_HEADS, DK, DV), jnp.float32)
    rows = x_sample.shape[1] // GRID_W
    xp = x_prompt
    xs = x_sample + grid_pos_embed(rows, D_MODEL).astype(x_sample.dtype)[None]
    new_f = []
    new_b = []
    for l in range(DEPTH):
        mod_ctx = (jax.nn.silu(c_ctx) @ w_mod[l] + b_mod[l])[None, None, :]
        mod_lat = (jax.nn.silu(c) @ w_mod[l] + b_mod[l])[:, None, :]
        lw = (w_in[l], conv_qkv[l], a_log[l], dt_bias[l], norm_o[l], w_a_out[l], conv_b[l], w_b_out[l], w_o[l],
              ln1_g[l], ln1_b[l], router_w[l], router_b[l], w_gu[l], b_gu[l], w_down[l], b_down[l], ln2_g[l], ln2_b[l])
        xp, sf, sb = trunk_layer(xp, mod_ctx, zero_state, zero_state, *lw)
        new_f.append(sf.astype(x_prompt.dtype))
        new_b.append(sb.astype(x_prompt.dtype))
        xs, _, _ = trunk_layer(xs, mod_lat, state_fwd[:, l], state_bwd[:, l], *lw)
    new_state_fwd = jnp.stack(new_f, axis=1)
    new_state_bwd = jnp.stack(new_b, axis=1)
    return (xp, xs, new_state_fwd, new_state_bwd)
```

```python
import functools
import math

import jax
import jax.numpy as jnp
import numpy as np
from jax import lax
from jax.experimental import pallas as pl
from jax.experimental.pallas import tpu as pltpu

F32 = jnp.float32
BF16 = jnp.bfloat16

D_MODEL = 2048
GRID_W = 64
NK_HEADS = 16
NV_HEADS = 32
DK = 128
DV = 128
QK_W = NK_HEADS * DK
V_W = NV_HEADS * DV
QKV_W = 2 * QK_W + V_W
CHUNK = 64
CONV_CH = D_MODEL
IN_SIZES = (QKV_W, V_W, 2 * NV_HEADS, 2 * NV_HEADS, CONV_CH, CONV_CH, CONV_CH, D_MODEL, D_MODEL)
N_EXPERTS = 32
TOP_K = 4
D_FF = D_MODEL
SWIGLU_LIMIT = 7.0
SWIGLU_ALPHA = 1.702
DEPTH = 1
DEEPNORM_ALPHA = (2 * DEPTH) ** 0.25
EPS = 1e-6
POS_BASE = 10000.0

VMEM_LIMIT_BYTES = 56 * 1024 * 1024


def _mm_body(x_ref, w_ref, o_ref):
    o_ref[...] = jnp.dot(x_ref[...].astype(BF16), w_ref[...].astype(BF16), preferred_element_type=F32)


def _mm(x, w, *, tm, tn):
    m, k = x.shape
    n = w.shape[1]
    assert m % tm == 0 and n % tn == 0
    return pl.pallas_call(
        _mm_body,
        grid=(m // tm, n // tn),
        in_specs=[pl.BlockSpec((tm, k), lambda i, j: (i, 0)),
                  pl.BlockSpec((k, tn), lambda i, j: (0, j))],
        out_specs=pl.BlockSpec((tm, tn), lambda i, j: (i, j)),
        out_shape=jax.ShapeDtypeStruct((m, n), F32),
        compiler_params=pltpu.CompilerParams(dimension_semantics=("parallel", "parallel"),
                                             vmem_limit_bytes=VMEM_LIMIT_BYTES),
    )(x, w)


MOE_TM = 512
MOE_TF = 512


def _moe_body(be_ref, nv_ref, x_ref, wg_ref, wu_ref, bg_ref, bu_ref, wd_ref, bd_ref, p_ref, o_ref, acc_ref):
    i = pl.program_id(0)
    f = pl.program_id(1)
    valid = i < nv_ref[0]

    @pl.when(f == 0)
    def _():
        acc_ref[...] = jnp.zeros_like(acc_ref)

    @pl.when(valid)
    def _():
        x = x_ref[...]
        g = jnp.dot(x, wg_ref[...].astype(BF16), preferred_element_type=F32) + bg_ref[...]
        u = jnp.dot(x, wu_ref[...].astype(BF16), preferred_element_type=F32) + bu_ref[...]
        gate = jnp.minimum(g, SWIGLU_LIMIT)
        up = jnp.clip(u, -SWIGLU_LIMIT, SWIGLU_LIMIT)
        act = (up + 1.0) * gate * jax.nn.sigmoid(SWIGLU_ALPHA * gate)
        acc_ref[...] += jnp.dot(act.astype(BF16), wd_ref[...].astype(BF16), preferred_element_type=F32)

    @pl.when(f == pl.num_programs(1) - 1)
    def _():
        o_ref[...] = jnp.where(valid, (acc_ref[...] + bd_ref[...]) * p_ref[...], 0.0)


def _moe_blocks(xs, slot_p, block_e, n_valid, w_gu, b_gu, w_down, b_down):
    n_slots, d = xs.shape
    n_blocks = n_slots // MOE_TM
    nf = D_FF // MOE_TF

    def f_eff(i, f, nv):
        return jnp.where(i < nv[0], f, nf - 1)

    grid_spec = pltpu.PrefetchScalarGridSpec(
        num_scalar_prefetch=2,
        grid=(n_blocks, nf),
        in_specs=[
            pl.BlockSpec((MOE_TM, d), lambda i, f, be, nv: (i, 0)),
            pl.BlockSpec((None, d, MOE_TF), lambda i, f, be, nv: (be[i], 0, f_eff(i, f, nv))),
            pl.BlockSpec((None, d, MOE_TF), lambda i, f, be, nv: (be[i], 0, nf + f_eff(i, f, nv))),
            pl.BlockSpec((None, 1, MOE_TF), lambda i, f, be, nv: (be[i], 0, f_eff(i, f, nv))),
            pl.BlockSpec((None, 1, MOE_TF), lambda i, f, be, nv: (be[i], 0, nf + f_eff(i, f, nv))),
            pl.BlockSpec((None, MOE_TF, d), lambda i, f, be, nv: (be[i], f_eff(i, f, nv), 0)),
            pl.BlockSpec((None, 1, d), lambda i, f, be, nv: (be[i], 0, 0)),
            pl.BlockSpec((MOE_TM, 1), lambda i, f, be, nv: (i, 0)),
        ],
        out_specs=pl.BlockSpec((MOE_TM, d), lambda i, f, be, nv: (i, 0)),
        scratch_shapes=[pltpu.VMEM((MOE_TM, d), F32)],
    )
    return pl.pallas_call(
        _moe_body,
        grid_spec=grid_spec,
        out_shape=jax.ShapeDtypeStruct((n_slots, d), F32),
        compiler_params=pltpu.CompilerParams(dimension_semantics=("arbitrary", "arbitrary"),
                                             vmem_limit_bytes=VMEM_LIMIT_BYTES),
    )(block_e, n_valid, xs, w_gu, w_gu, b_gu[:, None, :], b_gu[:, None, :], w_down, b_down[:, None, :],
      slot_p[:, None])


def _routed_experts(h, router_w, router_b, w_gu, b_gu, w_down, b_down):
    n_tok, d = h.shape
    n_assign = n_tok * TOP_K
    logits = jnp.dot(h, router_w, precision=lax.Precision.HIGHEST) + router_b
    top_logit, top_e = lax.top_k(logits, TOP_K)
    top_p = jax.nn.softmax(top_logit, axis=-1)
    flat_e = top_e.reshape(-1)
    onehot = (flat_e[:, None] == jnp.arange(N_EXPERTS)[None, :]).astype(jnp.int32)
    csum = jnp.cumsum(onehot, axis=0)
    counts = csum[-1]
    rank = jnp.take_along_axis(csum, flat_e[:, None], axis=1)[:, 0] - 1
    padded = (counts + MOE_TM - 1) // MOE_TM * MOE_TM
    pad_end = jnp.cumsum(padded)
    pad_start = pad_end - padded
    slot = pad_start[flat_e] + rank
    n_blocks = -(-(n_assign + N_EXPERTS * (MOE_TM - 1)) // MOE_TM)
    n_slots = n_blocks * MOE_TM
    slot_tok = jnp.full((n_slots,), n_tok, jnp.int32).at[slot].set(jnp.arange(n_assign, dtype=jnp.int32) // TOP_K)
    slot_p = jnp.zeros((n_slots,), F32).at[slot].set(top_p.reshape(-1))
    n_valid = (pad_end[-1] // MOE_TM).astype(jnp.int32)
    blk = jnp.arange(n_blocks, dtype=jnp.int32)
    block_e = jnp.searchsorted(pad_end, jnp.minimum(blk, n_valid - 1) * MOE_TM, side='right').astype(jnp.int32)
    block_e = jnp.minimum(block_e, N_EXPERTS - 1)
    h_pad = jnp.concatenate([h.astype(BF16), jnp.zeros((1, d), BF16)], axis=0)
    xs = h_pad[slot_tok]
    y = _moe_blocks(xs, slot_p, block_e, n_valid.reshape(1), w_gu, b_gu, w_down, b_down)
    return y[slot.reshape(n_tok, TOP_K)].sum(axis=1)


def _layer_norm(x, g, b):
    mu = x.mean(-1, keepdims=True)
    var = jnp.square(x - mu).mean(-1, keepdims=True)
    return (x - mu) * lax.rsqrt(var + EPS) * g + b


def _l2norm(t):
    return t * lax.rsqrt(jnp.sum(t * t, -1, keepdims=True) + EPS)


def _dwconv3(x, w):
    xm = jnp.pad(x, ((0, 0), (1, 0), (0, 0)))[:, :-1]
    xp = jnp.pad(x, ((0, 0), (0, 1), (0, 0)))[:, 1:]
    return xm * w[0] + x * w[1] + xp * w[2]


def _grid_pos_embed(rows, dim):
    quarter = dim // 4
    freqs = 1.0 / (POS_BASE ** (jnp.arange(quarter, dtype=F32) / quarter))

    def axis_embed(n):
        ang = jnp.arange(n, dtype=F32)[:, None] * freqs[None, :]
        return jnp.concatenate([jnp.sin(ang), jnp.cos(ang)], axis=-1)
    er = axis_embed(rows)
    ec = axis_embed(GRID_W)
    pos = jnp.concatenate([jnp.broadcast_to(er[:, None, :], (rows, GRID_W, dim // 2)),
                           jnp.broadcast_to(ec[None, :, :], (rows, GRID_W, dim // 2))], axis=-1)
    return pos.reshape(rows * GRID_W, dim)


def _gated_delta_chunked(q, k, v, g, beta, s0):
    bsz, t, h, _ = q.shape
    n = t // CHUNK

    def chunks(a):
        return a.reshape((bsz, n, CHUNK, h) + a.shape[3:]).transpose((1, 0, 3, 2) + tuple(range(4, a.ndim + 1)))
    qc = chunks(q * DK ** -0.5)
    kc = chunks(k)
    vc = chunks(v)
    gc = jnp.cumsum(chunks(g), axis=-1)
    bc = chunks(beta)
    causal = jnp.tril(jnp.ones((CHUNK, CHUNK), bool))
    strict = jnp.tril(jnp.ones((CHUNK, CHUNK), bool), -1)
    decay = jnp.exp(jnp.where(causal, gc[..., :, None] - gc[..., None, :], -jnp.inf))
    kb = kc * bc[..., None]
    lmat = jnp.where(strict, jnp.einsum('nbhid,nbhjd->nbhij', kb, kc) * decay, 0.0)
    rhs = jnp.concatenate([vc * bc[..., None], kb * jnp.exp(gc)[..., None]], axis=-1)
    sol = lax.linalg.triangular_solve(lmat, rhs, left_side=True, lower=True, unit_diagonal=True)
    u, w = sol[..., :DV], sol[..., DV:]
    qk = jnp.where(causal, jnp.einsum('nbhid,nbhjd->nbhij', qc, kc) * decay, 0.0)

    def step(s, inp):
        q_i, k_i, u_i, w_i, g_i, a_i = inp
        v_new = u_i - jnp.einsum('bhck,bhkv->bhcv', w_i, s)
        o = (jnp.einsum('bhck,bhkv->bhcv', q_i * jnp.exp(g_i)[..., None], s)
             + jnp.einsum('bhij,bhjv->bhiv', a_i, v_new))
        g_last = g_i[..., -1:]
        s = (s * jnp.exp(g_last)[..., None]
             + jnp.einsum('bhck,bhcv->bhkv', k_i * jnp.exp(g_last - g_i)[..., None], v_new))
        return s, o
    s, o = lax.scan(step, s0, (qc, kc, u, w, gc, qk))
    return o.transpose(1, 0, 3, 2, 4).reshape(bsz, t, h, -1), s


def _deltanet_core(qkv, z, b_fb, a_fb, conv_w, a_log, dt_bias, norm_w, s_f, s_b):
    bsz, t, _ = qkv.shape
    qkv = jax.nn.silu(_dwconv3(qkv, conv_w))
    rep = NV_HEADS // NK_HEADS
    q = jnp.repeat(_l2norm(qkv[..., :QK_W].reshape(bsz, t, NK_HEADS, DK)), rep, axis=2)
    k = jnp.repeat(_l2norm(qkv[..., QK_W:2 * QK_W].reshape(bsz, t, NK_HEADS, DK)), rep, axis=2)
    v = qkv[..., 2 * QK_W:].reshape(bsz, t, NV_HEADS, DV)
    beta = jax.nn.sigmoid(b_fb).reshape(bsz, t, 2, NV_HEADS)
    g = -jnp.exp(a_log) * jax.nn.softplus(a_fb.reshape(bsz, t, 2, NV_HEADS) + dt_bias)
    o_f, s_f = _gated_delta_chunked(q, k, v, g[:, :, 0], beta[:, :, 0], s_f)
    o_b, s_b = _gated_delta_chunked(jnp.flip(q, 1), jnp.flip(k, 1), jnp.flip(v, 1),
                                    jnp.flip(g[:, :, 1], 1), jnp.flip(beta[:, :, 1], 1), s_b)
    o = o_f + jnp.flip(o_b, 1)
    o = (o * lax.rsqrt(jnp.mean(o * o, -1, keepdims=True) + EPS) * norm_w
         * jax.nn.silu(z.reshape(bsz, t, NV_HEADS, DV)))
    return o.reshape(bsz, t, V_W), s_f, s_b


def kernel(x_prompt, x_sample, c, state_fwd, state_bwd, c_ctx, w_mod, b_mod, w_in, conv_qkv, a_log, dt_bias,
           norm_o, w_a_out, conv_b, w_b_out, w_o, ln1_g, ln1_b, router_w, router_b, w_gu, b_gu, w_down, b_down,
           ln2_g, ln2_b):
    nb_p, t_p, d = x_prompt.shape
    nb_s, t_s, _ = x_sample.shape
    n_p = nb_p * t_p
    n_s = nb_s * t_s
    rows = t_s // GRID_W
    xs_in = x_sample + _grid_pos_embed(rows, d)[None]

    l = 0
    cvec = jnp.concatenate([c_ctx[None, :], c], axis=0)
    cpad = jnp.zeros((8, d), F32).at[:1 + nb_s].set(jax.nn.silu(cvec))
    mod = _mm(cpad, w_mod[l], tm=8, tn=512)[:1 + nb_s] + b_mod[l]
    grp = jnp.concatenate([jnp.zeros((n_p,), jnp.int32), 1 + jnp.arange(n_s, dtype=jnp.int32) // t_s])
    x_all = jnp.concatenate([x_prompt.reshape(n_p, d), xs_in.reshape(n_s, d)], axis=0)
    n_all = n_p + n_s
    mod_t = mod[grp]
    shift1, scale1, gate1, shift2, scale2, gate2 = jnp.split(mod_t, 6, axis=-1)

    h = (x_all * (1.0 + scale1) + shift1).astype(BF16)
    proj = _mm(h, w_in[l], tm=2048, tn=384)
    offs = tuple(int(o) for o in np.cumsum(IN_SIZES)[:-1])
    qkv, z, b_fb, a_fb, u_b, u_c, u_x, r_a, r_b = jnp.split(proj, offs, axis=-1)

    def branch(sl, bsz, t, s_f, s_b):
        return _deltanet_core(qkv[sl].reshape(bsz, t, -1), z[sl].reshape(bsz, t, -1), b_fb[sl].reshape(bsz, t, -1),
                              a_fb[sl].reshape(bsz, t, -1), conv_qkv[l], a_log[l], dt_bias[l], norm_o[l], s_f, s_b)
    zero_state = jnp.zeros((nb_p, NV_HEADS, DK, DV), F32)
    o_p, sf, sb = branch(slice(0, n_p), nb_p, t_p, zero_state, zero_state)
    o_s, _, _ = branch(slice(n_p, n_all), nb_s, t_s, state_fwd[:, l], state_bwd[:, l])
    o_all = jnp.concatenate([o_p.reshape(n_p, V_W), o_s.reshape(n_s, V_W)], axis=0)
    y_a = _mm(o_all, w_a_out[l], tm=1024, tn=512)

    ucx = u_c * u_x
    cv = jnp.concatenate([_dwconv3(ucx[:n_p].reshape(nb_p, t_p, -1), conv_b[l]).reshape(n_p, -1),
                          _dwconv3(ucx[n_p:].reshape(nb_s, t_s, -1), conv_b[l]).reshape(n_s, -1)], axis=0)
    y_b = _mm(u_b * cv, w_b_out[l], tm=1024, tn=512)
    mixed = _mm(jax.nn.sigmoid(r_a) * y_a + jax.nn.sigmoid(r_b) * y_b, w_o[l], tm=1024, tn=512)
    x1 = _layer_norm(DEEPNORM_ALPHA * x_all + gate1 * mixed, ln1_g[l], ln1_b[l])
    h2 = x1 * (1.0 + scale2) + shift2
    ff = _routed_experts(h2, router_w[l], router_b[l], w_gu[l], b_gu[l], w_down[l], b_down[l])
    x2 = _layer_norm(DEEPNORM_ALPHA * x1 + gate2 * ff, ln2_g[l], ln2_b[l])
    y_prompt = x2[:n_p].reshape(nb_p, t_p, d)
    y_sample = x2[n_p:].reshape(nb_s, t_s, d)
    return (y_prompt, y_sample, sf[:, None], sb[:, None])
```

```python
import functools

import jax
import jax.numpy as jnp
import numpy as np
from jax import lax
from jax.experimental import pallas as pl
from jax.experimental.pallas import tpu as pltpu

F32 = jnp.float32
BF16 = jnp.bfloat16

D_MODEL = 2048
GRID_W = 64
NK_HEADS = 16
NV_HEADS = 32
DK = 128
DV = 128
QK_W = NK_HEADS * DK
V_W = NV_HEADS * DV
QKV_W = 2 * QK_W + V_W
CHUNK = 64
CONV_CH = D_MODEL
IN_SIZES = (QKV_W, V_W, 2 * NV_HEADS, 2 * NV_HEADS, CONV_CH, CONV_CH, CONV_CH, D_MODEL, D_MODEL)
N_EXPERTS = 32
TOP_K = 4
D_FF = D_MODEL
SWIGLU_LIMIT = 7.0
SWIGLU_ALPHA = 1.702
DEPTH = 1
DEEPNORM_ALPHA = (2 * DEPTH) ** 0.25
EPS = 1e-6
POS_BASE = 10000.0

VMEM_LIMIT_BYTES = 56 * 1024 * 1024


def _mm_body(x_ref, w_ref, o_ref):
    o_ref[...] = jnp.dot(x_ref[...].astype(BF16), w_ref[...].astype(BF16), preferred_element_type=F32)


def _mm(x, w, *, tm, tn):
    m, k = x.shape
    n = w.shape[1]
    assert m % tm == 0 and n % tn == 0
    return pl.pallas_call(
        _mm_body,
        grid=(m // tm, n // tn),
        in_specs=[pl.BlockSpec((tm, k), lambda i, j: (i, 0)),
                  pl.BlockSpec((k, tn), lambda i, j: (0, j))],
        out_specs=pl.BlockSpec((tm, tn), lambda i, j: (i, j)),
        out_shape=jax.ShapeDtypeStruct((m, n), F32),
        compiler_params=pltpu.CompilerParams(dimension_semantics=("parallel", "parallel"),
                                             vmem_limit_bytes=VMEM_LIMIT_BYTES),
    )(x, w)


MOE_TM = 512
MOE_TF = 512


def _moe_body(be_ref, nv_ref, x_ref, wg_ref, wu_ref, bg_ref, bu_ref, wd_ref, bd_ref, p_ref, o_ref, acc_ref):
    i = pl.program_id(0)
    f = pl.program_id(1)
    valid = i < nv_ref[0]

    @pl.when(f == 0)
    def _():
        acc_ref[...] = jnp.zeros_like(acc_ref)

    @pl.when(valid)
    def _():
        x = x_ref[...]
        g = jnp.dot(x, wg_ref[...].astype(BF16), preferred_element_type=F32) + bg_ref[...]
        u = jnp.dot(x, wu_ref[...].astype(BF16), preferred_element_type=F32) + bu_ref[...]
        gate = jnp.minimum(g, SWIGLU_LIMIT)
        up = jnp.clip(u, -SWIGLU_LIMIT, SWIGLU_LIMIT)
        act = (up + 1.0) * gate * jax.nn.sigmoid(SWIGLU_ALPHA * gate)
        acc_ref[...] += jnp.dot(act.astype(BF16), wd_ref[...].astype(BF16), preferred_element_type=F32)

    @pl.when(f == pl.num_programs(1) - 1)
    def _():
        o_ref[...] = jnp.where(valid, (acc_ref[...] + bd_ref[...]) * p_ref[...], 0.0)


def _moe_blocks(xs, slot_p, block_e, n_valid, w_gu, b_gu, w_down, b_down):
    n_slots, d = xs.shape
    n_blocks = n_slots // MOE_TM
    nf = D_FF // MOE_TF

    def f_eff(i, f, nv):
        return jnp.where(i < nv[0], f, nf - 1)

    grid_spec = pltpu.PrefetchScalarGridSpec(
        num_scalar_prefetch=2,
        grid=(n_blocks, nf),
        in_specs=[
            pl.BlockSpec((MOE_TM, d), lambda i, f, be, nv: (i, 0)),
            pl.BlockSpec((None, d, MOE_TF), lambda i, f, be, nv: (be[i], 0, f_eff(i, f, nv))),
            pl.BlockSpec((None, d, MOE_TF), lambda i, f, be, nv: (be[i], 0, nf + f_eff(i, f, nv))),
            pl.BlockSpec((None, 1, MOE_TF), lambda i, f, be, nv: (be[i], 0, f_eff(i, f, nv))),
            pl.BlockSpec((None, 1, MOE_TF), lambda i, f, be, nv: (be[i], 0, nf + f_eff(i, f, nv))),
            pl.BlockSpec((None, MOE_TF, d), lambda i, f, be, nv: (be[i], f_eff(i, f, nv), 0)),
            pl.BlockSpec((None, 1, d), lambda i, f, be, nv: (be[i], 0, 0)),
            pl.BlockSpec((MOE_TM, 1), lambda i, f, be, nv: (i, 0)),
        ],
        out_specs=pl.BlockSpec((MOE_TM, d), lambda i, f, be, nv: (i, 0)),
        scratch_shapes=[pltpu.VMEM((MOE_TM, d), F32)],
    )
    return pl.pallas_call(
        _moe_body,
        grid_spec=grid_spec,
        out_shape=jax.ShapeDtypeStruct((n_slots, d), F32),
        compiler_params=pltpu.CompilerParams(dimension_semantics=("arbitrary", "arbitrary"),
                                             vmem_limit_bytes=VMEM_LIMIT_BYTES),
    )(block_e, n_valid, xs, w_gu, w_gu, b_gu[:, None, :], b_gu[:, None, :], w_down, b_down[:, None, :],
      slot_p[:, None])


def _routed_experts(h, router_w, router_b, w_gu, b_gu, w_down, b_down):
    n_tok, d = h.shape
    n_assign = n_tok * TOP_K
    logits = jnp.dot(h, router_w, precision=lax.Precision.HIGHEST) + router_b
    top_logit, top_e = lax.top_k(logits, TOP_K)
    top_p = jax.nn.softmax(top_logit, axis=-1)
    flat_e = top_e.reshape(-1)
    onehot = (flat_e[:, None] == jnp.arange(N_EXPERTS)[None, :]).astype(jnp.int32)
    csum = jnp.cumsum(onehot, axis=0)
    counts = csum[-1]
    rank = jnp.take_along_axis(csum, flat_e[:, None], axis=1)[:, 0] - 1
    padded = (counts + MOE_TM - 1) // MOE_TM * MOE_TM
    pad_end = jnp.cumsum(padded)
    pad_start = pad_end - padded
    slot = pad_start[flat_e] + rank
    n_blocks = -(-(n_assign + N_EXPERTS * (MOE_TM - 1)) // MOE_TM)
    n_slots = n_blocks * MOE_TM
    slot_tok = jnp.full((n_slots,), n_tok, jnp.int32).at[slot].set(jnp.arange(n_assign, dtype=jnp.int32) // TOP_K)
    slot_p = jnp.zeros((n_slots,), F32).at[slot].set(top_p.reshape(-1))
    n_valid = (pad_end[-1] // MOE_TM).astype(jnp.int32)
    blk = jnp.arange(n_blocks, dtype=jnp.int32)
    block_e = jnp.searchsorted(pad_end, jnp.minimum(blk, n_valid - 1) * MOE_TM, side='right').astype(jnp.int32)
    block_e = jnp.minimum(block_e, N_EXPERTS - 1)
    h_pad = jnp.concatenate([h.astype(BF16), jnp.zeros((1, d), BF16)], axis=0)
    xs = h_pad[slot_tok]
    y = _moe_blocks(xs, slot_p, block_e, n_valid.reshape(1), w_gu, b_gu, w_down, b_down)
    return y[slot.reshape(n_tok, TOP_K)].sum(axis=1)


PREP_ROWS = 4096


def _prep_body(x_ref, w_ref, o_ref, *, t_first, t_rest):
    i = pl.program_id(0)
    j = pl.program_id(1)
    t_seq = jnp.where(i == 0, t_first, t_rest)
    x = x_ref[...]
    n = x.shape[0]
    row = lax.broadcasted_iota(jnp.int32, x.shape, 0)
    tpos = row & (t_seq - 1)
    xm = jnp.where(tpos == 0, 0.0, pltpu.roll(x, 1, axis=0))
    xp = jnp.where(tpos == t_seq - 1, 0.0, pltpu.roll(x, n - 1, axis=0))
    w = w_ref[...]
    y = xm * w[0:1] + x * w[1:2] + xp * w[2:3]
    y = y * jax.nn.sigmoid(y)
    rs = lax.rsqrt(jnp.sum(y * y, axis=-1, keepdims=True) + EPS)
    nq = QK_W // DK
    fac = jnp.where(j < nq, rs * (DK ** -0.5), jnp.where(j < 2 * nq, rs, 1.0))
    o_ref[...] = y * fac


def _qkv_prep(proj, conv_w, *, t_first, t_rest):
    n = proj.shape[0]
    return pl.pallas_call(
        functools.partial(_prep_body, t_first=t_first, t_rest=t_rest),
        grid=(n // PREP_ROWS, QKV_W // DK),
        in_specs=[pl.BlockSpec((PREP_ROWS, DK), lambda i, j: (i, j)),
                  pl.BlockSpec((3, DK), lambda i, j: (0, j))],
        out_specs=pl.BlockSpec((PREP_ROWS, DK), lambda i, j: (i, j)),
        out_shape=jax.ShapeDtypeStruct((n, QKV_W), F32),
        compiler_params=pltpu.CompilerParams(dimension_semantics=("parallel", "parallel"),
                                             vmem_limit_bytes=VMEM_LIMIT_BYTES),
    )(proj, conv_w)


SOLVE_WAYS = 4


def _mmul(a, b):
    return jnp.dot(a, b, preferred_element_type=F32)


def _delta_body(q_ref, k_ref, v_ref, z_ref, cols_ref, rows_ref, nw_ref, s0f_ref, s0b_ref,
                o_ref, sf_ref, sb_ref, acc_ref, s_ref, wu_ref, a_ref, *, n_chunks, zero_init):
    lane = lax.broadcasted_iota(jnp.int32, (CHUNK, 2 * CHUNK), 1)
    row = lax.broadcasted_iota(jnp.int32, (CHUNK, 2 * CHUNK), 0)
    pos = lane & (CHUNK - 1)
    left = lane < CHUNK
    left1 = lax.broadcasted_iota(jnp.int32, (1, 2 * CHUNK), 1) < CHUNK
    eye2 = (pos == row).astype(F32)
    incl = (pos <= row, pos >= row)
    strict = (pos < row, pos > row)
    same8 = (row >> 3) == (pos >> 3)
    same16 = (row >> 4) == (pos >> 4)
    same32 = (row >> 5) == (pos >> 5)
    nt = (((1,), (1,)), ((), ()))
    tn = (((0,), (0,)), ((), ()))

    def hi_lo(m):
        hi = m.astype(BF16)
        return hi, (m - hi.astype(F32)).astype(BF16)

    def split(m):
        return jnp.where(left, m, 0.0), jnp.where(left, 0.0, m)

    def bd3(m):
        hi, lo = hi_lo(jnp.concatenate(split(m), axis=0))
        return jnp.concatenate([hi, hi, lo], axis=0)

    def mm3(a, b3):
        hi, lo = hi_lo(a)
        return _mmul(jnp.concatenate([hi, lo, hi], axis=1), b3)

    def inv_unit_tri(lms):
        d8 = [jnp.where(same8, lm, 0.0) for lm in lms]
        xs = [eye2 - d for d in d8]
        d2 = [mm3(d, bd3(d)) for d in d8]
        xs = [x + mm3(x, bd3(d)) for x, d in zip(xs, d2)]
        d4 = [mm3(d, bd3(d)) for d in d2]
        xs = [x + mm3(x, bd3(d)) for x, d in zip(xs, d4)]
        for keep in (same16 & ~same8, same32 & ~same16, ~same32):
            ys = [mm3(jnp.where(keep, lm, 0.0), bd3(x)) for lm, x in zip(lms, xs)]
            xs = [x - mm3(x, bd3(y)) for x, y in zip(xs, ys)]
        return xs

    def chunk_terms(c, d):
        r0 = pl.multiple_of(c * CHUNK, CHUNK)
        cols = cols_ref[pl.ds(r0, CHUNK), :]
        rows = rows_ref[c]
        gcol = [cols[:, 2 * d + h:2 * d + h + 1] for h in (0, 1)]
        bcol = [cols[:, 4 + 2 * d + h:4 + 2 * d + h + 1] for h in (0, 1)]
        gl = [rows[4 + 2 * d + h:5 + 2 * d + h] for h in (0, 1)]
        return r0, rows, gcol, bcol, gl

    acc_ref[...] = jnp.zeros_like(acc_ref)
    if zero_init:
        s_ref[...] = jnp.zeros_like(s_ref)
    else:
        s_ref[0] = s0f_ref[0]
        s_ref[1] = s0f_ref[1]
        s_ref[2] = s0b_ref[0]
        s_ref[3] = s0b_ref[1]

    def solve(i, carry):
        chunks = [i + part * (n_chunks // SOLVE_WAYS) for part in range(SOLVE_WAYS)]
        kcs, vcs, kkms, qkms = [], [], [], []
        for c in chunks:
            r0 = pl.multiple_of(c * CHUNK, CHUNK)
            kc = k_ref[pl.ds(r0, CHUNK), :]
            kb = kc.astype(BF16)
            kk2 = jnp.concatenate([kb, kb], axis=0)
            kcs.append(kc)
            vcs.append(v_ref[pl.ds(r0, CHUNK), :])
            kkms.append(lax.dot_general(kb, kk2, nt, preferred_element_type=F32))
            qkms.append(lax.dot_general(q_ref[pl.ds(r0, CHUNK), :].astype(BF16), kk2, nt,
                                        preferred_element_type=F32))
        chains = [(j, d) for j in range(SOLVE_WAYS) for d in (0, 1)]
        lms, rhs4s = [], []
        for j, d in chains:
            c = chunks[j]
            _, rows, gcol, bcol, _ = chunk_terms(c, d)
            gcol2 = jnp.where(left, gcol[0], gcol[1])
            bcol2 = jnp.where(left, bcol[0], bcol[1])
            grow2 = jnp.where(left1, rows[2 * d:2 * d + 1], rows[2 * d + 1:2 * d + 2])
            dm = jnp.exp(jnp.where(incl[d], gcol2 - grow2, -jnp.inf))
            a_ref[d, c] = (qkms[j] * dm).astype(BF16)
            lms.append(jnp.where(strict[d], kkms[j] * dm * bcol2, 0.0))
            rhs = [jnp.concatenate([vcs[j][:, h * DV:(h + 1) * DV] * bcol[h],
                                    kcs[j] * (bcol[h] * jnp.exp(gcol[h]))], axis=1) for h in (0, 1)]
            rhs2 = jnp.concatenate(rhs, axis=0).astype(BF16)
            rhs4s.append(jnp.concatenate([rhs2, rhs2], axis=0))
        t2s = inv_unit_tri(lms)
        sols = [[_mmul(jnp.concatenate(hi_lo(th), axis=1), rhs4) for th in split(t2)]
                for t2, rhs4 in zip(t2s, rhs4s)]
        for (j, d), sol in zip(chains, sols):
            wu_ref[d, chunks[j]] = jnp.concatenate(sol, axis=1).astype(BF16)
        return carry

    lax.fori_loop(0, n_chunks // SOLVE_WAYS, solve, 0)

    def step(i, carry):
        terms = [chunk_terms(i if d == 0 else n_chunks - 1 - i, d) for d in (0, 1)]
        aws, ngs, qcs = [], [], []
        for d in (0, 1):
            c = i if d == 0 else n_chunks - 1 - i
            r0, _, gcol, _, gl = terms[d]
            qcs.append(q_ref[pl.ds(r0, CHUNK), :])
            kb = k_ref[pl.ds(r0, CHUNK), :].astype(BF16)
            wu = wu_ref[d, c]
            wu2 = jnp.concatenate([wu[:, :2 * DV], wu[:, 2 * DV:]], axis=0)
            al, ar = split(a_ref[d, c].astype(F32))
            aws.append([_mmul(al.astype(BF16), wu2), _mmul(ar.astype(BF16), wu2)])
            ek = [jnp.exp(gl[h][:, 0:1] - gcol[h]) for h in (0, 1)]
            wuf = wu.astype(F32)
            zz = jnp.concatenate([wuf[:, :2 * DV] * ek[0], wuf[:, 2 * DV:] * ek[1]], axis=1).astype(BF16)
            ngs.append(lax.dot_general(kb, zz, tn, preferred_element_type=F32))
        streams = [(d, h) for d in (0, 1) for h in (0, 1)]
        outs = []
        for d, h in streams:
            p = qcs[d] * jnp.exp(terms[d][2][h]) - aws[d][h][:, DV:]
            pg = jnp.concatenate([p, ngs[d][:, (2 * h + 1) * DV:(2 * h + 2) * DV]], axis=0).astype(BF16)
            outs.append(_mmul(pg, s_ref[2 * d + h].astype(BF16)))
        for (d, h), out in zip(streams, outs):
            s = 2 * d + h
            r0, gl = terms[d][0], terms[d][4]
            acc_ref[pl.ds(r0, CHUNK), h * DV:(h + 1) * DV] += out[:CHUNK] + aws[d][h][:, :DV]
            s_ref[s] = s_ref[s] * jnp.exp(gl[h]) - out[CHUNK:] + ngs[d][:, 2 * h * DV:(2 * h + 1) * DV]
        return carry

    lax.fori_loop(0, n_chunks, step, 0)

    nw = nw_ref[...]

    def finish(j, carry):
        r0 = pl.multiple_of(j * CHUNK, CHUNK)
        o = acc_ref[pl.ds(r0, CHUNK), :]
        z = z_ref[pl.ds(r0, CHUNK), :]
        for h in (0, 1):
            oh = o[:, h * DV:(h + 1) * DV]
            zh = z[:, h * DV:(h + 1) * DV]
            y = oh * lax.rsqrt(jnp.mean(oh * oh, axis=-1, keepdims=True) + EPS) * nw * (zh * jax.nn.sigmoid(zh))
            o_ref[pl.ds(r0, CHUNK), h * DV:(h + 1) * DV] = y.astype(o_ref.dtype)
        return carry

    lax.fori_loop(0, n_chunks, finish, 0)
    sf_ref[0] = s_ref[0]
    sf_ref[1] = s_ref[1]
    sb_ref[0] = s_ref[2]
    sb_ref[1] = s_ref[3]


def _deltanet(qkvn, proj, cols, rows, norm_w, s0f, s0b, *, n_seq, t_seq, row_blk0, zero_init):
    n_chunks = t_seq // CHUNK
    nq = QK_W // DK
    z_blk0 = QKV_W // (2 * DV)
    rep = NV_HEADS // NK_HEADS
    in_specs = [
        pl.BlockSpec((t_seq, DK), lambda b, kh: (row_blk0 + b, kh)),
        pl.BlockSpec((t_seq, DK), lambda b, kh: (row_blk0 + b, nq + kh)),
        pl.BlockSpec((t_seq, rep * DV), lambda b, kh: (row_blk0 + b, nq + kh)),
        pl.BlockSpec((t_seq, rep * DV), lambda b, kh: (row_blk0 + b, z_blk0 + kh)),
        pl.BlockSpec((None, t_seq, 8), lambda b, kh: (kh, row_blk0 + b, 0)),
        pl.BlockSpec((None, n_chunks, 8, 2 * CHUNK), lambda b, kh: (kh, row_blk0 + b, 0, 0)),
        pl.BlockSpec((1, DV), lambda b, kh: (0, 0)),
        pl.BlockSpec((None, rep, DK, DV), lambda b, kh: (0 if zero_init else b, kh, 0, 0)),
        pl.BlockSpec((None, rep, DK, DV), lambda b, kh: (0 if zero_init else b, kh, 0, 0)),
    ]
    out_specs = [
        pl.BlockSpec((t_seq, rep * DV), lambda b, kh: (b, kh)),
        pl.BlockSpec((None, rep, DK, DV), lambda b, kh: (b, kh, 0, 0)),
        pl.BlockSpec((None, rep, DK, DV), lambda b, kh: (b, kh, 0, 0)),
    ]
    out_shape = [
        jax.ShapeDtypeStruct((n_seq * t_seq, V_W), BF16),
        jax.ShapeDtypeStruct((n_seq, NV_HEADS, DK, DV), F32),
        jax.ShapeDtypeStruct((n_seq, NV_HEADS, DK, DV), F32),
    ]
    return pl.pallas_call(
        functools.partial(_delta_body, n_chunks=n_chunks, zero_init=zero_init),
        grid=(n_seq, NK_HEADS),
        in_specs=in_specs,
        out_specs=out_specs,
        out_shape=out_shape,
        scratch_shapes=[pltpu.VMEM((t_seq, rep * DV), F32), pltpu.VMEM((2 * rep, DK, DV), F32),
                        pltpu.VMEM((2, n_chunks, CHUNK, 2 * rep * DV), BF16),
                        pltpu.VMEM((2, n_chunks, CHUNK, 2 * CHUNK), BF16)],
        compiler_params=pltpu.CompilerParams(dimension_semantics=("parallel", "parallel"),
                                             vmem_limit_bytes=VMEM_LIMIT_BYTES),
    )(qkvn, qkvn, qkvn, proj, cols, rows, norm_w.reshape(1, DV), s0f, s0b)


def _decay_tables(ab, a_log, dt_bias):
    n = ab.shape[0]
    nh2 = 2 * NV_HEADS
    rep = NV_HEADS // NK_HEADS
    beta = jax.nn.sigmoid(ab[:, :nh2])
    g = -jnp.exp(a_log.reshape(nh2)) * jax.nn.softplus(ab[:, nh2:] + dt_bias.reshape(nh2))
    gch = g.reshape(n // CHUNK, CHUNK, 2, NV_HEADS)
    gcf = jnp.cumsum(gch[:, :, 0], axis=1)
    gcb = jnp.flip(jnp.cumsum(jnp.flip(gch[:, :, 1], 1), axis=1), 1)
    gc = jnp.stack([gcf, gcb], axis=2)
    gl = jnp.stack([gcf[:, -1], gcb[:, 0]], axis=1)

    def per_kh(a):
        a = a.reshape(a.shape[:-2] + (2, NK_HEADS, rep))
        return jnp.moveaxis(a, -3, -2).reshape(a.shape[:-3] + (NK_HEADS, 2 * rep))
    gc_k = per_kh(gc)
    cols = jnp.concatenate([gc_k.reshape(n, NK_HEADS, 2 * rep), per_kh(beta.reshape(n, 2, NV_HEADS))], axis=-1)
    cols = jnp.transpose(cols, (1, 0, 2))
    gr = jnp.transpose(gc_k, (2, 0, 3, 1))
    gr = jnp.concatenate([gr, gr], axis=-1)
    glr = jnp.broadcast_to(jnp.transpose(per_kh(gl), (1, 0, 2))[..., None], gr.shape)
    rows = jnp.concatenate([gr, glr], axis=2)
    return cols, rows


def _layer_norm(x, g, b):
    mu = x.mean(-1, keepdims=True)
    var = jnp.square(x - mu).mean(-1, keepdims=True)
    return (x - mu) * lax.rsqrt(var + EPS) * g + b


def _dwconv3(x, w):
    xm = jnp.pad(x, ((0, 0), (1, 0), (0, 0)))[:, :-1]
    xp = jnp.pad(x, ((0, 0), (0, 1), (0, 0)))[:, 1:]
    return xm * w[0] + x * w[1] + xp * w[2]


def _grid_pos_embed(rows, dim):
    quarter = dim // 4
    freqs = 1.0 / (POS_BASE ** (jnp.arange(quarter, dtype=F32) / quarter))

    def axis_embed(n):
        ang = jnp.arange(n, dtype=F32)[:, None] * freqs[None, :]
        return jnp.concatenate([jnp.sin(ang), jnp.cos(ang)], axis=-1)
    er = axis_embed(rows)
    ec = axis_embed(GRID_W)
    pos = jnp.concatenate([jnp.broadcast_to(er[:, None, :], (rows, GRID_W, dim // 2)),
                           jnp.broadcast_to(ec[None, :, :], (rows, GRID_W, dim // 2))], axis=-1)
    return pos.reshape(rows * GRID_W, dim)


def kernel(x_prompt, x_sample, c, state_fwd, state_bwd, c_ctx, w_mod, b_mod, w_in, conv_qkv, a_log, dt_bias,
           norm_o, w_a_out, conv_b, w_b_out, w_o, ln1_g, ln1_b, router_w, router_b, w_gu, b_gu, w_down, b_down,
           ln2_g, ln2_b):
    nb_p, t_p, d = x_prompt.shape
    nb_s, t_s, _ = x_sample.shape
    n_p = nb_p * t_p
    n_s = nb_s * t_s
    rows = t_s // GRID_W
    xs_in = x_sample + _grid_pos_embed(rows, d)[None]

    l = 0
    cvec = jnp.concatenate([c_ctx[None, :], c], axis=0)
    cpad = jnp.zeros((8, d), F32).at[:1 + nb_s].set(jax.nn.silu(cvec))
    mod = _mm(cpad, w_mod[l], tm=8, tn=512)[:1 + nb_s] + b_mod[l]
    grp = jnp.concatenate([jnp.zeros((n_p,), jnp.int32), 1 + jnp.arange(n_s, dtype=jnp.int32) // t_s])
    x_all = jnp.concatenate([x_prompt.reshape(n_p, d), xs_in.reshape(n_s, d)], axis=0)
    mod_t = mod[grp]
    shift1, scale1, gate1, shift2, scale2, gate2 = jnp.split(mod_t, 6, axis=-1)

    h = (x_all * (1.0 + scale1) + shift1).astype(BF16)
    proj = _mm(h, w_in[l], tm=2048, tn=384)
    offs = tuple(int(o) for o in np.cumsum(IN_SIZES)[:-1])
    _, _, _, _, u_b, u_c, u_x, r_a, r_b = jnp.split(proj, offs, axis=-1)

    assert n_p == PREP_ROWS and t_s == PREP_ROWS
    qkvn = _qkv_prep(proj, conv_qkv[l], t_first=t_p, t_rest=t_s)
    ab0 = offs[1]
    cols, rws = _decay_tables(proj[:, ab0:ab0 + 4 * NV_HEADS], a_log[l], dt_bias[l])
    zero_state = jnp.zeros((1, NV_HEADS, DK, DV), F32)
    o_p, sf, sb = _deltanet(qkvn, proj, cols, rws, norm_o[l], zero_state, zero_state,
                            n_seq=nb_p, t_seq=t_p, row_blk0=0, zero_init=True)
    o_s, _, _ = _deltanet(qkvn, proj, cols, rws, norm_o[l], state_fwd[:, l], state_bwd[:, l],
                          n_seq=nb_s, t_seq=t_s, row_blk0=n_p // t_s, zero_init=False)
    o_all = jnp.concatenate([o_p, o_s], axis=0)
    y_a = _mm(o_all, w_a_out[l], tm=1024, tn=512)

    ucx = u_c * u_x
    cv = jnp.concatenate([_dwconv3(ucx[:n_p].reshape(nb_p, t_p, -1), conv_b[l]).reshape(n_p, -1),
                          _dwconv3(ucx[n_p:].reshape(nb_s, t_s, -1), conv_b[l]).reshape(n_s, -1)], axis=0)
    y_b = _mm(u_b * cv, w_b_out[l], tm=1024, tn=512)
    mixed = _mm(jax.nn.sigmoid(r_a) * y_a + jax.nn.sigmoid(r_b) * y_b, w_o[l], tm=1024, tn=512)
    x1 = _layer_norm(DEEPNORM_ALPHA * x_all + gate1 * mixed, ln1_g[l], ln1_b[l])
    h2 = x1 * (1.0 + scale2) + shift2
    ff = _routed_experts(h2, router_w[l], router_b[l], w_gu[l], b_gu[l], w_down[l], b_down[l])
    x2 = _layer_norm(DEEPNORM_ALPHA * x1 + gate2 * ff, ln2_g[l], ln2_b[l])
    y_prompt = x2[:n_p].reshape(nb_p, t_p, d)
    y_sample = x2[n_p:].reshape(nb_s, t_s, d)
    return (y_prompt, y_sample, sf[:, None], sb[:, None])
```

```python
import functools

import jax
import jax.numpy as jnp
import numpy as np
from jax import lax
from jax.experimental import pallas as pl
from jax.experimental.pallas import tpu as pltpu

F32 = jnp.float32
BF16 = jnp.bfloat16

D_MODEL = 2048
GRID_W = 64
NK_HEADS = 16
NV_HEADS = 32
DK = 128
DV = 128
QK_W = NK_HEADS * DK
V_W = NV_HEADS * DV
QKV_W = 2 * QK_W + V_W
CHUNK = 64
CONV_CH = D_MODEL
IN_SIZES = (QKV_W, V_W, 2 * NV_HEADS, 2 * NV_HEADS, CONV_CH, CONV_CH, CONV_CH, D_MODEL, D_MODEL)
N_EXPERTS = 32
TOP_K = 4
D_FF = D_MODEL
SWIGLU_LIMIT = 7.0
SWIGLU_ALPHA = 1.702
DEPTH = 1
DEEPNORM_ALPHA = (2 * DEPTH) ** 0.25
EPS = 1e-6
POS_BASE = 10000.0

VMEM_LIMIT_BYTES = 56 * 1024 * 1024


def _mm_body(x_ref, w_ref, o_ref):
    o_ref[...] = jnp.dot(x_ref[...].astype(BF16), w_ref[...].astype(BF16), preferred_element_type=F32)


def _mm(x, w, *, tm, tn):
    m, k = x.shape
    n = w.shape[1]
    assert m % tm == 0 and n % tn == 0
    return pl.pallas_call(
        _mm_body,
        name="matmul",
        grid=(m // tm, n // tn),
        in_specs=[pl.BlockSpec((tm, k), lambda i, j: (i, 0)),
                  pl.BlockSpec((k, tn), lambda i, j: (0, j))],
        out_specs=pl.BlockSpec((tm, tn), lambda i, j: (i, j)),
        out_shape=jax.ShapeDtypeStruct((m, n), F32),
        compiler_params=pltpu.CompilerParams(dimension_semantics=("parallel", "parallel"),
                                             vmem_limit_bytes=VMEM_LIMIT_BYTES),
    )(x, w)


MOE_TM = 512
MOE_TF = 512


def _moe_body(be_ref, nv_ref, x_ref, wg_ref, wu_ref, bg_ref, bu_ref, wd_ref, bd_ref, p_ref, o_ref, acc_ref):
    i = pl.program_id(0)
    f = pl.program_id(1)
    valid = i < nv_ref[0]

    @pl.when(f == 0)
    def _():
        acc_ref[...] = jnp.zeros_like(acc_ref)

    @pl.when(valid)
    def _():
        x = x_ref[...]
        g = jnp.dot(x, wg_ref[...].astype(BF16), preferred_element_type=F32) + bg_ref[...]
        u = jnp.dot(x, wu_ref[...].astype(BF16), preferred_element_type=F32) + bu_ref[...]
        gate = jnp.minimum(g, SWIGLU_LIMIT)
        up = jnp.clip(u, -SWIGLU_LIMIT, SWIGLU_LIMIT)
        act = (up + 1.0) * gate * jax.nn.sigmoid(SWIGLU_ALPHA * gate)
        acc_ref[...] += jnp.dot(act.astype(BF16), wd_ref[...].astype(BF16), preferred_element_type=F32)

    @pl.when(f == pl.num_programs(1) - 1)
    def _():
        o_ref[...] = jnp.where(valid, (acc_ref[...] + bd_ref[...]) * p_ref[...], 0.0)


def _moe_blocks(xs, slot_p, block_e, n_valid, w_gu, b_gu, w_down, b_down):
    n_slots, d = xs.shape
    n_blocks = n_slots // MOE_TM
    nf = D_FF // MOE_TF

    def f_eff(i, f, nv):
        return jnp.where(i < nv[0], f, nf - 1)

    grid_spec = pltpu.PrefetchScalarGridSpec(
        num_scalar_prefetch=2,
        grid=(n_blocks, nf),
        in_specs=[
            pl.BlockSpec((MOE_TM, d), lambda i, f, be, nv: (i, 0)),
            pl.BlockSpec((None, d, MOE_TF), lambda i, f, be, nv: (be[i], 0, f_eff(i, f, nv))),
            pl.BlockSpec((None, d, MOE_TF), lambda i, f, be, nv: (be[i], 0, nf + f_eff(i, f, nv))),
            pl.BlockSpec((None, 1, MOE_TF), lambda i, f, be, nv: (be[i], 0, f_eff(i, f, nv))),
            pl.BlockSpec((None, 1, MOE_TF), lambda i, f, be, nv: (be[i], 0, nf + f_eff(i, f, nv))),
            pl.BlockSpec((None, MOE_TF, d), lambda i, f, be, nv: (be[i], f_eff(i, f, nv), 0)),
            pl.BlockSpec((None, 1, d), lambda i, f, be, nv: (be[i], 0, 0)),
            pl.BlockSpec((MOE_TM, 1), lambda i, f, be, nv: (i, 0)),
        ],
        out_specs=pl.BlockSpec((MOE_TM, d), lambda i, f, be, nv: (i, 0)),
        scratch_shapes=[pltpu.VMEM((MOE_TM, d), F32)],
    )
    return pl.pallas_call(
        _moe_body,
        name="moe_experts",
        grid_spec=grid_spec,
        out_shape=jax.ShapeDtypeStruct((n_slots, d), F32),
        compiler_params=pltpu.CompilerParams(dimension_semantics=("arbitrary", "arbitrary"),
                                             vmem_limit_bytes=VMEM_LIMIT_BYTES),
    )(block_e, n_valid, xs, w_gu, w_gu, b_gu[:, None, :], b_gu[:, None, :], w_down, b_down[:, None, :],
      slot_p[:, None])


def _route(logits):
    n_tok = logits.shape[0]
    n_assign = n_tok * TOP_K
    top_logit, top_e = lax.top_k(logits, TOP_K)
    top_p = jax.nn.softmax(top_logit, axis=-1)
    flat_e = top_e.reshape(-1)
    onehot = (flat_e[:, None] == jnp.arange(N_EXPERTS)[None, :]).astype(jnp.int32)
    csum = jnp.cumsum(onehot, axis=0)
    counts = csum[-1]
    rank = jnp.take_along_axis(csum, flat_e[:, None], axis=1)[:, 0] - 1
    padded = (counts + MOE_TM - 1) // MOE_TM * MOE_TM
    pad_end = jnp.cumsum(padded)
    pad_start = pad_end - padded
    slot = (pad_start[flat_e] + rank).astype(jnp.int32)
    n_blocks = -(-(n_assign + N_EXPERTS * (MOE_TM - 1)) // MOE_TM)
    n_slots = n_blocks * MOE_TM
    slot_tok = jnp.full((n_slots,), n_tok, jnp.int32).at[slot].set(jnp.arange(n_assign, dtype=jnp.int32) // TOP_K)
    slot_p = jnp.zeros((n_slots,), F32).at[slot].set(top_p.reshape(-1))
    n_valid = (pad_end[-1] // MOE_TM).astype(jnp.int32)
    blk = jnp.arange(n_blocks, dtype=jnp.int32)
    block_e = jnp.searchsorted(pad_end, jnp.minimum(blk, n_valid - 1) * MOE_TM, side='right').astype(jnp.int32)
    block_e = jnp.minimum(block_e, N_EXPERTS - 1)
    return slot, slot_tok, slot_p, block_e, n_valid.reshape(1)


PREP_ROWS = 4096


def _prep_body(x_ref, w_ref, o_ref, *, t_first, t_rest):
    i = pl.program_id(0)
    j = pl.program_id(1)
    t_seq = jnp.where(i == 0, t_first, t_rest)
    x = x_ref[...]
    n = x.shape[0]
    row = lax.broadcasted_iota(jnp.int32, x.shape, 0)
    tpos = row & (t_seq - 1)
    xm = jnp.where(tpos == 0, 0.0, pltpu.roll(x, 1, axis=0))
    xp = jnp.where(tpos == t_seq - 1, 0.0, pltpu.roll(x, n - 1, axis=0))
    w = w_ref[...]
    y = xm * w[0:1] + x * w[1:2] + xp * w[2:3]
    y = y * jax.nn.sigmoid(y)
    rs = lax.rsqrt(jnp.sum(y * y, axis=-1, keepdims=True) + EPS)
    nq = QK_W // DK
    fac = jnp.where(j < nq, rs * (DK ** -0.5), jnp.where(j < 2 * nq, rs, 1.0))
    o_ref[...] = y * fac


def _qkv_prep(proj, conv_w, *, t_first, t_rest):
    n = proj.shape[0]
    return pl.pallas_call(
        functools.partial(_prep_body, t_first=t_first, t_rest=t_rest),
        name="qkv_prep",
        grid=(n // PREP_ROWS, QKV_W // DK),
        in_specs=[pl.BlockSpec((PREP_ROWS, DK), lambda i, j: (i, j)),
                  pl.BlockSpec((3, DK), lambda i, j: (0, j))],
        out_specs=pl.BlockSpec((PREP_ROWS, DK), lambda i, j: (i, j)),
        out_shape=jax.ShapeDtypeStruct((n, QKV_W), F32),
        compiler_params=pltpu.CompilerParams(dimension_semantics=("parallel", "parallel"),
                                             vmem_limit_bytes=VMEM_LIMIT_BYTES),
    )(proj, conv_w)


SOLVE_WAYS = 4


def _mmul(a, b):
    return jnp.dot(a, b, preferred_element_type=F32)


def _delta_body(q_ref, k_ref, v_ref, z_ref, cols_ref, rows_ref, nw_ref, s0f_ref, s0b_ref, *rest,
                n_chunks, zero_init, has_prev):
    o_ref, sf_ref, sb_ref, acc_ref, s_ref, wu_ref, a_ref = rest[1:] if has_prev else rest
    lane = lax.broadcasted_iota(jnp.int32, (CHUNK, 2 * CHUNK), 1)
    row = lax.broadcasted_iota(jnp.int32, (CHUNK, 2 * CHUNK), 0)
    pos = lane & (CHUNK - 1)
    left = lane < CHUNK
    left1 = lax.broadcasted_iota(jnp.int32, (1, 2 * CHUNK), 1) < CHUNK
    eye2 = (pos == row).astype(F32)
    incl = (pos <= row, pos >= row)
    strict = (pos < row, pos > row)
    same8 = (row >> 3) == (pos >> 3)
    same16 = (row >> 4) == (pos >> 4)
    same32 = (row >> 5) == (pos >> 5)
    nt = (((1,), (1,)), ((), ()))
    tn = (((0,), (0,)), ((), ()))

    def hi_lo(m):
        hi = m.astype(BF16)
        return hi, (m - hi.astype(F32)).astype(BF16)

    def split(m):
        return jnp.where(left, m, 0.0), jnp.where(left, 0.0, m)

    def bd3(m):
        hi, lo = hi_lo(jnp.concatenate(split(m), axis=0))
        return jnp.concatenate([hi, hi, lo], axis=0)

    def mm3(a, b3):
        hi, lo = hi_lo(a)
        return _mmul(jnp.concatenate([hi, lo, hi], axis=1), b3)

    def inv_unit_tri(lms):
        d8 = [jnp.where(same8, lm, 0.0) for lm in lms]
        xs = [eye2 - d for d in d8]
        d2 = [mm3(d, bd3(d)) for d in d8]
        xs = [x + mm3(x, bd3(d)) for x, d in zip(xs, d2)]
        d4 = [mm3(d, bd3(d)) for d in d2]
        xs = [x + mm3(x, bd3(d)) for x, d in zip(xs, d4)]
        for keep in (same16 & ~same8, same32 & ~same16, ~same32):
            ys = [mm3(jnp.where(keep, lm, 0.0), bd3(x)) for lm, x in zip(lms, xs)]
            xs = [x - mm3(x, bd3(y)) for x, y in zip(xs, ys)]
        return xs

    def chunk_terms(c, d):
        r0 = pl.multiple_of(c * CHUNK, CHUNK)
        cols = cols_ref[pl.ds(r0, CHUNK), :]
        rows = rows_ref[c]
        gcol = [cols[:, 2 * d + h:2 * d + h + 1] for h in (0, 1)]
        bcol = [cols[:, 4 + 2 * d + h:4 + 2 * d + h + 1] for h in (0, 1)]
        gl = [rows[4 + 2 * d + h:5 + 2 * d + h] for h in (0, 1)]
        return r0, rows, gcol, bcol, gl

    acc_ref[...] = jnp.zeros_like(acc_ref)
    if zero_init:
        s_ref[...] = jnp.zeros_like(s_ref)
    else:
        s_ref[0] = s0f_ref[0]
        s_ref[1] = s0f_ref[1]
        s_ref[2] = s0b_ref[0]
        s_ref[3] = s0b_ref[1]

    def solve(i, carry):
        chunks = [i + part * (n_chunks // SOLVE_WAYS) for part in range(SOLVE_WAYS)]
        kcs, vcs, kkms, qkms = [], [], [], []
        for c in chunks:
            r0 = pl.multiple_of(c * CHUNK, CHUNK)
            kc = k_ref[pl.ds(r0, CHUNK), :]
            kb = kc.astype(BF16)
            kk2 = jnp.concatenate([kb, kb], axis=0)
            kcs.append(kc)
            vcs.append(v_ref[pl.ds(r0, CHUNK), :])
            kkms.append(lax.dot_general(kb, kk2, nt, preferred_element_type=F32))
            qkms.append(lax.dot_general(q_ref[pl.ds(r0, CHUNK), :].astype(BF16), kk2, nt,
                                        preferred_element_type=F32))
        chains = [(j, d) for j in range(SOLVE_WAYS) for d in (0, 1)]
        lms, rhs4s = [], []
        for j, d in chains:
            c = chunks[j]
            _, rows, gcol, bcol, _ = chunk_terms(c, d)
            gcol2 = jnp.where(left, gcol[0], gcol[1])
            bcol2 = jnp.where(left, bcol[0], bcol[1])
            grow2 = jnp.where(left1, rows[2 * d:2 * d + 1], rows[2 * d + 1:2 * d + 2])
            dm = jnp.exp(jnp.where(incl[d], gcol2 - grow2, -jnp.inf))
            a_ref[d, c] = (qkms[j] * dm).astype(BF16)
            lms.append(jnp.where(strict[d], kkms[j] * dm * bcol2, 0.0))
            rhs = [jnp.concatenate([vcs[j][:, h * DV:(h + 1) * DV] * bcol[h],
                                    kcs[j] * (bcol[h] * jnp.exp(gcol[h]))], axis=1) for h in (0, 1)]
            rhs2 = jnp.concatenate(rhs, axis=0).astype(BF16)
            rhs4s.append(jnp.concatenate([rhs2, rhs2], axis=0))
        t2s = inv_unit_tri(lms)
        sols = [[_mmul(jnp.concatenate(hi_lo(th), axis=1), rhs4) for th in split(t2)]
                for t2, rhs4 in zip(t2s, rhs4s)]
        for (j, d), sol in zip(chains, sols):
            wu_ref[d, chunks[j]] = jnp.concatenate(sol, axis=1).astype(BF16)
        return carry

    lax.fori_loop(0, n_chunks // SOLVE_WAYS, solve, 0)

    def step(i, carry):
        terms = [chunk_terms(i if d == 0 else n_chunks - 1 - i, d) for d in (0, 1)]
        aws, ngs, qcs = [], [], []
        for d in (0, 1):
            c = i if d == 0 else n_chunks - 1 - i
            r0, _, gcol, _, gl = terms[d]
            qcs.append(q_ref[pl.ds(r0, CHUNK), :])
            kb = k_ref[pl.ds(r0, CHUNK), :].astype(BF16)
            wu = wu_ref[d, c]
            wu2 = jnp.concatenate([wu[:, :2 * DV], wu[:, 2 * DV:]], axis=0)
            al, ar = split(a_ref[d, c].astype(F32))
            aws.append([_mmul(al.astype(BF16), wu2), _mmul(ar.astype(BF16), wu2)])
            ek = [jnp.exp(gl[h][:, 0:1] - gcol[h]) for h in (0, 1)]
            wuf = wu.astype(F32)
            zz = jnp.concatenate([wuf[:, :2 * DV] * ek[0], wuf[:, 2 * DV:] * ek[1]], axis=1).astype(BF16)
            ngs.append(lax.dot_general(kb, zz, tn, preferred_element_type=F32))
        streams = [(d, h) for d in (0, 1) for h in (0, 1)]
        outs = []
        for d, h in streams:
            p = qcs[d] * jnp.exp(terms[d][2][h]) - aws[d][h][:, DV:]
            pg = jnp.concatenate([p, ngs[d][:, (2 * h + 1) * DV:(2 * h + 2) * DV]], axis=0).astype(BF16)
            outs.append(_mmul(pg, s_ref[2 * d + h].astype(BF16)))
        for (d, h), out in zip(streams, outs):
            s = 2 * d + h
            r0, gl = terms[d][0], terms[d][4]
            acc_ref[pl.ds(r0, CHUNK), h * DV:(h + 1) * DV] += out[:CHUNK] + aws[d][h][:, :DV]
            s_ref[s] = s_ref[s] * jnp.exp(gl[h]) - out[CHUNK:] + ngs[d][:, 2 * h * DV:(2 * h + 1) * DV]
        return carry

    lax.fori_loop(0, n_chunks, step, 0)

    nw = nw_ref[...]

    def finish(j, carry):
        r0 = pl.multiple_of(j * CHUNK, CHUNK)
        o = acc_ref[pl.ds(r0, CHUNK), :]
        z = z_ref[pl.ds(r0, CHUNK), :]
        for h in (0, 1):
            oh = o[:, h * DV:(h + 1) * DV]
            zh = z[:, h * DV:(h + 1) * DV]
            y = oh * lax.rsqrt(jnp.mean(oh * oh, axis=-1, keepdims=True) + EPS) * nw * (zh * jax.nn.sigmoid(zh))
            o_ref[pl.ds(r0, CHUNK), h * DV:(h + 1) * DV] = y.astype(o_ref.dtype)
        return carry

    lax.fori_loop(0, n_chunks, finish, 0)
    sf_ref[0] = s_ref[0]
    sf_ref[1] = s_ref[1]
    sb_ref[0] = s_ref[2]
    sb_ref[1] = s_ref[3]


def _deltanet(qkvn, proj, cols, rows, norm_w, s0f, s0b, o_prev, *, n_seq, t_seq, row_blk0, zero_init):
    n_chunks = t_seq // CHUNK
    nq = QK_W // DK
    z_blk0 = QKV_W // (2 * DV)
    rep = NV_HEADS // NK_HEADS
    in_specs = [
        pl.BlockSpec((t_seq, DK), lambda b, kh: (row_blk0 + b, kh)),
        pl.BlockSpec((t_seq, DK), lambda b, kh: (row_blk0 + b, nq + kh)),
        pl.BlockSpec((t_seq, rep * DV), lambda b, kh: (row_blk0 + b, nq + kh)),
        pl.BlockSpec((t_seq, rep * DV), lambda b, kh: (row_blk0 + b, z_blk0 + kh)),
        pl.BlockSpec((None, t_seq, 8), lambda b, kh: (kh, row_blk0 + b, 0)),
        pl.BlockSpec((None, n_chunks, 8, 2 * CHUNK), lambda b, kh: (kh, row_blk0 + b, 0, 0)),
        pl.BlockSpec((1, DV), lambda b, kh: (0, 0)),
        pl.BlockSpec((None, rep, DK, DV), lambda b, kh: (0 if zero_init else b, kh, 0, 0)),
        pl.BlockSpec((None, rep, DK, DV), lambda b, kh: (0 if zero_init else b, kh, 0, 0)),
    ]
    out_specs = [
        pl.BlockSpec((t_seq, rep * DV), lambda b, kh: (row_blk0 + b, kh)),
        pl.BlockSpec((None, rep, DK, DV), lambda b, kh: (b, kh, 0, 0)),
        pl.BlockSpec((None, rep, DK, DV), lambda b, kh: (b, kh, 0, 0)),
    ]
    out_shape = [
        jax.ShapeDtypeStruct((qkvn.shape[0], V_W), BF16),
        jax.ShapeDtypeStruct((n_seq, NV_HEADS, DK, DV), F32),
        jax.ShapeDtypeStruct((n_seq, NV_HEADS, DK, DV), F32),
    ]
    args = [qkvn, qkvn, qkvn, proj, cols, rows, norm_w.reshape(1, DV), s0f, s0b]
    aliases = {}
    if o_prev is not None:
        in_specs.append(pl.BlockSpec(memory_space=pl.ANY))
        args.append(o_prev)
        aliases = {len(args) - 1: 0}
    return pl.pallas_call(
        functools.partial(_delta_body, n_chunks=n_chunks, zero_init=zero_init, has_prev=o_prev is not None),
        grid=(n_seq, NK_HEADS),
        input_output_aliases=aliases,
        name="deltanet",
        in_specs=in_specs,
        out_specs=out_specs,
        out_shape=out_shape,
        scratch_shapes=[pltpu.VMEM((t_seq, rep * DV), F32), pltpu.VMEM((2 * rep, DK, DV), F32),
                        pltpu.VMEM((2, n_chunks, CHUNK, 2 * rep * DV), BF16),
                        pltpu.VMEM((2, n_chunks, CHUNK, 2 * CHUNK), BF16)],
        compiler_params=pltpu.CompilerParams(dimension_semantics=("parallel", "parallel"),
                                             vmem_limit_bytes=VMEM_LIMIT_BYTES),
    )(*args)


def _decay_tables(ab, a_log, dt_bias):
    n = ab.shape[0]
    nh2 = 2 * NV_HEADS
    rep = NV_HEADS // NK_HEADS
    beta = jax.nn.sigmoid(ab[:, :nh2])
    g = -jnp.exp(a_log.reshape(nh2)) * jax.nn.softplus(ab[:, nh2:] + dt_bias.reshape(nh2))
    gch = g.reshape(n // CHUNK, CHUNK, 2, NV_HEADS)
    gcf = jnp.cumsum(gch[:, :, 0], axis=1)
    gcb = jnp.flip(jnp.cumsum(jnp.flip(gch[:, :, 1], 1), axis=1), 1)
    gc = jnp.stack([gcf, gcb], axis=2)
    gl = jnp.stack([gcf[:, -1], gcb[:, 0]], axis=1)

    def per_kh(a):
        a = a.reshape(a.shape[:-2] + (2, NK_HEADS, rep))
        return jnp.moveaxis(a, -3, -2).reshape(a.shape[:-3] + (NK_HEADS, 2 * rep))
    gc_k = per_kh(gc)
    cols = jnp.concatenate([gc_k.reshape(n, NK_HEADS, 2 * rep), per_kh(beta.reshape(n, 2, NV_HEADS))], axis=-1)
    cols = jnp.transpose(cols, (1, 0, 2))
    gr = jnp.transpose(gc_k, (2, 0, 3, 1))
    gr = jnp.concatenate([gr, gr], axis=-1)
    glr = jnp.broadcast_to(jnp.transpose(per_kh(gl), (1, 0, 2))[..., None], gr.shape)
    rows = jnp.concatenate([gr, glr], axis=2)
    return cols, rows


TOK_TILE = 512
MIX_TILE = 256
COMBINE_TILE = 128


def _group_of_tile(i, tile, n_prompt, t_sample):
    first = n_prompt // tile
    return jnp.where(i < first, 0, 1 + (i - first) // (t_sample // tile))


def _token_specs(tile, n_prompt, t_sample, d):
    first = n_prompt // tile
    per_seq = t_sample // tile
    return [
        pl.BlockSpec((tile, d), lambda i, *_: (jnp.minimum(i, first - 1), 0)),
        pl.BlockSpec((tile, d), lambda i, *_: (jnp.maximum(i - first, 0), 0)),
        pl.BlockSpec((tile, d), lambda i, *_: (jnp.maximum(i - first, 0) % per_seq, 0)),
        pl.BlockSpec((None, 1, 6 * d), lambda i, *_: (_group_of_tile(i, tile, n_prompt, t_sample), 0, 0)),
    ]


def _token_rows(i, first, xp_ref, xs_ref, pos_ref):
    return jnp.where(i < first, xp_ref[...], xs_ref[...] + pos_ref[...])


def _modulate_body(xp_ref, xs_ref, pos_ref, mod_ref, h_ref, *, first):
    d = h_ref.shape[-1]
    x = _token_rows(pl.program_id(0), first, xp_ref, xs_ref, pos_ref)
    h_ref[...] = (x * (1.0 + mod_ref[:, d:2 * d]) + mod_ref[:, 0:d]).astype(h_ref.dtype)


def _modulate(xp, xs, pos, mod, *, t_sample):
    n_p, d = xp.shape
    n = n_p + xs.shape[0]
    return pl.pallas_call(
        functools.partial(_modulate_body, first=n_p // TOK_TILE),
        grid=(n // TOK_TILE,),
        in_specs=_token_specs(TOK_TILE, n_p, t_sample, d),
        out_specs=pl.BlockSpec((TOK_TILE, d), lambda i: (i, 0)),
        out_shape=jax.ShapeDtypeStruct((n, d), BF16),
        compiler_params=pltpu.CompilerParams(dimension_semantics=("parallel",), vmem_limit_bytes=VMEM_LIMIT_BYTES),
        name="modulate",
    )(xp, xs, pos, mod)


def _gconv_body(ub_ref, uc_ref, ux_ref, w_ref, o_ref, *, t_first, t_rest):
    t_seq = jnp.where(pl.program_id(0) == 0, t_first, t_rest)
    x = uc_ref[...] * ux_ref[...]
    n = x.shape[0]
    tpos = lax.broadcasted_iota(jnp.int32, x.shape, 0) & (t_seq - 1)
    xm = jnp.where(tpos == 0, 0.0, pltpu.roll(x, 1, axis=0))
    xp = jnp.where(tpos == t_seq - 1, 0.0, pltpu.roll(x, n - 1, axis=0))
    w = w_ref[...]
    o_ref[...] = (ub_ref[...] * (xm * w[0:1] + x * w[1:2] + xp * w[2:3])).astype(o_ref.dtype)


def _gated_conv(proj, conv_w, col0, *, t_first, t_rest):
    n = proj.shape[0]
    nblk = CONV_CH // DK
    b0 = col0 // DK
    return pl.pallas_call(
        functools.partial(_gconv_body, t_first=t_first, t_rest=t_rest),
        grid=(n // PREP_ROWS, nblk),
        in_specs=[pl.BlockSpec((PREP_ROWS, DK), lambda i, j: (i, b0 + j)),
                  pl.BlockSpec((PREP_ROWS, DK), lambda i, j: (i, b0 + nblk + j)),
                  pl.BlockSpec((PREP_ROWS, DK), lambda i, j: (i, b0 + 2 * nblk + j)),
                  pl.BlockSpec((3, DK), lambda i, j: (0, j))],
        out_specs=pl.BlockSpec((PREP_ROWS, DK), lambda i, j: (i, j)),
        out_shape=jax.ShapeDtypeStruct((n, CONV_CH), BF16),
        compiler_params=pltpu.CompilerParams(dimension_semantics=("parallel", "parallel"),
                                             vmem_limit_bytes=VMEM_LIMIT_BYTES),
        name="gated_conv",
    )(proj, proj, proj, conv_w)


BR_TM = 512
BR_TN = 512


def _branch_body(o_ref, g_ref, wa_ref, wb_ref, ra_ref, rb_ref, m_ref, wa_bf, wb_bf):
    @pl.when(pl.program_id(1) == 0)
    def _():
        wa_bf[...] = wa_ref[...].astype(BF16)
        wb_bf[...] = wb_ref[...].astype(BF16)
    ya = jnp.dot(o_ref[...], wa_bf[...], preferred_element_type=F32)
    yb = jnp.dot(g_ref[...], wb_bf[...], preferred_element_type=F32)
    m_ref[...] = (jax.nn.sigmoid(ra_ref[...]) * ya + jax.nn.sigmoid(rb_ref[...]) * yb).astype(m_ref.dtype)


def _branch_merge(o, g, w_a, w_b, proj, ra_col, rb_col):
    n = o.shape[0]
    d = w_a.shape[1]
    el = pl.Element
    return pl.pallas_call(
        _branch_body,
        grid=(d // BR_TN, n // BR_TM),
        in_specs=[pl.BlockSpec((BR_TM, o.shape[1]), lambda j, i: (i, 0)),
                  pl.BlockSpec((BR_TM, g.shape[1]), lambda j, i: (i, 0)),
                  pl.BlockSpec((w_a.shape[0], BR_TN), lambda j, i: (0, j)),
                  pl.BlockSpec((w_b.shape[0], BR_TN), lambda j, i: (0, j)),
                  pl.BlockSpec((el(BR_TM), el(BR_TN)),
                               lambda j, i: (i * BR_TM, pl.multiple_of(ra_col + j * BR_TN, DK))),
                  pl.BlockSpec((el(BR_TM), el(BR_TN)),
                               lambda j, i: (i * BR_TM, pl.multiple_of(rb_col + j * BR_TN, DK)))],
        out_specs=pl.BlockSpec((BR_TM, BR_TN), lambda j, i: (i, j)),
        out_shape=jax.ShapeDtypeStruct((n, d), BF16),
        scratch_shapes=[pltpu.VMEM((w_a.shape[0], BR_TN), BF16), pltpu.VMEM((w_b.shape[0], BR_TN), BF16)],
        compiler_params=pltpu.CompilerParams(dimension_semantics=("arbitrary", "arbitrary"),
                                             vmem_limit_bytes=VMEM_LIMIT_BYTES),
        name="branch_merge",
    )(o, g, w_a, w_b, proj, proj)


def _ln_rows(v, g, b):
    mu = jnp.mean(v, axis=-1, keepdims=True)
    c = v - mu
    return c * lax.rsqrt(jnp.mean(c * c, axis=-1, keepdims=True) + EPS) * g + b


def _mix_body(m_ref, wo_ref, xp_ref, xs_ref, pos_ref, mod_ref, g_ref, b_ref, rw_ref, rb_ref,
              x1_ref, h2_ref, lg_ref, *, first):
    d = x1_ref.shape[-1]
    x = _token_rows(pl.program_id(0), first, xp_ref, xs_ref, pos_ref)
    mixed = jnp.dot(m_ref[...], wo_ref[...], preferred_element_type=F32)
    x1 = _ln_rows(DEEPNORM_ALPHA * x + mod_ref[:, 2 * d:3 * d] * mixed, g_ref[...], b_ref[...])
    h2 = x1 * (1.0 + mod_ref[:, 4 * d:5 * d]) + mod_ref[:, 3 * d:4 * d]
    x1_ref[...] = x1
    h2_ref[...] = h2.astype(h2_ref.dtype)
    lg_ref[...] = jnp.dot(h2, rw_ref[...], precision=lax.Precision.HIGHEST, preferred_element_type=F32) + rb_ref[...]


def _mix_norm_route(m, w_o_bf, xp, xs, pos, mod, ln_g, ln_b, router_w, router_b, *, t_sample):
    n, d = m.shape
    n_p = xp.shape[0]
    ne = router_w.shape[1]
    const = lambda i: (0, 0)
    return pl.pallas_call(
        functools.partial(_mix_body, first=n_p // MIX_TILE),
        grid=(n // MIX_TILE,),
        in_specs=[pl.BlockSpec((MIX_TILE, d), lambda i: (i, 0)), pl.BlockSpec((d, d), const)]
        + _token_specs(MIX_TILE, n_p, t_sample, d)
        + [pl.BlockSpec((1, d), const), pl.BlockSpec((1, d), const),
           pl.BlockSpec((d, ne), const), pl.BlockSpec((1, ne), const)],
        out_specs=[pl.BlockSpec((MIX_TILE, d), lambda i: (i, 0)), pl.BlockSpec((MIX_TILE, d), lambda i: (i, 0)),
                   pl.BlockSpec((MIX_TILE, ne), lambda i: (i, 0))],
        out_shape=[jax.ShapeDtypeStruct((n, d), F32), jax.ShapeDtypeStruct((n, d), BF16),
                   jax.ShapeDtypeStruct((n, ne), F32)],
        compiler_params=pltpu.CompilerParams(dimension_semantics=("parallel",), vmem_limit_bytes=VMEM_LIMIT_BYTES),
        name="mix_norm_route",
    )(m, w_o_bf, xp, xs, pos, mod, ln_g.reshape(1, d), ln_b.reshape(1, d), router_w, router_b.reshape(1, ne))


def _combine_body(slot_ref, y_hbm, x1_ref, mod_ref, g_ref, b_ref, o_ref, buf, sem, *, n_tiles):
    tt = COMBINE_TILE
    d = o_ref.shape[-1]
    i = pl.program_id(0)

    def row_copy(src_row, dst_row, which):
        return pltpu.make_async_copy(y_hbm.at[pl.ds(src_row, 1), :], buf.at[which, pl.ds(dst_row, 1), :],
                                     sem.at[which])

    def issue(tile, which):
        base = tile * (tt * TOP_K)

        def body(t, carry):
            for k in range(TOP_K):
                row_copy(slot_ref[base + t * TOP_K + k], k * tt + t, which).start()
            return carry
        lax.fori_loop(0, tt, body, 0)

    @pl.when(i == 0)
    def _():
        issue(0, 0)

    @pl.when(i + 1 < n_tiles)
    def _():
        issue(i + 1, (i + 1) % 2)

    cur = i % 2
    pltpu.make_async_copy(y_hbm.at[pl.ds(0, TOP_K * tt), :], buf.at[cur], sem.at[cur]).wait()
    ff = buf[cur, 0:tt, :]
    for k in range(1, TOP_K):
        ff = ff + buf[cur, k * tt:(k + 1) * tt, :]
    o_ref[...] = _ln_rows(DEEPNORM_ALPHA * x1_ref[...] + mod_ref[:, 5 * d:6 * d] * ff, g_ref[...], b_ref[...])


def _combine_norm(slot, y, x1, mod, ln_g, ln_b, *, n_prompt, t_sample):
    n, d = x1.shape
    tt = COMBINE_TILE
    n_tiles = n // tt
    grid_spec = pltpu.PrefetchScalarGridSpec(
        num_scalar_prefetch=1,
        grid=(n_tiles,),
        in_specs=[pl.BlockSpec(memory_space=pl.ANY),
                  pl.BlockSpec((tt, d), lambda i, s: (i, 0)),
                  pl.BlockSpec((None, 1, 6 * d), lambda i, s: (_group_of_tile(i, tt, n_prompt, t_sample), 0, 0)),
                  pl.BlockSpec((1, d), lambda i, s: (0, 0)),
                  pl.BlockSpec((1, d), lambda i, s: (0, 0))],
        out_specs=pl.BlockSpec((tt, d), lambda i, s: (i, 0)),
        scratch_shapes=[pltpu.VMEM((2, TOP_K * tt, d), F32), pltpu.SemaphoreType.DMA((2,))],
    )
    return pl.pallas_call(
        functools.partial(_combine_body, n_tiles=n_tiles),
        grid_spec=grid_spec,
        out_shape=jax.ShapeDtypeStruct((n, d), F32),
        compiler_params=pltpu.CompilerParams(dimension_semantics=("arbitrary",), vmem_limit_bytes=VMEM_LIMIT_BYTES),
        name="combine_norm",
    )(slot, y, x1, mod, ln_g.reshape(1, d), ln_b.reshape(1, d))


def _grid_pos_embed(rows, dim):
    quarter = dim // 4
    freqs = 1.0 / (POS_BASE ** (jnp.arange(quarter, dtype=F32) / quarter))

    def axis_embed(n):
        ang = jnp.arange(n, dtype=F32)[:, None] * freqs[None, :]
        return jnp.concatenate([jnp.sin(ang), jnp.cos(ang)], axis=-1)
    er = axis_embed(rows)
    ec = axis_embed(GRID_W)
    pos = jnp.concatenate([jnp.broadcast_to(er[:, None, :], (rows, GRID_W, dim // 2)),
                           jnp.broadcast_to(ec[None, :, :], (rows, GRID_W, dim // 2))], axis=-1)
    return pos.reshape(rows * GRID_W, dim)


def kernel(x_prompt, x_sample, c, state_fwd, state_bwd, c_ctx, w_mod, b_mod, w_in, conv_qkv, a_log, dt_bias,
           norm_o, w_a_out, conv_b, w_b_out, w_o, ln1_g, ln1_b, router_w, router_b, w_gu, b_gu, w_down, b_down,
           ln2_g, ln2_b):
    nb_p, t_p, d = x_prompt.shape
    nb_s, t_s, _ = x_sample.shape
    n_p = nb_p * t_p
    n_s = nb_s * t_s
    assert n_p == PREP_ROWS and t_s == PREP_ROWS
    xp = x_prompt.reshape(n_p, d)
    xs = x_sample.reshape(n_s, d)
    pos = _grid_pos_embed(t_s // GRID_W, d)
    offs = tuple(int(o) for o in np.cumsum(IN_SIZES)[:-1])

    l = 0
    cvec = jnp.concatenate([c_ctx[None, :], c], axis=0)
    cpad = jnp.zeros((8, d), F32).at[:1 + nb_s].set(jax.nn.silu(cvec))
    mod = (_mm(cpad, w_mod[l], tm=8, tn=512)[:1 + nb_s] + b_mod[l])[:, None, :]

    h = _modulate(xp, xs, pos, mod, t_sample=t_s)
    proj = _mm(h, w_in[l], tm=2048, tn=384)

    qkvn = _qkv_prep(proj, conv_qkv[l], t_first=t_p, t_rest=t_s)
    cols, rws = _decay_tables(proj[:, offs[1]:offs[3]], a_log[l], dt_bias[l])
    zero_state = jnp.zeros((1, NV_HEADS, DK, DV), F32)
    o_all = jnp.zeros((n_p + n_s, V_W), BF16)
    o_all, sf, sb = _deltanet(qkvn, proj, cols, rws, norm_o[l], zero_state, zero_state, o_all,
                              n_seq=nb_p, t_seq=t_p, row_blk0=0, zero_init=True)
    o_all, _, _ = _deltanet(qkvn, proj, cols, rws, norm_o[l], state_fwd[:, l], state_bwd[:, l], o_all,
                            n_seq=nb_s, t_seq=t_s, row_blk0=n_p // t_s, zero_init=False)

    gconv = _gated_conv(proj, conv_b[l], offs[3], t_first=t_p, t_rest=t_s)
    m = _branch_merge(o_all, gconv, w_a_out[l], w_b_out[l], proj, offs[6], offs[7])
    x1, h2, logits = _mix_norm_route(m, w_o[l].astype(BF16), xp, xs, pos, mod, ln1_g[l], ln1_b[l],
                                     router_w[l], router_b[l], t_sample=t_s)
    slot, slot_tok, slot_p, block_e, n_valid = _route(logits)
    h_pad = jnp.concatenate([h2, jnp.zeros((1, d), BF16)], axis=0)
    y = _moe_blocks(h_pad[slot_tok], slot_p, block_e, n_valid, w_gu[l], b_gu[l], w_down[l], b_down[l])
    x2 = _combine_norm(slot, y, x1, mod, ln2_g[l], ln2_b[l], n_prompt=n_p, t_sample=t_s)
    y_prompt = x2[:n_p].reshape(nb_p, t_p, d)
    y_sample = x2[n_p:].reshape(nb_s, t_s, d)
    return (y_prompt, y_sample, sf[:, None], sb[:, None])
```

```python
import functools

import jax
import jax.numpy as jnp
import numpy as np
from jax import lax
from jax.experimental import pallas as pl
from jax.experimental.pallas import tpu as pltpu

F32 = jnp.float32
BF16 = jnp.bfloat16

D_MODEL = 2048
GRID_W = 64
NK_HEADS = 16
NV_HEADS = 32
DK = 128
DV = 128
QK_W = NK_HEADS * DK
V_W = NV_HEADS * DV
QKV_W = 2 * QK_W + V_W
CHUNK = 64
CONV_CH = D_MODEL
IN_SIZES = (QKV_W, V_W, 2 * NV_HEADS, 2 * NV_HEADS, CONV_CH, CONV_CH, CONV_CH, D_MODEL, D_MODEL)
N_EXPERTS = 32
TOP_K = 4
D_FF = D_MODEL
SWIGLU_LIMIT = 7.0
SWIGLU_ALPHA = 1.702
DEPTH = 1
DEEPNORM_ALPHA = (2 * DEPTH) ** 0.25
EPS = 1e-6
POS_BASE = 10000.0

VMEM_LIMIT_BYTES = 56 * 1024 * 1024


def _mm_body(x_ref, w_ref, o_ref):
    o_ref[...] = jnp.dot(x_ref[...].astype(BF16), w_ref[...].astype(BF16), preferred_element_type=F32)


def _mm(x, w, *, tm, tn):
    m, k = x.shape
    n = w.shape[1]
    assert m % tm == 0 and n % tn == 0
    return pl.pallas_call(
        _mm_body,
        name="matmul",
        grid=(m // tm, n // tn),
        in_specs=[pl.BlockSpec((tm, k), lambda i, j: (i, 0)),
                  pl.BlockSpec((k, tn), lambda i, j: (0, j))],
        out_specs=pl.BlockSpec((tm, tn), lambda i, j: (i, j)),
        out_shape=jax.ShapeDtypeStruct((m, n), F32),
        compiler_params=pltpu.CompilerParams(dimension_semantics=("parallel", "parallel"),
                                             vmem_limit_bytes=VMEM_LIMIT_BYTES),
    )(x, w)


MOE_TM = 512
MOE_TF = 512


def _moe_body(be_ref, nv_ref, x_ref, wg_ref, wu_ref, bg_ref, bu_ref, wd_ref, bd_ref, p_ref, o_ref, acc_ref):
    i = pl.program_id(0)
    f = pl.program_id(1)
    valid = i < nv_ref[0]

    @pl.when(f == 0)
    def _():
        acc_ref[...] = jnp.zeros_like(acc_ref)

    @pl.when(valid)
    def _():
        x = x_ref[...].astype(BF16)
        g = jnp.dot(x, wg_ref[...].astype(BF16), preferred_element_type=F32) + bg_ref[...]
        u = jnp.dot(x, wu_ref[...].astype(BF16), preferred_element_type=F32) + bu_ref[...]
        gate = jnp.minimum(g, SWIGLU_LIMIT)
        up = jnp.clip(u, -SWIGLU_LIMIT, SWIGLU_LIMIT)
        act = (up + 1.0) * gate * jax.nn.sigmoid(SWIGLU_ALPHA * gate)
        acc_ref[...] += jnp.dot(act.astype(BF16), wd_ref[...].astype(BF16), preferred_element_type=F32)

    @pl.when(f == pl.num_programs(1) - 1)
    def _():
        o_ref[...] = jnp.where(valid, (acc_ref[...] + bd_ref[...]) * p_ref[...], 0.0)


def _moe_blocks(xs, slot_p, block_e, n_valid, w_gu, b_gu, w_down, b_down):
    n_slots, d = xs.shape
    n_blocks = n_slots // MOE_TM
    nf = D_FF // MOE_TF

    def f_eff(i, f, nv):
        return jnp.where(i < nv[0], f, nf - 1)

    grid_spec = pltpu.PrefetchScalarGridSpec(
        num_scalar_prefetch=2,
        grid=(n_blocks, nf),
        in_specs=[
            pl.BlockSpec((MOE_TM, d), lambda i, f, be, nv: (i, 0)),
            pl.BlockSpec((None, d, MOE_TF), lambda i, f, be, nv: (be[i], 0, f_eff(i, f, nv))),
            pl.BlockSpec((None, d, MOE_TF), lambda i, f, be, nv: (be[i], 0, nf + f_eff(i, f, nv))),
            pl.BlockSpec((None, 1, MOE_TF), lambda i, f, be, nv: (be[i], 0, f_eff(i, f, nv))),
            pl.BlockSpec((None, 1, MOE_TF), lambda i, f, be, nv: (be[i], 0, nf + f_eff(i, f, nv))),
            pl.BlockSpec((None, MOE_TF, d), lambda i, f, be, nv: (be[i], f_eff(i, f, nv), 0)),
            pl.BlockSpec((None, 1, d), lambda i, f, be, nv: (be[i], 0, 0)),
            pl.BlockSpec((MOE_TM, 1), lambda i, f, be, nv: (i, 0)),
        ],
        out_specs=pl.BlockSpec((MOE_TM, d), lambda i, f, be, nv: (i, 0)),
        scratch_shapes=[pltpu.VMEM((MOE_TM, d), F32)],
    )
    return pl.pallas_call(
        _moe_body,
        name="moe_experts",
        grid_spec=grid_spec,
        out_shape=jax.ShapeDtypeStruct((n_slots, d), F32),
        compiler_params=pltpu.CompilerParams(dimension_semantics=("arbitrary", "arbitrary"),
                                             vmem_limit_bytes=VMEM_LIMIT_BYTES),
    )(block_e, n_valid, xs, w_gu, w_gu, b_gu[:, None, :], b_gu[:, None, :], w_down, b_down[:, None, :],
      slot_p[:, None])


def _route(logits):
    n_tok = logits.shape[0]
    n_assign = n_tok * TOP_K
    top_logit, top_e = lax.top_k(logits, TOP_K)
    top_p = jax.nn.softmax(top_logit, axis=-1)
    flat_e = top_e.reshape(-1)
    onehot = (flat_e[:, None] == jnp.arange(N_EXPERTS)[None, :]).astype(jnp.int32)
    csum = jnp.cumsum(onehot, axis=0)
    counts = csum[-1]
    rank = jnp.take_along_axis(csum, flat_e[:, None], axis=1)[:, 0] - 1
    padded = (counts + MOE_TM - 1) // MOE_TM * MOE_TM
    pad_end = jnp.cumsum(padded)
    pad_start = pad_end - padded
    slot = (pad_start[flat_e] + rank).astype(jnp.int32)
    n_blocks = -(-(n_assign + N_EXPERTS * (MOE_TM - 1)) // MOE_TM)
    n_slots = n_blocks * MOE_TM
    slot_tok = jnp.zeros((n_slots,), jnp.int32).at[slot].set(jnp.arange(n_assign, dtype=jnp.int32) // TOP_K)
    slot_p = jnp.zeros((n_slots,), F32).at[slot].set(top_p.reshape(-1))
    n_valid = (pad_end[-1] // MOE_TM).astype(jnp.int32)
    blk = jnp.arange(n_blocks, dtype=jnp.int32)
    block_e = jnp.searchsorted(pad_end, jnp.minimum(blk, n_valid - 1) * MOE_TM, side='right').astype(jnp.int32)
    block_e = jnp.minimum(block_e, N_EXPERTS - 1)
    return slot, slot_tok, slot_p, block_e, n_valid.reshape(1)


PREP_ROWS = 4096


def _prep_body(x_ref, w_ref, o_ref, *, t_first, t_rest):
    i = pl.program_id(0)
    j = pl.program_id(1)
    t_seq = jnp.where(i == 0, t_first, t_rest)
    x = x_ref[...]
    n = x.shape[0]
    row = lax.broadcasted_iota(jnp.int32, x.shape, 0)
    tpos = row & (t_seq - 1)
    xm = jnp.where(tpos == 0, 0.0, pltpu.roll(x, 1, axis=0))
    xp = jnp.where(tpos == t_seq - 1, 0.0, pltpu.roll(x, n - 1, axis=0))
    w = w_ref[...]
    y = xm * w[0:1] + x * w[1:2] + xp * w[2:3]
    y = y * jax.nn.sigmoid(y)
    rs = lax.rsqrt(jnp.sum(y * y, axis=-1, keepdims=True) + EPS)
    nq = QK_W // DK
    fac = jnp.where(j < nq, rs * (DK ** -0.5), jnp.where(j < 2 * nq, rs, 1.0))
    o_ref[...] = y * fac


def _qkv_prep(proj, conv_w, *, t_first, t_rest):
    n = proj.shape[0]
    return pl.pallas_call(
        functools.partial(_prep_body, t_first=t_first, t_rest=t_rest),
        name="qkv_prep",
        grid=(n // PREP_ROWS, QKV_W // DK),
        in_specs=[pl.BlockSpec((PREP_ROWS, DK), lambda i, j: (i, j)),
                  pl.BlockSpec((3, DK), lambda i, j: (0, j))],
        out_specs=pl.BlockSpec((PREP_ROWS, DK), lambda i, j: (i, j)),
        out_shape=jax.ShapeDtypeStruct((n, QKV_W), F32),
        compiler_params=pltpu.CompilerParams(dimension_semantics=("parallel", "parallel"),
                                             vmem_limit_bytes=VMEM_LIMIT_BYTES),
    )(proj, conv_w)


PROMPT_SUB = 2
SOLVE_WAYS = 8


def _mmul(a, b):
    return jnp.dot(a, b, preferred_element_type=F32)


def _delta_body(q_ref, k_ref, v_ref, z_ref, cols_ref, rows_ref, nw_ref, s0f_ref, s0b_ref, o_prev_ref,
                o_ref, sf_ref, sb_ref, acc_ref, s_ref, wu_ref, a_ref, *, n_sub, n_chunks, zero_init):
    del o_prev_ref
    n_all = n_sub * n_chunks
    ways = min(SOLVE_WAYS, n_all)
    lane = lax.broadcasted_iota(jnp.int32, (CHUNK, 2 * CHUNK), 1)
    row = lax.broadcasted_iota(jnp.int32, (CHUNK, 2 * CHUNK), 0)
    pos = lane & (CHUNK - 1)
    left = lane < CHUNK
    left1 = lax.broadcasted_iota(jnp.int32, (1, 2 * CHUNK), 1) < CHUNK
    eye2 = (pos == row).astype(F32)
    incl = (pos <= row, pos >= row)
    strict = (pos < row, pos > row)
    same8 = (row >> 3) == (pos >> 3)
    same16 = (row >> 4) == (pos >> 4)
    same32 = (row >> 5) == (pos >> 5)
    nt = (((1,), (1,)), ((), ()))
    tn = (((0,), (0,)), ((), ()))

    def split(m):
        return jnp.where(left, m, 0.0), jnp.where(left, 0.0, m)

    def bdiag(m):
        return jnp.concatenate(split(m), axis=0).astype(BF16)

    def pmm(a, b):
        return _mmul(a.astype(BF16), bdiag(b))

    def inv_unit_tri(lms):
        d8 = [jnp.where(same8, lm, 0.0) for lm in lms]
        xs = [eye2 - d for d in d8]
        d2 = [pmm(d, d) for d in d8]
        xs = [x + pmm(x, d) for x, d in zip(xs, d2)]
        d4 = [pmm(d, d) for d in d2]
        xs = [x + pmm(x, d) for x, d in zip(xs, d4)]
        for keep in (same16 & ~same8, same32 & ~same16, ~same32):
            ys = [pmm(jnp.where(keep, lm, 0.0), x) for lm, x in zip(lms, xs)]
            xs = [x - pmm(x, y) for x, y in zip(xs, ys)]
        return xs

    def chunk_terms(c, d):
        r0 = pl.multiple_of(c * CHUNK, CHUNK)
        cols = cols_ref[pl.ds(r0, CHUNK), :]
        rows = rows_ref[c]
        gcol = [cols[:, 2 * d + h:2 * d + h + 1] for h in (0, 1)]
        bcol = [cols[:, 4 + 2 * d + h:4 + 2 * d + h + 1] for h in (0, 1)]
        gl = [rows[4 + 2 * d + h:5 + 2 * d + h] for h in (0, 1)]
        return r0, rows, gcol, bcol, gl

    acc_ref[...] = jnp.zeros_like(acc_ref)
    rep = s0f_ref.shape[1]
    if zero_init:
        s_ref[...] = jnp.zeros_like(s_ref)
    else:
        for sub in range(n_sub):
            for h in range(rep):
                s_ref[2 * rep * sub + h] = s0f_ref[sub, h]
                s_ref[2 * rep * sub + rep + h] = s0b_ref[sub, h]

    def solve(i, carry):
        chunks = [i + part * (n_all // ways) for part in range(ways)]
        kcs, vcs, kkms, qkms = [], [], [], []
        for c in chunks:
            r0 = pl.multiple_of(c * CHUNK, CHUNK)
            kc = k_ref[pl.ds(r0, CHUNK), :]
            kb = kc.astype(BF16)
            kk2 = jnp.concatenate([kb, kb], axis=0)
            kcs.append(kc)
            vcs.append(v_ref[pl.ds(r0, CHUNK), :])
            kkms.append(lax.dot_general(kb, kk2, nt, preferred_element_type=F32))
            qkms.append(lax.dot_general(q_ref[pl.ds(r0, CHUNK), :].astype(BF16), kk2, nt,
                                        preferred_element_type=F32))
        chains = [(j, d) for j in range(ways) for d in (0, 1)]
        lms, rhs2s = [], []
        for j, d in chains:
            c = chunks[j]
            _, rows, gcol, bcol, _ = chunk_terms(c, d)
            gcol2 = jnp.where(left, gcol[0], gcol[1])
            bcol2 = jnp.where(left, bcol[0], bcol[1])
            grow2 = jnp.where(left1, rows[2 * d:2 * d + 1], rows[2 * d + 1:2 * d + 2])
            dm = jnp.exp(jnp.where(incl[d], gcol2 - grow2, -jnp.inf))
            a_ref[d, c] = (qkms[j] * dm).astype(BF16)
            lms.append(jnp.where(strict[d], kkms[j] * dm * bcol2, 0.0))
            rhs = [jnp.concatenate([vcs[j][:, h * DV:(h + 1) * DV] * bcol[h],
                                    kcs[j] * (bcol[h] * jnp.exp(gcol[h]))], axis=1) for h in (0, 1)]
            rhs2s.append(jnp.concatenate(rhs, axis=0).astype(BF16))
        t2s = inv_unit_tri(lms)
        sols = [[_mmul(th.astype(BF16), rhs2) for th in split(t2)]
                for t2, rhs2 in zip(t2s, rhs2s)]
        for (j, d), sol in zip(chains, sols):
            wu_ref[d, chunks[j]] = jnp.concatenate(sol, axis=1).astype(BF16)
        return carry

    lax.fori_loop(0, n_all // ways, solve, 0)

    def step(i, carry):
        lanes = [(sub, d) for sub in range(n_sub) for d in (0, 1)]
        where = [sub * n_chunks + (i if d == 0 else n_chunks - 1 - i) for sub, d in lanes]
        terms = [chunk_terms(c, d) for c, (_, d) in zip(where, lanes)]
        aws, ngs, qcs = [], [], []
        for c, (_, d), (r0, _, gcol, _, gl) in zip(where, lanes, terms):
            qcs.append(q_ref[pl.ds(r0, CHUNK), :])
            kb = k_ref[pl.ds(r0, CHUNK), :].astype(BF16)
            wu = wu_ref[d, c]
            wu2 = jnp.concatenate([wu[:, :2 * DV], wu[:, 2 * DV:]], axis=0)
            al, ar = split(a_ref[d, c].astype(F32))
            aws.append([_mmul(al.astype(BF16), wu2), _mmul(ar.astype(BF16), wu2)])
            ek = [jnp.exp(gl[h][:, 0:1] - gcol[h]) for h in (0, 1)]
            wuf = wu.astype(F32)
            zz = jnp.concatenate([wuf[:, :2 * DV] * ek[0], wuf[:, 2 * DV:] * ek[1]], axis=1).astype(BF16)
            ngs.append(lax.dot_general(kb, zz, tn, preferred_element_type=F32))
        streams = [(n, h) for n in range(len(lanes)) for h in (0, 1)]
        outs = []
        for n, h in streams:
            p = qcs[n] * jnp.exp(terms[n][2][h]) - aws[n][h][:, DV:]
            pg = jnp.concatenate([p, ngs[n][:, (2 * h + 1) * DV:(2 * h + 2) * DV]], axis=0).astype(BF16)
            outs.append(_mmul(pg, s_ref[2 * n + h].astype(BF16)))
        for (n, h), out in zip(streams, outs):
            s = 2 * n + h
            r0, gl = terms[n][0], terms[n][4]
            acc_ref[pl.ds(r0, CHUNK), h * DV:(h + 1) * DV] += out[:CHUNK] + aws[n][h][:, :DV]
            s_ref[s] = s_ref[s] * jnp.exp(gl[h]) - out[CHUNK:] + ngs[n][:, 2 * h * DV:(2 * h + 1) * DV]
        return carry

    lax.fori_loop(0, n_chunks, step, 0)

    nw = nw_ref[...]

    def finish(j, carry):
        r0 = pl.multiple_of(j * CHUNK, CHUNK)
        o = acc_ref[pl.ds(r0, CHUNK), :]
        z = z_ref[pl.ds(r0, CHUNK), :]
        for h in (0, 1):
            oh = o[:, h * DV:(h + 1) * DV]
            zh = z[:, h * DV:(h + 1) * DV]
            y = oh * lax.rsqrt(jnp.mean(oh * oh, axis=-1, keepdims=True) + EPS) * nw * (zh * jax.nn.sigmoid(zh))
            o_ref[pl.ds(r0, CHUNK), h * DV:(h + 1) * DV] = y.astype(o_ref.dtype)
        return carry

    lax.fori_loop(0, n_all, finish, 0)
    for sub in range(n_sub):
        for h in range(rep):
            sf_ref[sub, h] = s_ref[2 * rep * sub + h]
            sb_ref[sub, h] = s_ref[2 * rep * sub + rep + h]


def _deltanet(qkvn, proj, cols, rows, norm_w, s0f, s0b, o_prev, *, n_seq, t_seq, n_sub, row_blk0, zero_init):
    n_chunks = t_seq // CHUNK
    t_blk = n_sub * t_seq
    c_blk = n_sub * n_chunks
    nq = QK_W // DK
    z_blk0 = QKV_W // (2 * DV)
    rep = NV_HEADS // NK_HEADS
    state_spec = pl.BlockSpec((n_sub, rep, DK, DV), lambda b, kh: (b, kh, 0, 0))
    in_specs = [
        pl.BlockSpec((t_blk, DK), lambda b, kh: (row_blk0 + b, kh)),
        pl.BlockSpec((t_blk, DK), lambda b, kh: (row_blk0 + b, nq + kh)),
        pl.BlockSpec((t_blk, rep * DV), lambda b, kh: (row_blk0 + b, nq + kh)),
        pl.BlockSpec((t_blk, rep * DV), lambda b, kh: (row_blk0 + b, z_blk0 + kh)),
        pl.BlockSpec((None, t_blk, 8), lambda b, kh: (kh, row_blk0 + b, 0)),
        pl.BlockSpec((None, c_blk, 8, 2 * CHUNK), lambda b, kh: (kh, row_blk0 + b, 0, 0)),
        pl.BlockSpec((1, DV), lambda b, kh: (0, 0)),
        pl.BlockSpec((n_sub, rep, DK, DV), lambda b, kh: (0 if zero_init else b, kh, 0, 0)),
        pl.BlockSpec((n_sub, rep, DK, DV), lambda b, kh: (0 if zero_init else b, kh, 0, 0)),
        pl.BlockSpec(memory_space=pl.ANY),
    ]
    out_specs = [pl.BlockSpec((t_blk, rep * DV), lambda b, kh: (row_blk0 + b, kh)), state_spec, state_spec]
    out_shape = [
        jax.ShapeDtypeStruct((qkvn.shape[0], V_W), BF16),
        jax.ShapeDtypeStruct((n_seq, NV_HEADS, DK, DV), F32),
        jax.ShapeDtypeStruct((n_seq, NV_HEADS, DK, DV), F32),
    ]
    args = [qkvn, qkvn, qkvn, proj, cols, rows, norm_w.reshape(1, DV), s0f, s0b, o_prev]
    return pl.pallas_call(
        functools.partial(_delta_body, n_sub=n_sub, n_chunks=n_chunks, zero_init=zero_init),
        grid=(n_seq // n_sub, NK_HEADS),
        input_output_aliases={len(args) - 1: 0},
        name="deltanet",
        in_specs=in_specs,
        out_specs=out_specs,
        out_shape=out_shape,
        scratch_shapes=[pltpu.VMEM((t_blk, rep * DV), F32), pltpu.VMEM((2 * rep * n_sub, DK, DV), F32),
                        pltpu.VMEM((2, c_blk, CHUNK, 2 * rep * DV), BF16),
                        pltpu.VMEM((2, c_blk, CHUNK, 2 * CHUNK), BF16)],
        compiler_params=pltpu.CompilerParams(dimension_semantics=("parallel", "parallel"),
                                             vmem_limit_bytes=VMEM_LIMIT_BYTES),
    )(*args)


def _decay_tables(ab, a_log, dt_bias):
    n = ab.shape[0]
    nh2 = 2 * NV_HEADS
    rep = NV_HEADS // NK_HEADS
    beta = jax.nn.sigmoid(ab[:, :nh2])
    g = -jnp.exp(a_log.reshape(nh2)) * jax.nn.softplus(ab[:, nh2:] + dt_bias.reshape(nh2))
    gch = g.reshape(n // CHUNK, CHUNK, 2, NV_HEADS)
    gcf = jnp.cumsum(gch[:, :, 0], axis=1)
    gcb = jnp.flip(jnp.cumsum(jnp.flip(gch[:, :, 1], 1), axis=1), 1)
    gc = jnp.stack([gcf, gcb], axis=2)
    gl = jnp.stack([gcf[:, -1], gcb[:, 0]], axis=1)

    def per_kh(a):
        a = a.reshape(a.shape[:-2] + (2, NK_HEADS, rep))
        return jnp.moveaxis(a, -3, -2).reshape(a.shape[:-3] + (NK_HEADS, 2 * rep))
    gc_k = per_kh(gc)
    cols = jnp.concatenate([gc_k.reshape(n, NK_HEADS, 2 * rep), per_kh(beta.reshape(n, 2, NV_HEADS))], axis=-1)
    cols = jnp.transpose(cols, (1, 0, 2))
    gr = jnp.transpose(gc_k, (2, 0, 3, 1))
    gr = jnp.concatenate([gr, gr], axis=-1)
    glr = jnp.broadcast_to(jnp.transpose(per_kh(gl), (1, 0, 2))[..., None], gr.shape)
    rows = jnp.concatenate([gr, glr], axis=2)
    return cols, rows


TOK_TILE = 512
MIX_TILE = 256
COMBINE_TILE = 128


def _group_of_tile(i, tile, n_prompt, t_sample):
    first = n_prompt // tile
    return jnp.where(i < first, 0, 1 + (i - first) // (t_sample // tile))


def _token_specs(tile, n_prompt, t_sample, d):
    first = n_prompt // tile
    per_seq = t_sample // tile
    return [
        pl.BlockSpec((tile, d), lambda i, *_: (jnp.minimum(i, first - 1), 0)),
        pl.BlockSpec((tile, d), lambda i, *_: (jnp.maximum(i - first, 0), 0)),
        pl.BlockSpec((tile, d), lambda i, *_: (jnp.maximum(i - first, 0) % per_seq, 0)),
        pl.BlockSpec((None, 1, 6 * d), lambda i, *_: (_group_of_tile(i, tile, n_prompt, t_sample), 0, 0)),
    ]


def _token_rows(i, first, xp_ref, xs_ref, pos_ref):
    return jnp.where(i < first, xp_ref[...], xs_ref[...] + pos_ref[...])


def _modulate_body(xp_ref, xs_ref, pos_ref, mod_ref, h_ref, *, first):
    d = h_ref.shape[-1]
    x = _token_rows(pl.program_id(0), first, xp_ref, xs_ref, pos_ref)
    h_ref[...] = (x * (1.0 + mod_ref[:, d:2 * d]) + mod_ref[:, 0:d]).astype(h_ref.dtype)


def _modulate(xp, xs, pos, mod, *, t_sample):
    n_p, d = xp.shape
    n = n_p + xs.shape[0]
    return pl.pallas_call(
        functools.partial(_modulate_body, first=n_p // TOK_TILE),
        grid=(n // TOK_TILE,),
        in_specs=_token_specs(TOK_TILE, n_p, t_sample, d),
        out_specs=pl.BlockSpec((TOK_TILE, d), lambda i: (i, 0)),
        out_shape=jax.ShapeDtypeStruct((n, d), BF16),
        compiler_params=pltpu.CompilerParams(dimension_semantics=("parallel",), vmem_limit_bytes=VMEM_LIMIT_BYTES),
        name="modulate",
    )(xp, xs, pos, mod)


def _gconv_body(ub_ref, uc_ref, ux_ref, w_ref, o_ref, *, t_first, t_rest):
    t_seq = jnp.where(pl.program_id(0) == 0, t_first, t_rest)
    x = uc_ref[...] * ux_ref[...]
    n = x.shape[0]
    tpos = lax.broadcasted_iota(jnp.int32, x.shape, 0) & (t_seq - 1)
    xm = jnp.where(tpos == 0, 0.0, pltpu.roll(x, 1, axis=0))
    xp = jnp.where(tpos == t_seq - 1, 0.0, pltpu.roll(x, n - 1, axis=0))
    w = w_ref[...]
    o_ref[...] = (ub_ref[...] * (xm * w[0:1] + x * w[1:2] + xp * w[2:3])).astype(o_ref.dtype)


def _gated_conv(proj, conv_w, col0, *, t_first, t_rest):
    n = proj.shape[0]
    nblk = CONV_CH // DK
    b0 = col0 // DK
    return pl.pallas_call(
        functools.partial(_gconv_body, t_first=t_first, t_rest=t_rest),
        grid=(n // PREP_ROWS, nblk),
        in_specs=[pl.BlockSpec((PREP_ROWS, DK), lambda i, j: (i, b0 + j)),
                  pl.BlockSpec((PREP_ROWS, DK), lambda i, j: (i, b0 + nblk + j)),
                  pl.BlockSpec((PREP_ROWS, DK), lambda i, j: (i, b0 + 2 * nblk + j)),
                  pl.BlockSpec((3, DK), lambda i, j: (0, j))],
        out_specs=pl.BlockSpec((PREP_ROWS, DK), lambda i, j: (i, j)),
        out_shape=jax.ShapeDtypeStruct((n, CONV_CH), BF16),
        compiler_params=pltpu.CompilerParams(dimension_semantics=("parallel", "parallel"),
                                             vmem_limit_bytes=VMEM_LIMIT_BYTES),
        name="gated_conv",
    )(proj, proj, proj, conv_w)


BR_TM = 512
BR_TN = 512


def _branch_body(o_ref, g_ref, wa_ref, wb_ref, ra_ref, rb_ref, m_ref, wa_bf, wb_bf):
    @pl.when(pl.program_id(1) == 0)
    def _():
        wa_bf[...] = wa_ref[...].astype(BF16)
        wb_bf[...] = wb_ref[...].astype(BF16)
    ya = jnp.dot(o_ref[...], wa_bf[...], preferred_element_type=F32)
    yb = jnp.dot(g_ref[...], wb_bf[...], preferred_element_type=F32)
    m_ref[...] = (jax.nn.sigmoid(ra_ref[...]) * ya + jax.nn.sigmoid(rb_ref[...]) * yb).astype(m_ref.dtype)


def _branch_merge(o, g, w_a, w_b, proj, ra_col, rb_col):
    n = o.shape[0]
    d = w_a.shape[1]
    el = pl.Element
    return pl.pallas_call(
        _branch_body,
        grid=(d // BR_TN, n // BR_TM),
        in_specs=[pl.BlockSpec((BR_TM, o.shape[1]), lambda j, i: (i, 0)),
                  pl.BlockSpec((BR_TM, g.shape[1]), lambda j, i: (i, 0)),
                  pl.BlockSpec((w_a.shape[0], BR_TN), lambda j, i: (0, j)),
                  pl.BlockSpec((w_b.shape[0], BR_TN), lambda j, i: (0, j)),
                  pl.BlockSpec((el(BR_TM), el(BR_TN)),
                               lambda j, i: (i * BR_TM, pl.multiple_of(ra_col + j * BR_TN, DK))),
                  pl.BlockSpec((el(BR_TM), el(BR_TN)),
                               lambda j, i: (i * BR_TM, pl.multiple_of(rb_col + j * BR_TN, DK)))],
        out_specs=pl.BlockSpec((BR_TM, BR_TN), lambda j, i: (i, j)),
        out_shape=jax.ShapeDtypeStruct((n, d), BF16),
        scratch_shapes=[pltpu.VMEM((w_a.shape[0], BR_TN), BF16), pltpu.VMEM((w_b.shape[0], BR_TN), BF16)],
        compiler_params=pltpu.CompilerParams(dimension_semantics=("arbitrary", "arbitrary"),
                                             vmem_limit_bytes=VMEM_LIMIT_BYTES),
        name="branch_merge",
    )(o, g, w_a, w_b, proj, proj)


def _ln_rows(v, g, b):
    mu = jnp.mean(v, axis=-1, keepdims=True)
    c = v - mu
    return c * lax.rsqrt(jnp.mean(c * c, axis=-1, keepdims=True) + EPS) * g + b


def _mix_body(m_ref, wo_ref, xp_ref, xs_ref, pos_ref, mod_ref, g_ref, b_ref, rw_ref, rb_ref,
              x1_ref, h2_ref, lg_ref, *, first):
    d = x1_ref.shape[-1]
    x = _token_rows(pl.program_id(0), first, xp_ref, xs_ref, pos_ref)
    mixed = jnp.dot(m_ref[...], wo_ref[...], preferred_element_type=F32)
    x1 = _ln_rows(DEEPNORM_ALPHA * x + mod_ref[:, 2 * d:3 * d] * mixed, g_ref[...], b_ref[...])
    h2 = x1 * (1.0 + mod_ref[:, 4 * d:5 * d]) + mod_ref[:, 3 * d:4 * d]
    x1_ref[...] = x1
    h2_ref[...] = h2.astype(h2_ref.dtype)
    lg_ref[...] = jnp.dot(h2, rw_ref[...], precision=lax.Precision.HIGHEST, preferred_element_type=F32) + rb_ref[...]


def _mix_norm_route(m, w_o_bf, xp, xs, pos, mod, ln_g, ln_b, router_w, router_b, *, t_sample):
    n, d = m.shape
    n_p = xp.shape[0]
    ne = router_w.shape[1]
    const = lambda i: (0, 0)
    return pl.pallas_call(
        functools.partial(_mix_body, first=n_p // MIX_TILE),
        grid=(n // MIX_TILE,),
        in_specs=[pl.BlockSpec((MIX_TILE, d), lambda i: (i, 0)), pl.BlockSpec((d, d), const)]
        + _token_specs(MIX_TILE, n_p, t_sample, d)
        + [pl.BlockSpec((1, d), const), pl.BlockSpec((1, d), const),
           pl.BlockSpec((d, ne), const), pl.BlockSpec((1, ne), const)],
        out_specs=[pl.BlockSpec((MIX_TILE, d), lambda i: (i, 0)), pl.BlockSpec((MIX_TILE, d), lambda i: (i, 0)),
                   pl.BlockSpec((MIX_TILE, ne), lambda i: (i, 0))],
        out_shape=[jax.ShapeDtypeStruct((n, d), F32), jax.ShapeDtypeStruct((n, d), F32),
                   jax.ShapeDtypeStruct((n, ne), F32)],
        compiler_params=pltpu.CompilerParams(dimension_semantics=("parallel",), vmem_limit_bytes=VMEM_LIMIT_BYTES),
        name="mix_norm_route",
    )(m, w_o_bf, xp, xs, pos, mod, ln_g.reshape(1, d), ln_b.reshape(1, d), router_w, router_b.reshape(1, ne))


DISPATCH_ROWS = 512


def _dispatch_body(tok_ref, h_hbm, xs_hbm, sem, *, n_steps):
    i = pl.program_id(0)
    base = i * DISPATCH_ROWS

    def batch_wait(which):
        pltpu.make_async_copy(h_hbm.at[pl.ds(0, DISPATCH_ROWS), :], xs_hbm.at[pl.ds(0, DISPATCH_ROWS), :],
                              sem.at[which]).wait()

    def body(r, carry):
        pltpu.make_async_copy(h_hbm.at[pl.ds(tok_ref[base + r], 1), :], xs_hbm.at[pl.ds(base + r, 1), :],
                              sem.at[i % 2]).start()
        return carry
    lax.fori_loop(0, DISPATCH_ROWS, body, 0)

    @pl.when(i > 0)
    def _():
        batch_wait((i - 1) % 2)

    @pl.when(i == n_steps - 1)
    def _():
        batch_wait(i % 2)


def _dispatch(slot_tok, h):
    n_slots = slot_tok.shape[0]
    n_steps = n_slots // DISPATCH_ROWS
    grid_spec = pltpu.PrefetchScalarGridSpec(
        num_scalar_prefetch=1,
        grid=(n_steps,),
        in_specs=[pl.BlockSpec(memory_space=pl.ANY)],
        out_specs=pl.BlockSpec(memory_space=pl.ANY),
        scratch_shapes=[pltpu.SemaphoreType.DMA((2,))],
    )
    return pl.pallas_call(
        functools.partial(_dispatch_body, n_steps=n_steps),
        grid_spec=grid_spec,
        out_shape=jax.ShapeDtypeStruct((n_slots, h.shape[1]), h.dtype),
        compiler_params=pltpu.CompilerParams(dimension_semantics=("arbitrary",)),
        name="dispatch",
    )(slot_tok, h)


def _combine_body(slot_ref, y_hbm, x1_ref, mod_ref, g_ref, b_ref, o_ref, buf, sem, *, n_tiles):
    tt = COMBINE_TILE
    d = o_ref.shape[-1]
    i = pl.program_id(0)

    def row_copy(src_row, dst_row, which):
        return pltpu.make_async_copy(y_hbm.at[pl.ds(src_row, 1), :], buf.at[which, pl.ds(dst_row, 1), :],
                                     sem.at[which])

    def issue(tile, which):
        base = tile * (tt * TOP_K)

        def body(t, carry):
            for k in range(TOP_K):
                row_copy(slot_ref[base + t * TOP_K + k], k * tt + t, which).start()
            return carry
        lax.fori_loop(0, tt, body, 0)

    @pl.when(i == 0)
    def _():
        issue(0, 0)

    @pl.when(i + 1 < n_tiles)
    def _():
        issue(i + 1, (i + 1) % 2)

    cur = i % 2
    pltpu.make_async_copy(y_hbm.at[pl.ds(0, TOP_K * tt), :], buf.at[cur], sem.at[cur]).wait()
    ff = buf[cur, 0:tt, :]
    for k in range(1, TOP_K):
        ff = ff + buf[cur, k * tt:(k + 1) * tt, :]
    o_ref[...] = _ln_rows(DEEPNORM_ALPHA * x1_ref[...] + mod_ref[:, 5 * d:6 * d] * ff, g_ref[...], b_ref[...])


def _combine_norm(slot, y, x1, mod, ln_g, ln_b, *, n_prompt, t_sample):
    n, d = x1.shape
    tt = COMBINE_TILE
    n_tiles = n // tt
    grid_spec = pltpu.PrefetchScalarGridSpec(
        num_scalar_prefetch=1,
        grid=(n_tiles,),
        in_specs=[pl.BlockSpec(memory_space=pl.ANY),
                  pl.BlockSpec((tt, d), lambda i, s: (i, 0)),
                  pl.BlockSpec((None, 1, 6 * d), lambda i, s: (_group_of_tile(i, tt, n_prompt, t_sample), 0, 0)),
                  pl.BlockSpec((1, d), lambda i, s: (0, 0)),
                  pl.BlockSpec((1, d), lambda i, s: (0, 0))],
        out_specs=pl.BlockSpec((tt, d), lambda i, s: (i, 0)),
        scratch_shapes=[pltpu.VMEM((2, TOP_K * tt, d), F32), pltpu.SemaphoreType.DMA((2,))],
    )
    return pl.pallas_call(
        functools.partial(_combine_body, n_tiles=n_tiles),
        grid_spec=grid_spec,
        out_shape=jax.ShapeDtypeStruct((n, d), F32),
        compiler_params=pltpu.CompilerParams(dimension_semantics=("arbitrary",), vmem_limit_bytes=VMEM_LIMIT_BYTES),
        name="combine_norm",
    )(slot, y, x1, mod, ln_g.reshape(1, d), ln_b.reshape(1, d))


def _grid_pos_embed(rows, dim):
    quarter = dim // 4
    freqs = 1.0 / (POS_BASE ** (jnp.arange(quarter, dtype=F32) / quarter))

    def axis_embed(n):
        ang = jnp.arange(n, dtype=F32)[:, None] * freqs[None, :]
        return jnp.concatenate([jnp.sin(ang), jnp.cos(ang)], axis=-1)
    er = axis_embed(rows)
    ec = axis_embed(GRID_W)
    pos = jnp.concatenate([jnp.broadcast_to(er[:, None, :], (rows, GRID_W, dim // 2)),
                           jnp.broadcast_to(ec[None, :, :], (rows, GRID_W, dim // 2))], axis=-1)
    return pos.reshape(rows * GRID_W, dim)


def kernel(x_prompt, x_sample, c, state_fwd, state_bwd, c_ctx, w_mod, b_mod, w_in, conv_qkv, a_log, dt_bias,
           norm_o, w_a_out, conv_b, w_b_out, w_o, ln1_g, ln1_b, router_w, router_b, w_gu, b_gu, w_down, b_down,
           ln2_g, ln2_b):
    nb_p, t_p, d = x_prompt.shape
    nb_s, t_s, _ = x_sample.shape
    n_p = nb_p * t_p
    n_s = nb_s * t_s
    assert n_p == PREP_ROWS and t_s == PREP_ROWS
    xp = x_prompt.reshape(n_p, d)
    xs = x_sample.reshape(n_s, d)
    pos = _grid_pos_embed(t_s // GRID_W, d)
    offs = tuple(int(o) for o in np.cumsum(IN_SIZES)[:-1])

    l = 0
    cvec = jnp.concatenate([c_ctx[None, :], c], axis=0)
    cpad = jnp.zeros((8, d), F32).at[:1 + nb_s].set(jax.nn.silu(cvec))
    mod = (_mm(cpad, w_mod[l], tm=8, tn=512)[:1 + nb_s] + b_mod[l])[:, None, :]

    h = _modulate(xp, xs, pos, mod, t_sample=t_s)
    proj = _mm(h, w_in[l], tm=2048, tn=384)

    qkvn = _qkv_prep(proj, conv_qkv[l], t_first=t_p, t_rest=t_s)
    cols, rws = _decay_tables(proj[:, offs[1]:offs[3]], a_log[l], dt_bias[l])
    zero_state = jnp.zeros((PROMPT_SUB, NV_HEADS, DK, DV), F32)
    o_all = jnp.zeros((n_p + n_s, V_W), BF16)
    o_all, sf, sb = _deltanet(qkvn, proj, cols, rws, norm_o[l], zero_state, zero_state, o_all,
                              n_seq=nb_p, t_seq=t_p, n_sub=PROMPT_SUB, row_blk0=0, zero_init=True)
    o_all, _, _ = _deltanet(qkvn, proj, cols, rws, norm_o[l], state_fwd[:, l], state_bwd[:, l], o_all,
                            n_seq=nb_s, t_seq=t_s, n_sub=1, row_blk0=n_p // t_s, zero_init=False)

    gconv = _gated_conv(proj, conv_b[l], offs[3], t_first=t_p, t_rest=t_s)
    m = _branch_merge(o_all, gconv, w_a_out[l], w_b_out[l], proj, offs[6], offs[7])
    x1, h2, logits = _mix_norm_route(m, w_o[l].astype(BF16), xp, xs, pos, mod, ln1_g[l], ln1_b[l],
                                     router_w[l], router_b[l], t_sample=t_s)
    slot, slot_tok, slot_p, block_e, n_valid = _route(logits)
    y = _moe_blocks(_dispatch(slot_tok, h2), slot_p, block_e, n_valid, w_gu[l], b_gu[l], w_down[l], b_down[l])
    x2 = _combine_norm(slot, y, x1, mod, ln2_g[l], ln2_b[l], n_prompt=n_p, t_sample=t_s)
    y_prompt = x2[:n_p].reshape(nb_p, t_p, d)
    y_sample = x2[n_p:].reshape(nb_s, t_s, d)
    return (y_prompt, y_sample, sf[:, None], sb[:, None])
```

```python
import functools

import jax
import jax.numpy as jnp
import numpy as np
from jax import lax
from jax.experimental import pallas as pl
from jax.experimental.pallas import tpu as pltpu

F32 = jnp.float32
BF16 = jnp.bfloat16

D_MODEL = 2048
GRID_W = 64
NK_HEADS = 16
NV_HEADS = 32
DK = 128
DV = 128
QK_W = NK_HEADS * DK
V_W = NV_HEADS * DV
QKV_W = 2 * QK_W + V_W
CHUNK = 64
CONV_CH = D_MODEL
IN_SIZES = (QKV_W, V_W, 2 * NV_HEADS, 2 * NV_HEADS, CONV_CH, CONV_CH, CONV_CH, D_MODEL, D_MODEL)
N_EXPERTS = 32
TOP_K = 4
D_FF = D_MODEL
SWIGLU_LIMIT = 7.0
SWIGLU_ALPHA = 1.702
DEPTH = 1
DEEPNORM_ALPHA = (2 * DEPTH) ** 0.25
EPS = 1e-6
POS_BASE = 10000.0

VMEM_LIMIT_BYTES = 56 * 1024 * 1024


def _mm_body(x_ref, w_ref, o_ref):
    o_ref[...] = jnp.dot(x_ref[...].astype(BF16), w_ref[...].astype(BF16), preferred_element_type=F32)


def _mm(x, w, *, tm, tn):
    m, k = x.shape
    n = w.shape[1]
    assert m % tm == 0 and n % tn == 0
    return pl.pallas_call(
        _mm_body,
        name="matmul",
        grid=(m // tm, n // tn),
        in_specs=[pl.BlockSpec((tm, k), lambda i, j: (i, 0)),
                  pl.BlockSpec((k, tn), lambda i, j: (0, j))],
        out_specs=pl.BlockSpec((tm, tn), lambda i, j: (i, j)),
        out_shape=jax.ShapeDtypeStruct((m, n), F32),
        compiler_params=pltpu.CompilerParams(dimension_semantics=("parallel", "parallel"),
                                             vmem_limit_bytes=VMEM_LIMIT_BYTES),
    )(x, w)


MOE_TM = 512
MOE_TF = 512


def _moe_body(be_ref, nv_ref, x_ref, wg_ref, wu_ref, bg_ref, bu_ref, wd_ref, bd_ref, o_ref, acc_ref):
    i = pl.program_id(0)
    f = pl.program_id(1)
    valid = i < nv_ref[0]

    @pl.when(f == 0)
    def _():
        acc_ref[...] = jnp.zeros_like(acc_ref)

    @pl.when(valid)
    def _():
        x = x_ref[...].astype(BF16)
        g = jnp.dot(x, wg_ref[...].astype(BF16), preferred_element_type=F32) + bg_ref[...]
        u = jnp.dot(x, wu_ref[...].astype(BF16), preferred_element_type=F32) + bu_ref[...]
        gate = jnp.minimum(g, SWIGLU_LIMIT)
        up = jnp.clip(u, -SWIGLU_LIMIT, SWIGLU_LIMIT)
        act = (up + 1.0) * gate * jax.nn.sigmoid(SWIGLU_ALPHA * gate)
        acc_ref[...] += jnp.dot(act.astype(BF16), wd_ref[...].astype(BF16), preferred_element_type=F32)

    @pl.when(f == pl.num_programs(1) - 1)
    def _():
        o_ref[...] = jnp.where(valid, acc_ref[...] + bd_ref[...], 0.0)


def _moe_blocks(xs, block_e, n_valid, w_gu, b_gu, w_down, b_down):
    n_slots, d = xs.shape
    n_blocks = n_slots // MOE_TM
    nf = D_FF // MOE_TF

    def f_eff(i, f, nv):
        return jnp.where(i < nv[0], f, nf - 1)

    grid_spec = pltpu.PrefetchScalarGridSpec(
        num_scalar_prefetch=2,
        grid=(n_blocks, nf),
        in_specs=[
            pl.BlockSpec((MOE_TM, d), lambda i, f, be, nv: (i, 0)),
            pl.BlockSpec((None, d, MOE_TF), lambda i, f, be, nv: (be[i], 0, f_eff(i, f, nv))),
            pl.BlockSpec((None, d, MOE_TF), lambda i, f, be, nv: (be[i], 0, nf + f_eff(i, f, nv))),
            pl.BlockSpec((None, 1, MOE_TF), lambda i, f, be, nv: (be[i], 0, f_eff(i, f, nv))),
            pl.BlockSpec((None, 1, MOE_TF), lambda i, f, be, nv: (be[i], 0, nf + f_eff(i, f, nv))),
            pl.BlockSpec((None, MOE_TF, d), lambda i, f, be, nv: (be[i], f_eff(i, f, nv), 0)),
            pl.BlockSpec((None, 1, d), lambda i, f, be, nv: (be[i], 0, 0)),
        ],
        out_specs=pl.BlockSpec((MOE_TM, d), lambda i, f, be, nv: (i, 0)),
        scratch_shapes=[pltpu.VMEM((MOE_TM, d), F32)],
    )
    return pl.pallas_call(
        _moe_body,
        name="moe_experts",
        grid_spec=grid_spec,
        out_shape=jax.ShapeDtypeStruct((n_slots, d), F32),
        compiler_params=pltpu.CompilerParams(dimension_semantics=("arbitrary", "arbitrary"),
                                             vmem_limit_bytes=VMEM_LIMIT_BYTES),
    )(block_e, n_valid, xs, w_gu, w_gu, b_gu[:, None, :], b_gu[:, None, :], w_down, b_down[:, None, :])


def _route(logits):
    n_tok = logits.shape[0]
    n_assign = n_tok * TOP_K
    top_logit, top_e = lax.top_k(logits, TOP_K)
    top_p = jax.nn.softmax(top_logit, axis=-1)
    flat_e = top_e.reshape(-1)
    onehot = (flat_e[:, None] == jnp.arange(N_EXPERTS)[None, :]).astype(jnp.int32)
    csum = jnp.cumsum(onehot, axis=0)
    counts = csum[-1]
    rank = jnp.take_along_axis(csum, flat_e[:, None], axis=1)[:, 0] - 1
    padded = (counts + MOE_TM - 1) // MOE_TM * MOE_TM
    pad_end = jnp.cumsum(padded)
    pad_start = pad_end - padded
    slot = (pad_start[flat_e] + rank).astype(jnp.int32)
    n_blocks = -(-(n_assign + N_EXPERTS * (MOE_TM - 1)) // MOE_TM)
    n_slots = n_blocks * MOE_TM
    slot_tok = jnp.zeros((n_slots,), jnp.int32).at[slot].set(jnp.arange(n_assign, dtype=jnp.int32) // TOP_K)
    n_valid = (pad_end[-1] // MOE_TM).astype(jnp.int32)
    blk = jnp.arange(n_blocks, dtype=jnp.int32)
    block_e = jnp.searchsorted(pad_end, jnp.minimum(blk, n_valid - 1) * MOE_TM, side='right').astype(jnp.int32)
    block_e = jnp.minimum(block_e, N_EXPERTS - 1)
    return slot, slot_tok, top_p, block_e, n_valid.reshape(1)


PREP_ROWS = 4096


def _prep_body(x_ref, w_ref, o_ref, *, t_first, t_rest):
    i = pl.program_id(0)
    j = pl.program_id(1)
    t_seq = jnp.where(i == 0, t_first, t_rest)
    x = x_ref[...]
    n = x.shape[0]
    row = lax.broadcasted_iota(jnp.int32, x.shape, 0)
    tpos = row & (t_seq - 1)
    xm = jnp.where(tpos == 0, 0.0, pltpu.roll(x, 1, axis=0))
    xp = jnp.where(tpos == t_seq - 1, 0.0, pltpu.roll(x, n - 1, axis=0))
    w = w_ref[...]
    y = xm * w[0:1] + x * w[1:2] + xp * w[2:3]
    y = y * jax.nn.sigmoid(y)
    rs = lax.rsqrt(jnp.sum(y * y, axis=-1, keepdims=True) + EPS)
    nq = QK_W // DK
    fac = jnp.where(j < nq, rs * (DK ** -0.5), jnp.where(j < 2 * nq, rs, 1.0))
    o_ref[...] = y * fac


def _qkv_prep(proj, conv_w, *, t_first, t_rest):
    n = proj.shape[0]
    return pl.pallas_call(
        functools.partial(_prep_body, t_first=t_first, t_rest=t_rest),
        name="qkv_prep",
        grid=(n // PREP_ROWS, QKV_W // DK),
        in_specs=[pl.BlockSpec((PREP_ROWS, DK), lambda i, j: (i, j)),
                  pl.BlockSpec((3, DK), lambda i, j: (0, j))],
        out_specs=pl.BlockSpec((PREP_ROWS, DK), lambda i, j: (i, j)),
        out_shape=jax.ShapeDtypeStruct((n, QKV_W), F32),
        compiler_params=pltpu.CompilerParams(dimension_semantics=("parallel", "parallel"),
                                             vmem_limit_bytes=VMEM_LIMIT_BYTES),
    )(proj, conv_w)


PROMPT_SUB = 2
SOLVE_WAYS = 8


def _mmul(a, b):
    return jnp.dot(a, b, preferred_element_type=F32)


def _delta_body(q_ref, k_ref, v_ref, z_ref, cols_ref, rows_ref, nw_ref, s0f_ref, s0b_ref, o_prev_ref,
                o_ref, sf_ref, sb_ref, acc_ref, s_ref, wu_ref, a_ref, *, n_sub, n_chunks, zero_init):
    del o_prev_ref
    n_all = n_sub * n_chunks
    ways = min(SOLVE_WAYS, n_all)
    lane = lax.broadcasted_iota(jnp.int32, (CHUNK, 2 * CHUNK), 1)
    row = lax.broadcasted_iota(jnp.int32, (CHUNK, 2 * CHUNK), 0)
    pos = lane & (CHUNK - 1)
    left = lane < CHUNK
    left1 = lax.broadcasted_iota(jnp.int32, (1, 2 * CHUNK), 1) < CHUNK
    eye2 = (pos == row).astype(F32)
    incl = (pos <= row, pos >= row)
    strict = (pos < row, pos > row)
    same8 = (row >> 3) == (pos >> 3)
    same16 = (row >> 4) == (pos >> 4)
    same32 = (row >> 5) == (pos >> 5)
    nt = (((1,), (1,)), ((), ()))
    tn = (((0,), (0,)), ((), ()))

    def split(m):
        return jnp.where(left, m, 0.0), jnp.where(left, 0.0, m)

    def bdiag(m):
        return jnp.concatenate(split(m), axis=0).astype(BF16)

    def pmm(a, b):
        return _mmul(a.astype(BF16), bdiag(b))

    def inv_unit_tri(lms):
        d8 = [jnp.where(same8, lm, 0.0) for lm in lms]
        xs = [eye2 - d for d in d8]
        d2 = [pmm(d, d) for d in d8]
        xs = [x + pmm(x, d) for x, d in zip(xs, d2)]
        d4 = [pmm(d, d) for d in d2]
        xs = [x + pmm(x, d) for x, d in zip(xs, d4)]
        for keep in (same16 & ~same8, same32 & ~same16, ~same32):
            ys = [pmm(jnp.where(keep, lm, 0.0), x) for lm, x in zip(lms, xs)]
            xs = [x - pmm(x, y) for x, y in zip(xs, ys)]
        return xs

    def chunk_terms(c, d):
        r0 = pl.multiple_of(c * CHUNK, CHUNK)
        cols = cols_ref[pl.ds(r0, CHUNK), :]
        rows = rows_ref[c]
        gcol = [cols[:, 2 * d + h:2 * d + h + 1] for h in (0, 1)]
        bcol = [cols[:, 4 + 2 * d + h:4 + 2 * d + h + 1] for h in (0, 1)]
        gl = [rows[4 + 2 * d + h:5 + 2 * d + h] for h in (0, 1)]
        return r0, rows, gcol, bcol, gl

    acc_ref[...] = jnp.zeros_like(acc_ref)
    rep = s0f_ref.shape[1]
    if zero_init:
        s_ref[...] = jnp.zeros_like(s_ref)
    else:
        for sub in range(n_sub):
            for h in range(rep):
                s_ref[2 * rep * sub + h] = s0f_ref[sub, h]
                s_ref[2 * rep * sub + rep + h] = s0b_ref[sub, h]

    def solve(i, carry):
        chunks = [i + part * (n_all // ways) for part in range(ways)]
        kcs, vcs, kkms, qkms = [], [], [], []
        for c in chunks:
            r0 = pl.multiple_of(c * CHUNK, CHUNK)
            kc = k_ref[pl.ds(r0, CHUNK), :]
            kb = kc.astype(BF16)
            kk2 = jnp.concatenate([kb, kb], axis=0)
            kcs.append(kc)
            vcs.append(v_ref[pl.ds(r0, CHUNK), :])
            kkms.append(lax.dot_general(kb, kk2, nt, preferred_element_type=F32))
            qkms.append(lax.dot_general(q_ref[pl.ds(r0, CHUNK), :].astype(BF16), kk2, nt,
                                        preferred_element_type=F32))
        chains = [(j, d) for j in range(ways) for d in (0, 1)]
        lms, rhs2s = [], []
        for j, d in chains:
            c = chunks[j]
            _, rows, gcol, bcol, _ = chunk_terms(c, d)
            gcol2 = jnp.where(left, gcol[0], gcol[1])
            bcol2 = jnp.where(left, bcol[0], bcol[1])
            grow2 = jnp.where(left1, rows[2 * d:2 * d + 1], rows[2 * d + 1:2 * d + 2])
            dm = jnp.exp(jnp.where(incl[d], gcol2 - grow2, -jnp.inf))
            a_ref[d, c] = (qkms[j] * dm).astype(BF16)
            lms.append(jnp.where(strict[d], kkms[j] * dm * bcol2, 0.0))
            rhs = [jnp.concatenate([vcs[j][:, h * DV:(h + 1) * DV] * bcol[h],
                                    kcs[j] * (bcol[h] * jnp.exp(gcol[h]))], axis=1) for h in (0, 1)]
            rhs2s.append(jnp.concatenate(rhs, axis=0).astype(BF16))
        t2s = inv_unit_tri(lms)
        sols = [[_mmul(th.astype(BF16), rhs2) for th in split(t2)]
                for t2, rhs2 in zip(t2s, rhs2s)]
        for (j, d), sol in zip(chains, sols):
            wu_ref[d, chunks[j]] = jnp.concatenate(sol, axis=1).astype(BF16)
        return carry

    lax.fori_loop(0, n_all // ways, solve, 0)

    def step(i, carry):
        lanes = [(sub, d) for sub in range(n_sub) for d in (0, 1)]
        where = [sub * n_chunks + (i if d == 0 else n_chunks - 1 - i) for sub, d in lanes]
        terms = [chunk_terms(c, d) for c, (_, d) in zip(where, lanes)]
        aws, ngs, qcs = [], [], []
        for c, (_, d), (r0, _, gcol, _, gl) in zip(where, lanes, terms):
            qcs.append(q_ref[pl.ds(r0, CHUNK), :])
            kb = k_ref[pl.ds(r0, CHUNK), :].astype(BF16)
            wu = wu_ref[d, c]
            wu2 = jnp.concatenate([wu[:, :2 * DV], wu[:, 2 * DV:]], axis=0)
            al, ar = split(a_ref[d, c].astype(F32))
            aws.append([_mmul(al.astype(BF16), wu2), _mmul(ar.astype(BF16), wu2)])
            ek = [jnp.exp(gl[h][:, 0:1] - gcol[h]) for h in (0, 1)]
            wuf = wu.astype(F32)
            zz = jnp.concatenate([wuf[:, :2 * DV] * ek[0], wuf[:, 2 * DV:] * ek[1]], axis=1).astype(BF16)
            ngs.append(lax.dot_general(kb, zz, tn, preferred_element_type=F32))
        streams = [(n, h) for n in range(len(lanes)) for h in (0, 1)]
        outs = []
        for n, h in streams:
            p = qcs[n] * jnp.exp(terms[n][2][h]) - aws[n][h][:, DV:]
            pg = jnp.concatenate([p, ngs[n][:, (2 * h + 1) * DV:(2 * h + 2) * DV]], axis=0).astype(BF16)
            outs.append(_mmul(pg, s_ref[2 * n + h].astype(BF16)))
        for (n, h), out in zip(streams, outs):
            s = 2 * n + h
            r0, gl = terms[n][0], terms[n][4]
            acc_ref[pl.ds(r0, CHUNK), h * DV:(h + 1) * DV] += out[:CHUNK] + aws[n][h][:, :DV]
            s_ref[s] = s_ref[s] * jnp.exp(gl[h]) - out[CHUNK:] + ngs[n][:, 2 * h * DV:(2 * h + 1) * DV]
        return carry

    lax.fori_loop(0, n_chunks, step, 0)

    nw = nw_ref[...]

    def finish(j, carry):
        r0 = pl.multiple_of(j * CHUNK, CHUNK)
        o = acc_ref[pl.ds(r0, CHUNK), :]
        z = z_ref[pl.ds(r0, CHUNK), :]
        for h in (0, 1):
            oh = o[:, h * DV:(h + 1) * DV]
            zh = z[:, h * DV:(h + 1) * DV]
            y = oh * lax.rsqrt(jnp.mean(oh * oh, axis=-1, keepdims=True) + EPS) * nw * (zh * jax.nn.sigmoid(zh))
            o_ref[pl.ds(r0, CHUNK), h * DV:(h + 1) * DV] = y.astype(o_ref.dtype)
        return carry

    lax.fori_loop(0, n_all, finish, 0)
    for sub in range(n_sub):
        for h in range(rep):
            sf_ref[sub, h] = s_ref[2 * rep * sub + h]
            sb_ref[sub, h] = s_ref[2 * rep * sub + rep + h]


def _deltanet(qkvn, proj, cols, rows, norm_w, s0f, s0b, o_prev, *, n_seq, t_seq, n_sub, row_blk0, zero_init):
    n_chunks = t_seq // CHUNK
    t_blk = n_sub * t_seq
    c_blk = n_sub * n_chunks
    nq = QK_W // DK
    z_blk0 = QKV_W // (2 * DV)
    rep = NV_HEADS // NK_HEADS
    state_spec = pl.BlockSpec((n_sub, rep, DK, DV), lambda b, kh: (b, kh, 0, 0))
    in_specs = [
        pl.BlockSpec((t_blk, DK), lambda b, kh: (row_blk0 + b, kh)),
        pl.BlockSpec((t_blk, DK), lambda b, kh: (row_blk0 + b, nq + kh)),
        pl.BlockSpec((t_blk, rep * DV), lambda b, kh: (row_blk0 + b, nq + kh)),
        pl.BlockSpec((t_blk, rep * DV), lambda b, kh: (row_blk0 + b, z_blk0 + kh)),
        pl.BlockSpec((None, t_blk, 8), lambda b, kh: (kh, row_blk0 + b, 0)),
        pl.BlockSpec((None, c_blk, 8, 2 * CHUNK), lambda b, kh: (kh, row_blk0 + b, 0, 0)),
        pl.BlockSpec((1, DV), lambda b, kh: (0, 0)),
        pl.BlockSpec((n_sub, rep, DK, DV), lambda b, kh: (0 if zero_init else b, kh, 0, 0)),
        pl.BlockSpec((n_sub, rep, DK, DV), lambda b, kh: (0 if zero_init else b, kh, 0, 0)),
        pl.BlockSpec(memory_space=pl.ANY),
    ]
    out_specs = [pl.BlockSpec((t_blk, rep * DV), lambda b, kh: (row_blk0 + b, kh)), state_spec, state_spec]
    out_shape = [
        jax.ShapeDtypeStruct((qkvn.shape[0], V_W), BF16),
        jax.ShapeDtypeStruct((n_seq, NV_HEADS, DK, DV), F32),
        jax.ShapeDtypeStruct((n_seq, NV_HEADS, DK, DV), F32),
    ]
    args = [qkvn, qkvn, qkvn, proj, cols, rows, norm_w.reshape(1, DV), s0f, s0b, o_prev]
    return pl.pallas_call(
        functools.partial(_delta_body, n_sub=n_sub, n_chunks=n_chunks, zero_init=zero_init),
        grid=(n_seq // n_sub, NK_HEADS),
        input_output_aliases={len(args) - 1: 0},
        name="deltanet",
        in_specs=in_specs,
        out_specs=out_specs,
        out_shape=out_shape,
        scratch_shapes=[pltpu.VMEM((t_blk, rep * DV), F32), pltpu.VMEM((2 * rep * n_sub, DK, DV), F32),
                        pltpu.VMEM((2, c_blk, CHUNK, 2 * rep * DV), BF16),
                        pltpu.VMEM((2, c_blk, CHUNK, 2 * CHUNK), BF16)],
        compiler_params=pltpu.CompilerParams(dimension_semantics=("parallel", "parallel"),
                                             vmem_limit_bytes=VMEM_LIMIT_BYTES),
    )(*args)


def _decay_tables(ab, a_log, dt_bias):
    n = ab.shape[0]
    nh2 = 2 * NV_HEADS
    rep = NV_HEADS // NK_HEADS
    beta = jax.nn.sigmoid(ab[:, :nh2])
    g = -jnp.exp(a_log.reshape(nh2)) * jax.nn.softplus(ab[:, nh2:] + dt_bias.reshape(nh2))
    gch = g.reshape(n // CHUNK, CHUNK, 2, NV_HEADS)
    gcf = jnp.cumsum(gch[:, :, 0], axis=1)
    gcb = jnp.flip(jnp.cumsum(jnp.flip(gch[:, :, 1], 1), axis=1), 1)
    gc = jnp.stack([gcf, gcb], axis=2)
    gl = jnp.stack([gcf[:, -1], gcb[:, 0]], axis=1)

    def per_kh(a):
        a = a.reshape(a.shape[:-2] + (2, NK_HEADS, rep))
        return jnp.moveaxis(a, -3, -2).reshape(a.shape[:-3] + (NK_HEADS, 2 * rep))
    gc_k = per_kh(gc)
    cols = jnp.concatenate([gc_k.reshape(n, NK_HEADS, 2 * rep), per_kh(beta.reshape(n, 2, NV_HEADS))], axis=-1)
    cols = jnp.transpose(cols, (1, 0, 2))
    gr = jnp.transpose(gc_k, (2, 0, 3, 1))
    gr = jnp.concatenate([gr, gr], axis=-1)
    glr = jnp.broadcast_to(jnp.transpose(per_kh(gl), (1, 0, 2))[..., None], gr.shape)
    rows = jnp.concatenate([gr, glr], axis=2)
    return cols, rows


TOK_TILE = 512
MIX_TILE = 256
COMBINE_TILE = 128


def _group_of_tile(i, tile, n_prompt, t_sample):
    first = n_prompt // tile
    return jnp.where(i < first, 0, 1 + (i - first) // (t_sample // tile))


def _token_specs(tile, n_prompt, t_sample, d):
    first = n_prompt // tile
    per_seq = t_sample // tile
    return [
        pl.BlockSpec((tile, d), lambda i, *_: (jnp.minimum(i, first - 1), 0)),
        pl.BlockSpec((tile, d), lambda i, *_: (jnp.maximum(i - first, 0), 0)),
        pl.BlockSpec((tile, d), lambda i, *_: (jnp.maximum(i - first, 0) % per_seq, 0)),
        pl.BlockSpec((None, 1, 6 * d), lambda i, *_: (_group_of_tile(i, tile, n_prompt, t_sample), 0, 0)),
    ]


def _token_rows(i, first, xp_ref, xs_ref, pos_ref):
    return jnp.where(i < first, xp_ref[...], xs_ref[...] + pos_ref[...])


def _modulate_body(xp_ref, xs_ref, pos_ref, mod_ref, h_ref, *, first):
    d = h_ref.shape[-1]
    x = _token_rows(pl.program_id(0), first, xp_ref, xs_ref, pos_ref)
    h_ref[...] = (x * (1.0 + mod_ref[:, d:2 * d]) + mod_ref[:, 0:d]).astype(h_ref.dtype)


def _modulate(xp, xs, pos, mod, *, t_sample):
    n_p, d = xp.shape
    n = n_p + xs.shape[0]
    return pl.pallas_call(
        functools.partial(_modulate_body, first=n_p // TOK_TILE),
        grid=(n // TOK_TILE,),
        in_specs=_token_specs(TOK_TILE, n_p, t_sample, d),
        out_specs=pl.BlockSpec((TOK_TILE, d), lambda i: (i, 0)),
        out_shape=jax.ShapeDtypeStruct((n, d), BF16),
        compiler_params=pltpu.CompilerParams(dimension_semantics=("parallel",), vmem_limit_bytes=VMEM_LIMIT_BYTES),
        name="modulate",
    )(xp, xs, pos, mod)


def _gconv_body(ub_ref, uc_ref, ux_ref, w_ref, o_ref, *, t_first, t_rest):
    t_seq = jnp.where(pl.program_id(0) == 0, t_first, t_rest)
    x = uc_ref[...] * ux_ref[...]
    n = x.shape[0]
    tpos = lax.broadcasted_iota(jnp.int32, x.shape, 0) & (t_seq - 1)
    xm = jnp.where(tpos == 0, 0.0, pltpu.roll(x, 1, axis=0))
    xp = jnp.where(tpos == t_seq - 1, 0.0, pltpu.roll(x, n - 1, axis=0))
    w = w_ref[...]
    o_ref[...] = (ub_ref[...] * (xm * w[0:1] + x * w[1:2] + xp * w[2:3])).astype(o_ref.dtype)


def _gated_conv(proj, conv_w, col0, *, t_first, t_rest):
    n = proj.shape[0]
    nblk = CONV_CH // DK
    b0 = col0 // DK
    return pl.pallas_call(
        functools.partial(_gconv_body, t_first=t_first, t_rest=t_rest),
        grid=(n // PREP_ROWS, nblk),
        in_specs=[pl.BlockSpec((PREP_ROWS, DK), lambda i, j: (i, b0 + j)),
                  pl.BlockSpec((PREP_ROWS, DK), lambda i, j: (i, b0 + nblk + j)),
                  pl.BlockSpec((PREP_ROWS, DK), lambda i, j: (i, b0 + 2 * nblk + j)),
                  pl.BlockSpec((3, DK), lambda i, j: (0, j))],
        out_specs=pl.BlockSpec((PREP_ROWS, DK), lambda i, j: (i, j)),
        out_shape=jax.ShapeDtypeStruct((n, CONV_CH), BF16),
        compiler_params=pltpu.CompilerParams(dimension_semantics=("parallel", "parallel"),
                                             vmem_limit_bytes=VMEM_LIMIT_BYTES),
        name="gated_conv",
    )(proj, proj, proj, conv_w)


BR_TM = 512
BR_TN = 512


def _branch_body(o_ref, g_ref, wa_ref, wb_ref, ra_ref, rb_ref, m_ref, wa_bf, wb_bf):
    @pl.when(pl.program_id(1) == 0)
    def _():
        wa_bf[...] = wa_ref[...].astype(BF16)
        wb_bf[...] = wb_ref[...].astype(BF16)
    ya = jnp.dot(o_ref[...], wa_bf[...], preferred_element_type=F32)
    yb = jnp.dot(g_ref[...], wb_bf[...], preferred_element_type=F32)
    m_ref[...] = (jax.nn.sigmoid(ra_ref[...]) * ya + jax.nn.sigmoid(rb_ref[...]) * yb).astype(m_ref.dtype)


def _branch_merge(o, g, w_a, w_b, proj, ra_col, rb_col):
    n = o.shape[0]
    d = w_a.shape[1]
    el = pl.Element
    return pl.pallas_call(
        _branch_body,
        grid=(d // BR_TN, n // BR_TM),
        in_specs=[pl.BlockSpec((BR_TM, o.shape[1]), lambda j, i: (i, 0)),
                  pl.BlockSpec((BR_TM, g.shape[1]), lambda j, i: (i, 0)),
                  pl.BlockSpec((w_a.shape[0], BR_TN), lambda j, i: (0, j)),
                  pl.BlockSpec((w_b.shape[0], BR_TN), lambda j, i: (0, j)),
                  pl.BlockSpec((el(BR_TM), el(BR_TN)),
                               lambda j, i: (i * BR_TM, pl.multiple_of(ra_col + j * BR_TN, DK))),
                  pl.BlockSpec((el(BR_TM), el(BR_TN)),
                               lambda j, i: (i * BR_TM, pl.multiple_of(rb_col + j * BR_TN, DK)))],
        out_specs=pl.BlockSpec((BR_TM, BR_TN), lambda j, i: (i, j)),
        out_shape=jax.ShapeDtypeStruct((n, d), BF16),
        scratch_shapes=[pltpu.VMEM((w_a.shape[0], BR_TN), BF16), pltpu.VMEM((w_b.shape[0], BR_TN), BF16)],
        compiler_params=pltpu.CompilerParams(dimension_semantics=("arbitrary", "arbitrary"),
                                             vmem_limit_bytes=VMEM_LIMIT_BYTES),
        name="branch_merge",
    )(o, g, w_a, w_b, proj, proj)


def _ln_rows(v, g, b):
    mu = jnp.mean(v, axis=-1, keepdims=True)
    c = v - mu
    return c * lax.rsqrt(jnp.mean(c * c, axis=-1, keepdims=True) + EPS) * g + b


def _mix_body(m_ref, wo_ref, xp_ref, xs_ref, pos_ref, mod_ref, g_ref, b_ref, rw_ref, rb_ref,
              x1_ref, h2_ref, lg_ref, *, first):
    d = x1_ref.shape[-1]
    x = _token_rows(pl.program_id(0), first, xp_ref, xs_ref, pos_ref)
    mixed = jnp.dot(m_ref[...], wo_ref[...], preferred_element_type=F32)
    x1 = _ln_rows(DEEPNORM_ALPHA * x + mod_ref[:, 2 * d:3 * d] * mixed, g_ref[...], b_ref[...])
    h2 = x1 * (1.0 + mod_ref[:, 4 * d:5 * d]) + mod_ref[:, 3 * d:4 * d]
    x1_ref[...] = x1
    h2_ref[...] = h2.astype(h2_ref.dtype)
    lg_ref[...] = jnp.dot(h2, rw_ref[...], precision=lax.Precision.HIGHEST, preferred_element_type=F32) + rb_ref[...]


def _mix_norm_route(m, w_o_bf, xp, xs, pos, mod, ln_g, ln_b, router_w, router_b, *, t_sample):
    n, d = m.shape
    n_p = xp.shape[0]
    ne = router_w.shape[1]
    const = lambda i: (0, 0)
    return pl.pallas_call(
        functools.partial(_mix_body, first=n_p // MIX_TILE),
        grid=(n // MIX_TILE,),
        in_specs=[pl.BlockSpec((MIX_TILE, d), lambda i: (i, 0)), pl.BlockSpec((d, d), const)]
        + _token_specs(MIX_TILE, n_p, t_sample, d)
        + [pl.BlockSpec((1, d), const), pl.BlockSpec((1, d), const),
           pl.BlockSpec((d, ne), const), pl.BlockSpec((1, ne), const)],
        out_specs=[pl.BlockSpec((MIX_TILE, d), lambda i: (i, 0)), pl.BlockSpec((MIX_TILE, d), lambda i: (i, 0)),
                   pl.BlockSpec((MIX_TILE, ne), lambda i: (i, 0))],
        out_shape=[jax.ShapeDtypeStruct((n, d), F32), jax.ShapeDtypeStruct((n, d), F32),
                   jax.ShapeDtypeStruct((n, ne), F32)],
        compiler_params=pltpu.CompilerParams(dimension_semantics=("parallel",), vmem_limit_bytes=VMEM_LIMIT_BYTES),
        name="mix_norm_route",
    )(m, w_o_bf, xp, xs, pos, mod, ln_g.reshape(1, d), ln_b.reshape(1, d), router_w, router_b.reshape(1, ne))


DISPATCH_ROWS = 512


def _dispatch_body(tok_ref, h_hbm, xs_ref, sem):
    base = pl.program_id(0) * DISPATCH_ROWS

    def body(r, carry):
        pltpu.make_async_copy(h_hbm.at[pl.ds(tok_ref[base + r], 1), :], xs_ref.at[pl.ds(r, 1), :], sem).start()
        return carry
    lax.fori_loop(0, DISPATCH_ROWS, body, 0)
    pltpu.make_async_copy(h_hbm.at[pl.ds(0, DISPATCH_ROWS), :], xs_ref, sem).wait()


def _dispatch(slot_tok, h):
    n_slots = slot_tok.shape[0]
    d = h.shape[1]
    grid_spec = pltpu.PrefetchScalarGridSpec(
        num_scalar_prefetch=1,
        grid=(n_slots // DISPATCH_ROWS,),
        in_specs=[pl.BlockSpec(memory_space=pl.ANY)],
        out_specs=pl.BlockSpec((DISPATCH_ROWS, d), lambda i, t: (i, 0)),
        scratch_shapes=[pltpu.SemaphoreType.DMA(())],
    )
    return pl.pallas_call(
        _dispatch_body,
        grid_spec=grid_spec,
        out_shape=jax.ShapeDtypeStruct((n_slots, d), h.dtype),
        compiler_params=pltpu.CompilerParams(dimension_semantics=("arbitrary",), vmem_limit_bytes=VMEM_LIMIT_BYTES),
        name="dispatch",
    )(slot_tok, h)


def _combine_body(slot_ref, y_hbm, p_ref, x1_ref, mod_ref, g_ref, b_ref, o_ref, buf, sem, *, n_tiles):
    tt = COMBINE_TILE
    d = o_ref.shape[-1]
    i = pl.program_id(0)

    def row_copy(src_row, dst_row, which):
        return pltpu.make_async_copy(y_hbm.at[pl.ds(src_row, 1), :], buf.at[which, pl.ds(dst_row, 1), :],
                                     sem.at[which])

    def issue(tile, which):
        base = tile * (tt * TOP_K)

        def body(t, carry):
            for k in range(TOP_K):
                row_copy(slot_ref[base + t * TOP_K + k], k * tt + t, which).start()
            return carry
        lax.fori_loop(0, tt, body, 0)

    @pl.when(i == 0)
    def _():
        issue(0, 0)

    @pl.when(i + 1 < n_tiles)
    def _():
        issue(i + 1, (i + 1) % 2)

    cur = i % 2
    pltpu.make_async_copy(y_hbm.at[pl.ds(0, TOP_K * tt), :], buf.at[cur], sem.at[cur]).wait()
    p = p_ref[...]
    ff = p[:, 0:1] * buf[cur, 0:tt, :]
    for k in range(1, TOP_K):
        ff = ff + p[:, k:k + 1] * buf[cur, k * tt:(k + 1) * tt, :]
    o_ref[...] = _ln_rows(DEEPNORM_ALPHA * x1_ref[...] + mod_ref[:, 5 * d:6 * d] * ff, g_ref[...], b_ref[...])


def _combine_norm(slot, y, top_p, x1, mod, ln_g, ln_b, *, n_prompt, t_sample):
    n, d = x1.shape
    tt = COMBINE_TILE
    n_tiles = n // tt
    grid_spec = pltpu.PrefetchScalarGridSpec(
        num_scalar_prefetch=1,
        grid=(n_tiles,),
        in_specs=[pl.BlockSpec(memory_space=pl.ANY),
                  pl.BlockSpec((tt, TOP_K), lambda i, s: (i, 0)),
                  pl.BlockSpec((tt, d), lambda i, s: (i, 0)),
                  pl.BlockSpec((None, 1, 6 * d), lambda i, s: (_group_of_tile(i, tt, n_prompt, t_sample), 0, 0)),
                  pl.BlockSpec((1, d), lambda i, s: (0, 0)),
                  pl.BlockSpec((1, d), lambda i, s: (0, 0))],
        out_specs=pl.BlockSpec((tt, d), lambda i, s: (i, 0)),
        scratch_shapes=[pltpu.VMEM((2, TOP_K * tt, d), F32), pltpu.SemaphoreType.DMA((2,))],
    )
    return pl.pallas_call(
        functools.partial(_combine_body, n_tiles=n_tiles),
        grid_spec=grid_spec,
        out_shape=jax.ShapeDtypeStruct((n, d), F32),
        compiler_params=pltpu.CompilerParams(dimension_semantics=("arbitrary",), vmem_limit_bytes=VMEM_LIMIT_BYTES),
        name="combine_norm",
    )(slot, y, top_p, x1, mod, ln_g.reshape(1, d), ln_b.reshape(1, d))


def _grid_pos_embed(rows, dim):
    quarter = dim // 4
    freqs = 1.0 / (POS_BASE ** (jnp.arange(quarter, dtype=F32) / quarter))

    def axis_embed(n):
        ang = jnp.arange(n, dtype=F32)[:, None] * freqs[None, :]
        return jnp.concatenate([jnp.sin(ang), jnp.cos(ang)], axis=-1)
    er = axis_embed(rows)
    ec = axis_embed(GRID_W)
    pos = jnp.concatenate([jnp.broadcast_to(er[:, None, :], (rows, GRID_W, dim // 2)),
                           jnp.broadcast_to(ec[None, :, :], (rows, GRID_W, dim // 2))], axis=-1)
    return pos.reshape(rows * GRID_W, dim)


def kernel(x_prompt, x_sample, c, state_fwd, state_bwd, c_ctx, w_mod, b_mod, w_in, conv_qkv, a_log, dt_bias,
           norm_o, w_a_out, conv_b, w_b_out, w_o, ln1_g, ln1_b, router_w, router_b, w_gu, b_gu, w_down, b_down,
           ln2_g, ln2_b):
    nb_p, t_p, d = x_prompt.shape
    nb_s, t_s, _ = x_sample.shape
    n_p = nb_p * t_p
    n_s = nb_s * t_s
    assert n_p == PREP_ROWS and t_s == PREP_ROWS
    xp = x_prompt.reshape(n_p, d)
    xs = x_sample.reshape(n_s, d)
    pos = _grid_pos_embed(t_s // GRID_W, d)
    offs = tuple(int(o) for o in np.cumsum(IN_SIZES)[:-1])

    l = 0
    cvec = jnp.concatenate([c_ctx[None, :], c], axis=0)
    cpad = jnp.zeros((8, d), F32).at[:1 + nb_s].set(jax.nn.silu(cvec))
    mod = (_mm(cpad, w_mod[l], tm=8, tn=512)[:1 + nb_s] + b_mod[l])[:, None, :]

    h = _modulate(xp, xs, pos, mod, t_sample=t_s)
    proj = _mm(h, w_in[l], tm=2048, tn=384)

    qkvn = _qkv_prep(proj, conv_qkv[l], t_first=t_p, t_rest=t_s)
    cols, rws = _decay_tables(proj[:, offs[1]:offs[3]], a_log[l], dt_bias[l])
    zero_state = jnp.zeros((PROMPT_SUB, NV_HEADS, DK, DV), F32)
    o_all = jnp.zeros((n_p + n_s, V_W), BF16)
    o_all, sf, sb = _deltanet(qkvn, proj, cols, rws, norm_o[l], zero_state, zero_state, o_all,
                              n_seq=nb_p, t_seq=t_p, n_sub=PROMPT_SUB, row_blk0=0, zero_init=True)
    o_all, _, _ = _deltanet(qkvn, proj, cols, rws, norm_o[l], state_fwd[:, l], state_bwd[:, l], o_all,
                            n_seq=nb_s, t_seq=t_s, n_sub=1, row_blk0=n_p // t_s, zero_init=False)

    gconv = _gated_conv(proj, conv_b[l], offs[3], t_first=t_p, t_rest=t_s)
    m = _branch_merge(o_all, gconv, w_a_out[l], w_b_out[l], proj, offs[6], offs[7])
    x1, h2, logits = _mix_norm_route(m, w_o[l].astype(BF16), xp, xs, pos, mod, ln1_g[l], ln1_b[l],
                                     router_w[l], router_b[l], t_sample=t_s)
    slot, slot_tok, top_p, block_e, n_valid = _route(logits)
    y = _moe_blocks(_dispatch(slot_tok, h2), block_e, n_valid, w_gu[l], b_gu[l], w_down[l], b_down[l])
    x2 = _combine_norm(slot, y, top_p, x1, mod, ln2_g[l], ln2_b[l], n_prompt=n_p, t_sample=t_s)
    y_prompt = x2[:n_p].reshape(nb_p, t_p, d)
    y_sample = x2[n_p:].reshape(nb_s, t_s, d)
    return (y_prompt, y_sample, sf[:, None], sb[:, None])
```

```python
import functools

import jax
import jax.numpy as jnp
import numpy as np
from jax import lax
from jax.experimental import pallas as pl
from jax.experimental.pallas import tpu as pltpu

F32 = jnp.float32
BF16 = jnp.bfloat16

D_MODEL = 2048
GRID_W = 64
NK_HEADS = 16
NV_HEADS = 32
DK = 128
DV = 128
QK_W = NK_HEADS * DK
V_W = NV_HEADS * DV
QKV_W = 2 * QK_W + V_W
CHUNK = 64
CONV_CH = D_MODEL
IN_SIZES = (QKV_W, V_W, 2 * NV_HEADS, 2 * NV_HEADS, CONV_CH, CONV_CH, CONV_CH, D_MODEL, D_MODEL)
N_EXPERTS = 32
TOP_K = 4
D_FF = D_MODEL
SWIGLU_LIMIT = 7.0
SWIGLU_ALPHA = 1.702
DEPTH = 1
DEEPNORM_ALPHA = (2 * DEPTH) ** 0.25
EPS = 1e-6
POS_BASE = 10000.0

VMEM_LIMIT_BYTES = 56 * 1024 * 1024


def _mm_body(x_ref, w_ref, o_ref):
    o_ref[...] = jnp.dot(x_ref[...].astype(BF16), w_ref[...].astype(BF16), preferred_element_type=F32)


def _mm(x, w, *, tm, tn):
    m, k = x.shape
    n = w.shape[1]
    assert m % tm == 0 and n % tn == 0
    return pl.pallas_call(
        _mm_body,
        name="matmul",
        grid=(m // tm, n // tn),
        in_specs=[pl.BlockSpec((tm, k), lambda i, j: (i, 0)),
                  pl.BlockSpec((k, tn), lambda i, j: (0, j))],
        out_specs=pl.BlockSpec((tm, tn), lambda i, j: (i, j)),
        out_shape=jax.ShapeDtypeStruct((m, n), F32),
        compiler_params=pltpu.CompilerParams(dimension_semantics=("parallel", "parallel"),
                                             vmem_limit_bytes=VMEM_LIMIT_BYTES),
    )(x, w)


MOE_TM = 512
MOE_TF = 512


def _moe_body(be_ref, nv_ref, x_ref, wg_ref, wu_ref, bg_ref, bu_ref, wd_ref, bd_ref, o_ref, acc_ref):
    i = pl.program_id(0)
    f = pl.program_id(1)
    valid = i < nv_ref[0]

    @pl.when(f == 0)
    def _():
        acc_ref[...] = jnp.zeros_like(acc_ref)

    @pl.when(valid)
    def _():
        x = x_ref[...].astype(BF16)
        g = jnp.dot(x, wg_ref[...].astype(BF16), preferred_element_type=F32) + bg_ref[...]
        u = jnp.dot(x, wu_ref[...].astype(BF16), preferred_element_type=F32) + bu_ref[...]
        gate = jnp.minimum(g, SWIGLU_LIMIT)
        up = jnp.clip(u, -SWIGLU_LIMIT, SWIGLU_LIMIT)
        act = (up + 1.0) * gate * jax.nn.sigmoid(SWIGLU_ALPHA * gate)
        acc_ref[...] += jnp.dot(act.astype(BF16), wd_ref[...].astype(BF16), preferred_element_type=F32)

    @pl.when(f == pl.num_programs(1) - 1)
    def _():
        o_ref[...] = jnp.where(valid, acc_ref[...] + bd_ref[...], 0.0)


def _moe_blocks(xs, block_e, n_valid, w_gu, b_gu, w_down, b_down):
    n_slots, d = xs.shape
    n_blocks = n_slots // MOE_TM
    nf = D_FF // MOE_TF

    def f_eff(i, f, nv):
        return jnp.where(i < nv[0], f, nf - 1)

    grid_spec = pltpu.PrefetchScalarGridSpec(
        num_scalar_prefetch=2,
        grid=(n_blocks, nf),
        in_specs=[
            pl.BlockSpec((MOE_TM, d), lambda i, f, be, nv: (i, 0)),
            pl.BlockSpec((None, d, MOE_TF), lambda i, f, be, nv: (be[i], 0, f_eff(i, f, nv))),
            pl.BlockSpec((None, d, MOE_TF), lambda i, f, be, nv: (be[i], 0, nf + f_eff(i, f, nv))),
            pl.BlockSpec((None, 1, MOE_TF), lambda i, f, be, nv: (be[i], 0, f_eff(i, f, nv))),
            pl.BlockSpec((None, 1, MOE_TF), lambda i, f, be, nv: (be[i], 0, nf + f_eff(i, f, nv))),
            pl.BlockSpec((None, MOE_TF, d), lambda i, f, be, nv: (be[i], f_eff(i, f, nv), 0)),
            pl.BlockSpec((None, 1, d), lambda i, f, be, nv: (be[i], 0, 0)),
        ],
        out_specs=pl.BlockSpec((MOE_TM, d), lambda i, f, be, nv: (i, 0)),
        scratch_shapes=[pltpu.VMEM((MOE_TM, d), F32)],
    )
    return pl.pallas_call(
        _moe_body,
        name="moe_experts",
        grid_spec=grid_spec,
        out_shape=jax.ShapeDtypeStruct((n_slots, d), F32),
        compiler_params=pltpu.CompilerParams(dimension_semantics=("arbitrary", "arbitrary"),
                                             vmem_limit_bytes=VMEM_LIMIT_BYTES),
    )(block_e, n_valid, xs, w_gu, w_gu, b_gu[:, None, :], b_gu[:, None, :], w_down, b_down[:, None, :])


def _route(logits):
    n_tok = logits.shape[0]
    n_assign = n_tok * TOP_K
    top_logit, top_e = lax.top_k(logits, TOP_K)
    top_p = jax.nn.softmax(top_logit, axis=-1)
    flat_e = top_e.reshape(-1)
    onehot = (flat_e[:, None] == jnp.arange(N_EXPERTS)[None, :]).astype(jnp.int32)
    csum = jnp.cumsum(onehot, axis=0)
    counts = csum[-1]
    rank = jnp.take_along_axis(csum, flat_e[:, None], axis=1)[:, 0] - 1
    padded = (counts + MOE_TM - 1) // MOE_TM * MOE_TM
    pad_end = jnp.cumsum(padded)
    pad_start = pad_end - padded
    slot = (pad_start[flat_e] + rank).astype(jnp.int32)
    n_blocks = -(-(n_assign + N_EXPERTS * (MOE_TM - 1)) // MOE_TM)
    n_slots = n_blocks * MOE_TM
    slot_tok = jnp.zeros((n_slots,), jnp.int32).at[slot].set(jnp.arange(n_assign, dtype=jnp.int32) // TOP_K)
    n_valid = (pad_end[-1] // MOE_TM).astype(jnp.int32)
    blk = jnp.arange(n_blocks, dtype=jnp.int32)
    block_e = jnp.searchsorted(pad_end, jnp.minimum(blk, n_valid - 1) * MOE_TM, side='right').astype(jnp.int32)
    block_e = jnp.minimum(block_e, N_EXPERTS - 1)
    return slot, slot_tok, top_p, block_e, n_valid.reshape(1)


PREP_ROWS = 4096


def _prep_body(x_ref, w_ref, o_ref, *, t_first, t_rest):
    i = pl.program_id(0)
    j = pl.program_id(1)
    t_seq = jnp.where(i == 0, t_first, t_rest)
    x = x_ref[...]
    n = x.shape[0]
    row = lax.broadcasted_iota(jnp.int32, x.shape, 0)
    tpos = row & (t_seq - 1)
    xm = jnp.where(tpos == 0, 0.0, pltpu.roll(x, 1, axis=0))
    xp = jnp.where(tpos == t_seq - 1, 0.0, pltpu.roll(x, n - 1, axis=0))
    w = w_ref[...]
    y = xm * w[0:1] + x * w[1:2] + xp * w[2:3]
    y = y * jax.nn.sigmoid(y)
    rs = lax.rsqrt(jnp.sum(y * y, axis=-1, keepdims=True) + EPS)
    nq = QK_W // DK
    fac = jnp.where(j < nq, rs * (DK ** -0.5), jnp.where(j < 2 * nq, rs, 1.0))
    o_ref[...] = (y * fac).astype(o_ref.dtype)


def _qkv_prep(proj, conv_w, *, t_first, t_rest):
    n = proj.shape[0]
    return pl.pallas_call(
        functools.partial(_prep_body, t_first=t_first, t_rest=t_rest),
        name="qkv_prep",
        grid=(n // PREP_ROWS, QKV_W // DK),
        in_specs=[pl.BlockSpec((PREP_ROWS, DK), lambda i, j: (i, j)),
                  pl.BlockSpec((3, DK), lambda i, j: (0, j))],
        out_specs=pl.BlockSpec((PREP_ROWS, DK), lambda i, j: (i, j)),
        out_shape=jax.ShapeDtypeStruct((n, QKV_W), BF16),
        compiler_params=pltpu.CompilerParams(dimension_semantics=("parallel", "parallel"),
                                             vmem_limit_bytes=VMEM_LIMIT_BYTES),
    )(proj, conv_w)


PROMPT_SUB = 2
SOLVE_WAYS = 8


def _mmul(a, b):
    return jnp.dot(a, b, preferred_element_type=F32)


def _delta_body(q_ref, k_ref, v_ref, z_ref, cols_ref, rows_ref, nw_ref, s0f_ref, s0b_ref, o_prev_ref,
                o_ref, sf_ref, sb_ref, acc_ref, s_ref, wu_ref, a_ref, *, n_sub, n_chunks, zero_init):
    del o_prev_ref
    n_all = n_sub * n_chunks
    ways = min(SOLVE_WAYS, n_all)
    lane = lax.broadcasted_iota(jnp.int32, (CHUNK, 2 * CHUNK), 1)
    row = lax.broadcasted_iota(jnp.int32, (CHUNK, 2 * CHUNK), 0)
    pos = lane & (CHUNK - 1)
    left = lane < CHUNK
    left1 = lax.broadcasted_iota(jnp.int32, (1, 2 * CHUNK), 1) < CHUNK
    eye2 = (pos == row).astype(F32)
    incl = (pos <= row, pos >= row)
    strict = (pos < row, pos > row)
    same8 = (row >> 3) == (pos >> 3)
    same16 = (row >> 4) == (pos >> 4)
    same32 = (row >> 5) == (pos >> 5)
    nt = (((1,), (1,)), ((), ()))
    tn = (((0,), (0,)), ((), ()))

    def split(m):
        return jnp.where(left, m, 0.0), jnp.where(left, 0.0, m)

    def bdiag(m):
        return jnp.concatenate(split(m), axis=0).astype(BF16)

    def pmm(a, b):
        return _mmul(a.astype(BF16), bdiag(b))

    def inv_unit_tri(lms):
        d8 = [jnp.where(same8, lm, 0.0) for lm in lms]
        xs = [eye2 - d for d in d8]
        d2 = [pmm(d, d) for d in d8]
        xs = [x + pmm(x, d) for x, d in zip(xs, d2)]
        d4 = [pmm(d, d) for d in d2]
        xs = [x + pmm(x, d) for x, d in zip(xs, d4)]
        for keep in (same16 & ~same8, same32 & ~same16, ~same32):
            ys = [pmm(jnp.where(keep, lm, 0.0), x) for lm, x in zip(lms, xs)]
            xs = [x - pmm(x, y) for x, y in zip(xs, ys)]
        return xs

    def chunk_terms(c, d):
        r0 = pl.multiple_of(c * CHUNK, CHUNK)
        cols = cols_ref[pl.ds(r0, CHUNK), :]
        rows = rows_ref[c]
        gcol = [cols[:, 2 * d + h:2 * d + h + 1] for h in (0, 1)]
        bcol = [cols[:, 4 + 2 * d + h:4 + 2 * d + h + 1] for h in (0, 1)]
        gl = [rows[4 + 2 * d + h:5 + 2 * d + h] for h in (0, 1)]
        return r0, rows, gcol, bcol, gl

    acc_ref[...] = jnp.zeros_like(acc_ref)
    rep = s0f_ref.shape[1]
    if zero_init:
        s_ref[...] = jnp.zeros_like(s_ref)
    else:
        for sub in range(n_sub):
            for h in range(rep):
                s_ref[2 * rep * sub + h] = s0f_ref[sub, h]
                s_ref[2 * rep * sub + rep + h] = s0b_ref[sub, h]

    def solve(i, carry):
        chunks = [i + part * (n_all // ways) for part in range(ways)]
        kcs, vcs, kkms, qkms = [], [], [], []
        for c in chunks:
            r0 = pl.multiple_of(c * CHUNK, CHUNK)
            kc = k_ref[pl.ds(r0, CHUNK), :]
            kb = kc.astype(BF16)
            kk2 = jnp.concatenate([kb, kb], axis=0)
            kcs.append(kc)
            vcs.append(v_ref[pl.ds(r0, CHUNK), :])
            kkms.append(lax.dot_general(kb, kk2, nt, preferred_element_type=F32))
            qkms.append(lax.dot_general(q_ref[pl.ds(r0, CHUNK), :].astype(BF16), kk2, nt,
                                        preferred_element_type=F32))
        chains = [(j, d) for j in range(ways) for d in (0, 1)]
        lms, rhs2s = [], []
        for j, d in chains:
            c = chunks[j]
            _, rows, gcol, bcol, _ = chunk_terms(c, d)
            gcol2 = jnp.where(left, gcol[0], gcol[1])
            bcol2 = jnp.where(left, bcol[0], bcol[1])
            grow2 = jnp.where(left1, rows[2 * d:2 * d + 1], rows[2 * d + 1:2 * d + 2])
            dm = jnp.exp(jnp.where(incl[d], gcol2 - grow2, -jnp.inf))
            a_ref[d, c] = (qkms[j] * dm).astype(BF16)
            lms.append(jnp.where(strict[d], kkms[j] * dm * bcol2, 0.0))
            rhs = [jnp.concatenate([vcs[j][:, h * DV:(h + 1) * DV] * bcol[h],
                                    kcs[j] * (bcol[h] * jnp.exp(gcol[h]))], axis=1) for h in (0, 1)]
            rhs2s.append(jnp.concatenate(rhs, axis=0).astype(BF16))
        t2s = inv_unit_tri(lms)
        sols = [[_mmul(th.astype(BF16), rhs2) for th in split(t2)]
                for t2, rhs2 in zip(t2s, rhs2s)]
        for (j, d), sol in zip(chains, sols):
            wu_ref[d, chunks[j]] = jnp.concatenate(sol, axis=1).astype(BF16)
        return carry

    lax.fori_loop(0, n_all // ways, solve, 0)

    def step(i, carry):
        lanes = [(sub, d) for sub in range(n_sub) for d in (0, 1)]
        where = [sub * n_chunks + (i if d == 0 else n_chunks - 1 - i) for sub, d in lanes]
        terms = [chunk_terms(c, d) for c, (_, d) in zip(where, lanes)]
        aws, ngs, qcs = [], [], []
        for c, (_, d), (r0, _, gcol, _, gl) in zip(where, lanes, terms):
            qcs.append(q_ref[pl.ds(r0, CHUNK), :])
            kb = k_ref[pl.ds(r0, CHUNK), :].astype(BF16)
            wu = wu_ref[d, c]
            wu2 = jnp.concatenate([wu[:, :2 * DV], wu[:, 2 * DV:]], axis=0)
            al, ar = split(a_ref[d, c].astype(F32))
            aws.append([_mmul(al.astype(BF16), wu2), _mmul(ar.astype(BF16), wu2)])
            ek = [jnp.exp(gl[h][:, 0:1] - gcol[h]) for h in (0, 1)]
            wuf = wu.astype(F32)
            zz = jnp.concatenate([wuf[:, :2 * DV] * ek[0], wuf[:, 2 * DV:] * ek[1]], axis=1).astype(BF16)
            ngs.append(lax.dot_general(kb, zz, tn, preferred_element_type=F32))
        streams = [(n, h) for n in range(len(lanes)) for h in (0, 1)]
        outs = []
        for n, h in streams:
            p = qcs[n] * jnp.exp(terms[n][2][h]) - aws[n][h][:, DV:]
            pg = jnp.concatenate([p, ngs[n][:, (2 * h + 1) * DV:(2 * h + 2) * DV]], axis=0).astype(BF16)
            outs.append(_mmul(pg, s_ref[2 * n + h].astype(BF16)))
        for (n, h), out in zip(streams, outs):
            s = 2 * n + h
            r0, gl = terms[n][0], terms[n][4]
            acc_ref[pl.ds(r0, CHUNK), h * DV:(h + 1) * DV] += out[:CHUNK] + aws[n][h][:, :DV]
            s_ref[s] = s_ref[s] * jnp.exp(gl[h]) - out[CHUNK:] + ngs[n][:, 2 * h * DV:(2 * h + 1) * DV]
        return carry

    lax.fori_loop(0, n_chunks, step, 0)

    nw = nw_ref[...]

    def finish(j, carry):
        r0 = pl.multiple_of(j * CHUNK, CHUNK)
        o = acc_ref[pl.ds(r0, CHUNK), :]
        z = z_ref[pl.ds(r0, CHUNK), :]
        for h in (0, 1):
            oh = o[:, h * DV:(h + 1) * DV]
            zh = z[:, h * DV:(h + 1) * DV]
            y = oh * lax.rsqrt(jnp.mean(oh * oh, axis=-1, keepdims=True) + EPS) * nw * (zh * jax.nn.sigmoid(zh))
            o_ref[pl.ds(r0, CHUNK), h * DV:(h + 1) * DV] = y.astype(o_ref.dtype)
        return carry

    lax.fori_loop(0, n_all, finish, 0)
    for sub in range(n_sub):
        for h in range(rep):
            sf_ref[sub, h] = s_ref[2 * rep * sub + h]
            sb_ref[sub, h] = s_ref[2 * rep * sub + rep + h]


def _deltanet(qkvn, proj, cols, rows, norm_w, s0f, s0b, o_prev, *, n_seq, t_seq, n_sub, row_blk0, zero_init):
    n_chunks = t_seq // CHUNK
    t_blk = n_sub * t_seq
    c_blk = n_sub * n_chunks
    nq = QK_W // DK
    z_blk0 = QKV_W // (2 * DV)
    rep = NV_HEADS // NK_HEADS
    state_spec = pl.BlockSpec((n_sub, rep, DK, DV), lambda b, kh: (b, kh, 0, 0))
    in_specs = [
        pl.BlockSpec((t_blk, DK), lambda b, kh: (row_blk0 + b, kh)),
        pl.BlockSpec((t_blk, DK), lambda b, kh: (row_blk0 + b, nq + kh)),
        pl.BlockSpec((t_blk, rep * DV), lambda b, kh: (row_blk0 + b, nq + kh)),
        pl.BlockSpec((t_blk, rep * DV), lambda b, kh: (row_blk0 + b, z_blk0 + kh)),
        pl.BlockSpec((None, t_blk, 8), lambda b, kh: (kh, row_blk0 + b, 0)),
        pl.BlockSpec((None, c_blk, 8, 2 * CHUNK), lambda b, kh: (kh, row_blk0 + b, 0, 0)),
        pl.BlockSpec((1, DV), lambda b, kh: (0, 0)),
        pl.BlockSpec((n_sub, rep, DK, DV), lambda b, kh: (0 if zero_init else b, kh, 0, 0)),
        pl.BlockSpec((n_sub, rep, DK, DV), lambda b, kh: (0 if zero_init else b, kh, 0, 0)),
        pl.BlockSpec(memory_space=pl.ANY),
    ]
    out_specs = [pl.BlockSpec((t_blk, rep * DV), lambda b, kh: (row_blk0 + b, kh)), state_spec, state_spec]
    out_shape = [
        jax.ShapeDtypeStruct((qkvn.shape[0], V_W), BF16),
        jax.ShapeDtypeStruct((n_seq, NV_HEADS, DK, DV), F32),
        jax.ShapeDtypeStruct((n_seq, NV_HEADS, DK, DV), F32),
    ]
    args = [qkvn, qkvn, qkvn, proj, cols, rows, norm_w.reshape(1, DV), s0f, s0b, o_prev]
    return pl.pallas_call(
        functools.partial(_delta_body, n_sub=n_sub, n_chunks=n_chunks, zero_init=zero_init),
        grid=(n_seq // n_sub, NK_HEADS),
        input_output_aliases={len(args) - 1: 0},
        name="deltanet",
        in_specs=in_specs,
        out_specs=out_specs,
        out_shape=out_shape,
        scratch_shapes=[pltpu.VMEM((t_blk, rep * DV), F32), pltpu.VMEM((2 * rep * n_sub, DK, DV), F32),
                        pltpu.VMEM((2, c_blk, CHUNK, 2 * rep * DV), BF16),
                        pltpu.VMEM((2, c_blk, CHUNK, 2 * CHUNK), BF16)],
        compiler_params=pltpu.CompilerParams(dimension_semantics=("parallel", "parallel"),
                                             vmem_limit_bytes=VMEM_LIMIT_BYTES),
    )(*args)


def _decay_tables(ab, a_log, dt_bias):
    n = ab.shape[0]
    nh2 = 2 * NV_HEADS
    rep = NV_HEADS // NK_HEADS
    beta = jax.nn.sigmoid(ab[:, :nh2])
    g = -jnp.exp(a_log.reshape(nh2)) * jax.nn.softplus(ab[:, nh2:] + dt_bias.reshape(nh2))
    gch = g.reshape(n // CHUNK, CHUNK, 2, NV_HEADS)
    gcf = jnp.cumsum(gch[:, :, 0], axis=1)
    gcb = jnp.flip(jnp.cumsum(jnp.flip(gch[:, :, 1], 1), axis=1), 1)
    gc = jnp.stack([gcf, gcb], axis=2)
    gl = jnp.stack([gcf[:, -1], gcb[:, 0]], axis=1)

    def per_kh(a):
        a = a.reshape(a.shape[:-2] + (2, NK_HEADS, rep))
        return jnp.moveaxis(a, -3, -2).reshape(a.shape[:-3] + (NK_HEADS, 2 * rep))
    gc_k = per_kh(gc)
    cols = jnp.concatenate([gc_k.reshape(n, NK_HEADS, 2 * rep), per_kh(beta.reshape(n, 2, NV_HEADS))], axis=-1)
    cols = jnp.transpose(cols, (1, 0, 2))
    gr = jnp.transpose(gc_k, (2, 0, 3, 1))
    gr = jnp.concatenate([gr, gr], axis=-1)
    glr = jnp.broadcast_to(jnp.transpose(per_kh(gl), (1, 0, 2))[..., None], gr.shape)
    rows = jnp.concatenate([gr, glr], axis=2)
    return cols, rows


TOK_TILE = 512
MIX_TILE = 256
COMBINE_TILE = 128


def _group_of_tile(i, tile, n_prompt, t_sample):
    first = n_prompt // tile
    return jnp.where(i < first, 0, 1 + (i - first) // (t_sample // tile))


def _token_specs(tile, n_prompt, t_sample, d):
    first = n_prompt // tile
    per_seq = t_sample // tile
    return [
        pl.BlockSpec((tile, d), lambda i, *_: (jnp.minimum(i, first - 1), 0)),
        pl.BlockSpec((tile, d), lambda i, *_: (jnp.maximum(i - first, 0), 0)),
        pl.BlockSpec((tile, d), lambda i, *_: (jnp.maximum(i - first, 0) % per_seq, 0)),
        pl.BlockSpec((None, 1, 6 * d), lambda i, *_: (_group_of_tile(i, tile, n_prompt, t_sample), 0, 0)),
    ]


def _token_rows(i, first, xp_ref, xs_ref, pos_ref):
    return jnp.where(i < first, xp_ref[...], xs_ref[...] + pos_ref[...])


def _modulate_body(xp_ref, xs_ref, pos_ref, mod_ref, h_ref, *, first):
    d = h_ref.shape[-1]
    x = _token_rows(pl.program_id(0), first, xp_ref, xs_ref, pos_ref)
    h_ref[...] = (x * (1.0 + mod_ref[:, d:2 * d]) + mod_ref[:, 0:d]).astype(h_ref.dtype)


def _modulate(xp, xs, pos, mod, *, t_sample):
    n_p, d = xp.shape
    n = n_p + xs.shape[0]
    return pl.pallas_call(
        functools.partial(_modulate_body, first=n_p // TOK_TILE),
        grid=(n // TOK_TILE,),
        in_specs=_token_specs(TOK_TILE, n_p, t_sample, d),
        out_specs=pl.BlockSpec((TOK_TILE, d), lambda i: (i, 0)),
        out_shape=jax.ShapeDtypeStruct((n, d), BF16),
        compiler_params=pltpu.CompilerParams(dimension_semantics=("parallel",), vmem_limit_bytes=VMEM_LIMIT_BYTES),
        name="modulate",
    )(xp, xs, pos, mod)


def _gconv_body(ub_ref, uc_ref, ux_ref, w_ref, o_ref, *, t_first, t_rest):
    t_seq = jnp.where(pl.program_id(0) == 0, t_first, t_rest)
    x = uc_ref[...] * ux_ref[...]
    n = x.shape[0]
    tpos = lax.broadcasted_iota(jnp.int32, x.shape, 0) & (t_seq - 1)
    xm = jnp.where(tpos == 0, 0.0, pltpu.roll(x, 1, axis=0))
    xp = jnp.where(tpos == t_seq - 1, 0.0, pltpu.roll(x, n - 1, axis=0))
    w = w_ref[...]
    o_ref[...] = (ub_ref[...] * (xm * w[0:1] + x * w[1:2] + xp * w[2:3])).astype(o_ref.dtype)


def _gated_conv(proj, conv_w, col0, *, t_first, t_rest):
    n = proj.shape[0]
    nblk = CONV_CH // DK
    b0 = col0 // DK
    return pl.pallas_call(
        functools.partial(_gconv_body, t_first=t_first, t_rest=t_rest),
        grid=(n // PREP_ROWS, nblk),
        in_specs=[pl.BlockSpec((PREP_ROWS, DK), lambda i, j: (i, b0 + j)),
                  pl.BlockSpec((PREP_ROWS, DK), lambda i, j: (i, b0 + nblk + j)),
                  pl.BlockSpec((PREP_ROWS, DK), lambda i, j: (i, b0 + 2 * nblk + j)),
                  pl.BlockSpec((3, DK), lambda i, j: (0, j))],
        out_specs=pl.BlockSpec((PREP_ROWS, DK), lambda i, j: (i, j)),
        out_shape=jax.ShapeDtypeStruct((n, CONV_CH), BF16),
        compiler_params=pltpu.CompilerParams(dimension_semantics=("parallel", "parallel"),
                                             vmem_limit_bytes=VMEM_LIMIT_BYTES),
        name="gated_conv",
    )(proj, proj, proj, conv_w)


BR_TM = 512
BR_TN = 512


def _branch_body(o_ref, g_ref, wa_ref, wb_ref, ra_ref, rb_ref, m_ref, wa_bf, wb_bf):
    @pl.when(pl.program_id(1) == 0)
    def _():
        wa_bf[...] = wa_ref[...].astype(BF16)
        wb_bf[...] = wb_ref[...].astype(BF16)
    ya = jnp.dot(o_ref[...], wa_bf[...], preferred_element_type=F32)
    yb = jnp.dot(g_ref[...], wb_bf[...], preferred_element_type=F32)
    m_ref[...] = (jax.nn.sigmoid(ra_ref[...]) * ya + jax.nn.sigmoid(rb_ref[...]) * yb).astype(m_ref.dtype)


def _branch_merge(o, g, w_a, w_b, proj, ra_col, rb_col):
    n = o.shape[0]
    d = w_a.shape[1]
    el = pl.Element
    return pl.pallas_call(
        _branch_body,
        grid=(d // BR_TN, n // BR_TM),
        in_specs=[pl.BlockSpec((BR_TM, o.shape[1]), lambda j, i: (i, 0)),
                  pl.BlockSpec((BR_TM, g.shape[1]), lambda j, i: (i, 0)),
                  pl.BlockSpec((w_a.shape[0], BR_TN), lambda j, i: (0, j)),
                  pl.BlockSpec((w_b.shape[0], BR_TN), lambda j, i: (0, j)),
                  pl.BlockSpec((el(BR_TM), el(BR_TN)),
                               lambda j, i: (i * BR_TM, pl.multiple_of(ra_col + j * BR_TN, DK))),
                  pl.BlockSpec((el(BR_TM), el(BR_TN)),
                               lambda j, i: (i * BR_TM, pl.multiple_of(rb_col + j * BR_TN, DK)))],
        out_specs=pl.BlockSpec((BR_TM, BR_TN), lambda j, i: (i, j)),
        out_shape=jax.ShapeDtypeStruct((n, d), BF16),
        scratch_shapes=[pltpu.VMEM((w_a.shape[0], BR_TN), BF16), pltpu.VMEM((w_b.shape[0], BR_TN), BF16)],
        compiler_params=pltpu.CompilerParams(dimension_semantics=("arbitrary", "arbitrary"),
                                             vmem_limit_bytes=VMEM_LIMIT_BYTES),
        name="branch_merge",
    )(o, g, w_a, w_b, proj, proj)


def _ln_rows(v, g, b):
    mu = jnp.mean(v, axis=-1, keepdims=True)
    c = v - mu
    return c * lax.rsqrt(jnp.mean(c * c, axis=-1, keepdims=True) + EPS) * g + b


def _mix_body(m_ref, wo_ref, xp_ref, xs_ref, pos_ref, mod_ref, g_ref, b_ref, rw_ref, rb_ref,
              x1_ref, h2_ref, lg_ref, *, first):
    d = x1_ref.shape[-1]
    x = _token_rows(pl.program_id(0), first, xp_ref, xs_ref, pos_ref)
    mixed = jnp.dot(m_ref[...], wo_ref[...], preferred_element_type=F32)
    x1 = _ln_rows(DEEPNORM_ALPHA * x + mod_ref[:, 2 * d:3 * d] * mixed, g_ref[...], b_ref[...])
    h2 = x1 * (1.0 + mod_ref[:, 4 * d:5 * d]) + mod_ref[:, 3 * d:4 * d]
    x1_ref[...] = x1
    h2_ref[...] = h2.astype(h2_ref.dtype)
    lg_ref[...] = jnp.dot(h2, rw_ref[...], precision=lax.Precision.HIGHEST, preferred_element_type=F32) + rb_ref[...]


def _mix_norm_route(m, w_o_bf, xp, xs, pos, mod, ln_g, ln_b, router_w, router_b, *, t_sample):
    n, d = m.shape
    n_p = xp.shape[0]
    ne = router_w.shape[1]
    const = lambda i: (0, 0)
    return pl.pallas_call(
        functools.partial(_mix_body, first=n_p // MIX_TILE),
        grid=(n // MIX_TILE,),
        in_specs=[pl.BlockSpec((MIX_TILE, d), lambda i: (i, 0)), pl.BlockSpec((d, d), const)]
        + _token_specs(MIX_TILE, n_p, t_sample, d)
        + [pl.BlockSpec((1, d), const), pl.BlockSpec((1, d), const),
           pl.BlockSpec((d, ne), const), pl.BlockSpec((1, ne), const)],
        out_specs=[pl.BlockSpec((MIX_TILE, d), lambda i: (i, 0)), pl.BlockSpec((MIX_TILE, d), lambda i: (i, 0)),
                   pl.BlockSpec((MIX_TILE, ne), lambda i: (i, 0))],
        out_shape=[jax.ShapeDtypeStruct((n, d), F32), jax.ShapeDtypeStruct((n, d), F32),
                   jax.ShapeDtypeStruct((n, ne), F32)],
        compiler_params=pltpu.CompilerParams(dimension_semantics=("parallel",), vmem_limit_bytes=VMEM_LIMIT_BYTES),
        name="mix_norm_route",
    )(m, w_o_bf, xp, xs, pos, mod, ln_g.reshape(1, d), ln_b.reshape(1, d), router_w, router_b.reshape(1, ne))


DISPATCH_ROWS = MOE_TM


def _dispatch_body(tok_ref, nv_ref, h_hbm, xs_ref, buf, sem):
    i = pl.program_id(0)
    n_valid = nv_ref[0]

    def issue(step, which):
        base = step * DISPATCH_ROWS

        def body(pair, carry):
            for u in range(2):
                r = 2 * pair + u
                pltpu.make_async_copy(h_hbm.at[pl.ds(tok_ref[base + r], 1), :], buf.at[which, pl.ds(r, 1), :],
                                      sem.at[which]).start(priority=u)
            return carry
        lax.fori_loop(0, DISPATCH_ROWS // 2, body, 0)

    @pl.when(jnp.logical_and(i == 0, n_valid > 0))
    def _():
        issue(0, 0)

    @pl.when(i + 1 < n_valid)
    def _():
        issue(i + 1, (i + 1) % 2)

    @pl.when(i < n_valid)
    def _():
        cur = i % 2
        pltpu.make_async_copy(h_hbm.at[pl.ds(0, DISPATCH_ROWS), :], buf.at[cur], sem.at[cur]).wait()
        xs_ref[...] = buf[cur]

    @pl.when(i >= n_valid)
    def _():
        xs_ref[...] = jnp.zeros_like(xs_ref)


def _dispatch(slot_tok, n_valid, h):
    n_slots = slot_tok.shape[0]
    d = h.shape[1]
    grid_spec = pltpu.PrefetchScalarGridSpec(
        num_scalar_prefetch=2,
        grid=(n_slots // DISPATCH_ROWS,),
        in_specs=[pl.BlockSpec(memory_space=pl.ANY)],
        out_specs=pl.BlockSpec((DISPATCH_ROWS, d), lambda i, t, nv: (i, 0)),
        scratch_shapes=[pltpu.VMEM((2, DISPATCH_ROWS, d), h.dtype), pltpu.SemaphoreType.DMA((2,))],
    )
    return pl.pallas_call(
        _dispatch_body,
        grid_spec=grid_spec,
        out_shape=jax.ShapeDtypeStruct((n_slots, d), h.dtype),
        compiler_params=pltpu.CompilerParams(dimension_semantics=("arbitrary",), vmem_limit_bytes=VMEM_LIMIT_BYTES),
        name="dispatch",
    )(slot_tok, n_valid, h)


def _combine_body(slot_ref, y_hbm, p_ref, x1_ref, mod_ref, g_ref, b_ref, o_ref, buf, sem, *, n_tiles):
    tt = COMBINE_TILE
    d = o_ref.shape[-1]
    i = pl.program_id(0)

    def row_copy(src_row, dst_row, which):
        return pltpu.make_async_copy(y_hbm.at[pl.ds(src_row, 1), :], buf.at[which, pl.ds(dst_row, 1), :],
                                     sem.at[which])

    def issue(tile, which):
        base = tile * (tt * TOP_K)

        def body(t, carry):
            for k in range(TOP_K):
                row_copy(slot_ref[base + t * TOP_K + k], k * tt + t, which).start(priority=k % 2)
            return carry
        lax.fori_loop(0, tt, body, 0)

    @pl.when(i == 0)
    def _():
        issue(0, 0)

    @pl.when(i + 1 < n_tiles)
    def _():
        issue(i + 1, (i + 1) % 2)

    cur = i % 2
    pltpu.make_async_copy(y_hbm.at[pl.ds(0, TOP_K * tt), :], buf.at[cur], sem.at[cur]).wait()
    p = p_ref[...]
    ff = p[:, 0:1] * buf[cur, 0:tt, :]
    for k in range(1, TOP_K):
        ff = ff + p[:, k:k + 1] * buf[cur, k * tt:(k + 1) * tt, :]
    o_ref[...] = _ln_rows(DEEPNORM_ALPHA * x1_ref[...] + mod_ref[:, 5 * d:6 * d] * ff, g_ref[...], b_ref[...])


def _combine_norm(slot, y, top_p, x1, mod, ln_g, ln_b, *, n_prompt, t_sample):
    n, d = x1.shape
    tt = COMBINE_TILE
    n_tiles = n // tt
    grid_spec = pltpu.PrefetchScalarGridSpec(
        num_scalar_prefetch=1,
        grid=(n_tiles,),
        in_specs=[pl.BlockSpec(memory_space=pl.ANY),
                  pl.BlockSpec((tt, TOP_K), lambda i, s: (i, 0)),
                  pl.BlockSpec((tt, d), lambda i, s: (i, 0)),
                  pl.BlockSpec((None, 1, 6 * d), lambda i, s: (_group_of_tile(i, tt, n_prompt, t_sample), 0, 0)),
                  pl.BlockSpec((1, d), lambda i, s: (0, 0)),
                  pl.BlockSpec((1, d), lambda i, s: (0, 0))],
        out_specs=pl.BlockSpec((tt, d), lambda i, s: (i, 0)),
        scratch_shapes=[pltpu.VMEM((2, TOP_K * tt, d), F32), pltpu.SemaphoreType.DMA((2,))],
    )
    return pl.pallas_call(
        functools.partial(_combine_body, n_tiles=n_tiles),
        grid_spec=grid_spec,
        out_shape=jax.ShapeDtypeStruct((n, d), F32),
        compiler_params=pltpu.CompilerParams(dimension_semantics=("arbitrary",), vmem_limit_bytes=VMEM_LIMIT_BYTES),
        name="combine_norm",
    )(slot, y, top_p, x1, mod, ln_g.reshape(1, d), ln_b.reshape(1, d))


def _grid_pos_embed(rows, dim):
    quarter = dim // 4
    freqs = 1.0 / (POS_BASE ** (jnp.arange(quarter, dtype=F32) / quarter))

    def axis_embed(n):
        ang = jnp.arange(n, dtype=F32)[:, None] * freqs[None, :]
        return jnp.concatenate([jnp.sin(ang), jnp.cos(ang)], axis=-1)
    er = axis_embed(rows)
    ec = axis_embed(GRID_W)
    pos = jnp.concatenate([jnp.broadcast_to(er[:, None, :], (rows, GRID_W, dim // 2)),
                           jnp.broadcast_to(ec[None, :, :], (rows, GRID_W, dim // 2))], axis=-1)
    return pos.reshape(rows * GRID_W, dim)


def kernel(x_prompt, x_sample, c, state_fwd, state_bwd, c_ctx, w_mod, b_mod, w_in, conv_qkv, a_log, dt_bias,
           norm_o, w_a_out, conv_b, w_b_out, w_o, ln1_g, ln1_b, router_w, router_b, w_gu, b_gu, w_down, b_down,
           ln2_g, ln2_b):
    nb_p, t_p, d = x_prompt.shape
    nb_s, t_s, _ = x_sample.shape
    n_p = nb_p * t_p
    n_s = nb_s * t_s
    assert n_p == PREP_ROWS and t_s == PREP_ROWS
    xp = x_prompt.reshape(n_p, d)
    xs = x_sample.reshape(n_s, d)
    pos = _grid_pos_embed(t_s // GRID_W, d)
    offs = tuple(int(o) for o in np.cumsum(IN_SIZES)[:-1])

    l = 0
    cvec = jnp.concatenate([c_ctx[None, :], c], axis=0)
    cpad = jnp.zeros((8, d), F32).at[:1 + nb_s].set(jax.nn.silu(cvec))
    mod = (_mm(cpad, w_mod[l], tm=8, tn=512)[:1 + nb_s] + b_mod[l])[:, None, :]

    h = _modulate(xp, xs, pos, mod, t_sample=t_s)
    proj = _mm(h, w_in[l], tm=2048, tn=384)

    qkvn = _qkv_prep(proj, conv_qkv[l], t_first=t_p, t_rest=t_s)
    cols, rws = _decay_tables(proj[:, offs[1]:offs[3]], a_log[l], dt_bias[l])
    zero_state = jnp.zeros((PROMPT_SUB, NV_HEADS, DK, DV), F32)
    o_all = jnp.zeros((n_p + n_s, V_W), BF16)
    o_all, sf, sb = _deltanet(qkvn, proj, cols, rws, norm_o[l], zero_state, zero_state, o_all,
                              n_seq=nb_p, t_seq=t_p, n_sub=PROMPT_SUB, row_blk0=0, zero_init=True)
    o_all, _, _ = _deltanet(qkvn, proj, cols, rws, norm_o[l], state_fwd[:, l], state_bwd[:, l], o_all,
                            n_seq=nb_s, t_seq=t_s, n_sub=1, row_blk0=n_p // t_s, zero_init=False)

    gconv = _gated_conv(proj, conv_b[l], offs[3], t_first=t_p, t_rest=t_s)
    m = _branch_merge(o_all, gconv, w_a_out[l], w_b_out[l], proj, offs[6], offs[7])
    x1, h2, logits = _mix_norm_route(m, w_o[l].astype(BF16), xp, xs, pos, mod, ln1_g[l], ln1_b[l],
                                     router_w[l], router_b[l], t_sample=t_s)
    slot, slot_tok, top_p, block_e, n_valid = _route(logits)
    y = _moe_blocks(_dispatch(slot_tok, n_valid, h2), block_e, n_valid, w_gu[l], b_gu[l], w_down[l], b_down[l])
    x2 = _combine_norm(slot, y, top_p, x1, mod, ln2_g[l], ln2_b[l], n_prompt=n_p, t_sample=t_s)
    y_prompt = x2[:n_p].reshape(nb_p, t_p, d)
    y_sample = x2[n_p:].reshape(nb_s, t_s, d)
    return (y_prompt, y_sample, sf[:, None], sb[:, None])
```

```python
import functools

import jax
import jax.numpy as jnp
import numpy as np
from jax import lax
from jax.experimental import pallas as pl
from jax.experimental.pallas import tpu as pltpu

F32 = jnp.float32
BF16 = jnp.bfloat16

D_MODEL = 2048
GRID_W = 64
NK_HEADS = 16
NV_HEADS = 32
DK = 128
DV = 128
QK_W = NK_HEADS * DK
V_W = NV_HEADS * DV
QKV_W = 2 * QK_W + V_W
CHUNK = 64
CONV_CH = D_MODEL
IN_SIZES = (QKV_W, V_W, 2 * NV_HEADS, 2 * NV_HEADS, CONV_CH, CONV_CH, CONV_CH, D_MODEL, D_MODEL)
N_EXPERTS = 32
TOP_K = 4
D_FF = D_MODEL
SWIGLU_LIMIT = 7.0
SWIGLU_ALPHA = 1.702
DEPTH = 1
DEEPNORM_ALPHA = (2 * DEPTH) ** 0.25
EPS = 1e-6
POS_BASE = 10000.0

VMEM_LIMIT_BYTES = 56 * 1024 * 1024


def _mm_body(x_ref, w_ref, o_ref):
    o_ref[...] = jnp.dot(x_ref[...].astype(BF16), w_ref[...].astype(BF16), preferred_element_type=F32)


def _mm(x, w, *, tm, tn):
    m, k = x.shape
    n = w.shape[1]
    assert m % tm == 0 and n % tn == 0
    return pl.pallas_call(
        _mm_body,
        name="matmul",
        grid=(m // tm, n // tn),
        in_specs=[pl.BlockSpec((tm, k), lambda i, j: (i, 0)),
                  pl.BlockSpec((k, tn), lambda i, j: (0, j))],
        out_specs=pl.BlockSpec((tm, tn), lambda i, j: (i, j)),
        out_shape=jax.ShapeDtypeStruct((m, n), F32),
        compiler_params=pltpu.CompilerParams(dimension_semantics=("parallel", "parallel"),
                                             vmem_limit_bytes=VMEM_LIMIT_BYTES),
    )(x, w)


MOE_TM = 512
MOE_TF = 512


def _pack_bf16_pairs(x):
    n = x.shape[1] // 2
    lo = lax.bitcast_convert_type(x[:, :n].astype(BF16).astype(F32), jnp.uint32)
    hi = lax.bitcast_convert_type(x[:, n:].astype(BF16).astype(F32), jnp.uint32)
    return (lo >> 16) | hi


def _unpack_bf16_pairs(w):
    lo = lax.bitcast_convert_type(w << 16, F32)
    hi = lax.bitcast_convert_type(w & jnp.uint32(0xFFFF0000), F32)
    return jnp.concatenate([lo, hi], axis=1).astype(BF16)


def _moe_body(be_ref, nv_ref, x_ref, wg_ref, wu_ref, bg_ref, bu_ref, wd_ref, bd_ref, o_ref, acc_ref):
    i = pl.program_id(0)
    f = pl.program_id(1)
    valid = i < nv_ref[0]

    @pl.when(f == 0)
    def _():
        acc_ref[...] = jnp.zeros_like(acc_ref)

    @pl.when(valid)
    def _():
        x = _unpack_bf16_pairs(x_ref[...])
        g = jnp.dot(x, wg_ref[...].astype(BF16), preferred_element_type=F32) + bg_ref[...]
        u = jnp.dot(x, wu_ref[...].astype(BF16), preferred_element_type=F32) + bu_ref[...]
        gate = jnp.minimum(g, SWIGLU_LIMIT)
        up = jnp.clip(u, -SWIGLU_LIMIT, SWIGLU_LIMIT)
        act = (up + 1.0) * gate * jax.nn.sigmoid(SWIGLU_ALPHA * gate)
        acc_ref[...] += jnp.dot(act.astype(BF16), wd_ref[...].astype(BF16), preferred_element_type=F32)

    @pl.when(f == pl.num_programs(1) - 1)
    def _():
        o_ref[...] = jnp.where(valid, acc_ref[...] + bd_ref[...], 0.0)


def _moe_blocks(xs, block_e, n_valid, w_gu, b_gu, w_down, b_down):
    n_slots = xs.shape[0]
    d = w_down.shape[-1]
    n_blocks = n_slots // MOE_TM
    nf = D_FF // MOE_TF

    def f_eff(i, f, nv):
        return jnp.where(i < nv[0], f, nf - 1)

    grid_spec = pltpu.PrefetchScalarGridSpec(
        num_scalar_prefetch=2,
        grid=(n_blocks, nf),
        in_specs=[
            pl.BlockSpec((MOE_TM, xs.shape[1]), lambda i, f, be, nv: (i, 0)),
            pl.BlockSpec((None, d, MOE_TF), lambda i, f, be, nv: (be[i], 0, f_eff(i, f, nv))),
            pl.BlockSpec((None, d, MOE_TF), lambda i, f, be, nv: (be[i], 0, nf + f_eff(i, f, nv))),
            pl.BlockSpec((None, 1, MOE_TF), lambda i, f, be, nv: (be[i], 0, f_eff(i, f, nv))),
            pl.BlockSpec((None, 1, MOE_TF), lambda i, f, be, nv: (be[i], 0, nf + f_eff(i, f, nv))),
            pl.BlockSpec((None, MOE_TF, d), lambda i, f, be, nv: (be[i], f_eff(i, f, nv), 0)),
            pl.BlockSpec((None, 1, d), lambda i, f, be, nv: (be[i], 0, 0)),
        ],
        out_specs=pl.BlockSpec((MOE_TM, d), lambda i, f, be, nv: (i, 0)),
        scratch_shapes=[pltpu.VMEM((MOE_TM, d), F32)],
    )
    return pl.pallas_call(
        _moe_body,
        name="moe_experts",
        grid_spec=grid_spec,
        out_shape=jax.ShapeDtypeStruct((n_slots, d), F32),
        compiler_params=pltpu.CompilerParams(dimension_semantics=("arbitrary", "arbitrary"),
                                             vmem_limit_bytes=VMEM_LIMIT_BYTES),
    )(block_e, n_valid, xs, w_gu, w_gu, b_gu[:, None, :], b_gu[:, None, :], w_down, b_down[:, None, :])


def _route(logits):
    n_tok = logits.shape[0]
    n_assign = n_tok * TOP_K
    top_logit, top_e = lax.top_k(logits, TOP_K)
    top_p = jax.nn.softmax(top_logit, axis=-1)
    flat_e = top_e.reshape(-1)
    onehot = (flat_e[:, None] == jnp.arange(N_EXPERTS)[None, :]).astype(jnp.int32)
    csum = jnp.cumsum(onehot, axis=0)
    counts = csum[-1]
    rank = jnp.take_along_axis(csum, flat_e[:, None], axis=1)[:, 0] - 1
    padded = (counts + MOE_TM - 1) // MOE_TM * MOE_TM
    pad_end = jnp.cumsum(padded)
    pad_start = pad_end - padded
    slot = (pad_start[flat_e] + rank).astype(jnp.int32)
    n_blocks = -(-(n_assign + N_EXPERTS * (MOE_TM - 1)) // MOE_TM)
    n_slots = n_blocks * MOE_TM
    slot_tok = jnp.zeros((n_slots,), jnp.int32).at[slot].set(jnp.arange(n_assign, dtype=jnp.int32) // TOP_K)
    n_valid = (pad_end[-1] // MOE_TM).astype(jnp.int32)
    blk = jnp.arange(n_blocks, dtype=jnp.int32)
    block_e = jnp.searchsorted(pad_end, jnp.minimum(blk, n_valid - 1) * MOE_TM, side='right').astype(jnp.int32)
    block_e = jnp.minimum(block_e, N_EXPERTS - 1)
    return slot, slot_tok, top_p, block_e, n_valid.reshape(1)


PREP_ROWS = 4096


def _prep_body(x_ref, w_ref, o_ref, *, t_first, t_rest):
    i = pl.program_id(0)
    j = pl.program_id(1)
    t_seq = jnp.where(i == 0, t_first, t_rest)
    x = x_ref[...]
    n = x.shape[0]
    row = lax.broadcasted_iota(jnp.int32, x.shape, 0)
    tpos = row & (t_seq - 1)
    xm = jnp.where(tpos == 0, 0.0, pltpu.roll(x, 1, axis=0))
    xp = jnp.where(tpos == t_seq - 1, 0.0, pltpu.roll(x, n - 1, axis=0))
    w = w_ref[...]
    y = xm * w[0:1] + x * w[1:2] + xp * w[2:3]
    y = y * jax.nn.sigmoid(y)
    rs = lax.rsqrt(jnp.sum(y * y, axis=-1, keepdims=True) + EPS)
    nq = QK_W // DK
    fac = jnp.where(j < nq, rs * (DK ** -0.5), jnp.where(j < 2 * nq, rs, 1.0))
    o_ref[...] = (y * fac).astype(o_ref.dtype)


def _qkv_prep(proj, conv_w, *, t_first, t_rest):
    n = proj.shape[0]
    return pl.pallas_call(
        functools.partial(_prep_body, t_first=t_first, t_rest=t_rest),
        name="qkv_prep",
        grid=(n // PREP_ROWS, QKV_W // DK),
        in_specs=[pl.BlockSpec((PREP_ROWS, DK), lambda i, j: (i, j)),
                  pl.BlockSpec((3, DK), lambda i, j: (0, j))],
        out_specs=pl.BlockSpec((PREP_ROWS, DK), lambda i, j: (i, j)),
        out_shape=jax.ShapeDtypeStruct((n, QKV_W), BF16),
        compiler_params=pltpu.CompilerParams(dimension_semantics=("parallel", "parallel"),
                                             vmem_limit_bytes=VMEM_LIMIT_BYTES),
    )(proj, conv_w)


PROMPT_SUB = 2
SOLVE_WAYS = 8


def _mmul(a, b):
    return jnp.dot(a, b, preferred_element_type=F32)


def _delta_body(q_ref, k_ref, v_ref, z_ref, cols_ref, rows_ref, nw_ref, s0f_ref, s0b_ref, o_prev_ref,
                o_ref, sf_ref, sb_ref, acc_ref, s_ref, zz_ref, pr_ref, kt_ref, *, n_sub, n_chunks, zero_init):
    del o_prev_ref
    n_all = n_sub * n_chunks
    ways = min(SOLVE_WAYS, n_all)
    lane = lax.broadcasted_iota(jnp.int32, (CHUNK, 2 * CHUNK), 1)
    row = lax.broadcasted_iota(jnp.int32, (CHUNK, 2 * CHUNK), 0)
    pos = lane & (CHUNK - 1)
    left = lane < CHUNK
    left1 = lax.broadcasted_iota(jnp.int32, (1, 2 * CHUNK), 1) < CHUNK
    eye2 = (pos == row).astype(F32)
    incl = (pos <= row, pos >= row)
    strict = (pos < row, pos > row)
    same8 = (row >> 3) == (pos >> 3)
    same16 = (row >> 4) == (pos >> 4)
    same32 = (row >> 5) == (pos >> 5)
    nt = (((1,), (1,)), ((), ()))
    tn = (((0,), (0,)), ((), ()))

    def split(m):
        return jnp.where(left, m, 0.0), jnp.where(left, 0.0, m)

    def bdiag(m):
        return jnp.concatenate(split(m), axis=0).astype(BF16)

    def pmm(a, b):
        return _mmul(a.astype(BF16), bdiag(b))

    def inv_unit_tri(lms):
        d8 = [jnp.where(same8, lm, 0.0) for lm in lms]
        xs = [eye2 - d for d in d8]
        d2 = [pmm(d, d) for d in d8]
        xs = [x + pmm(x, d) for x, d in zip(xs, d2)]
        d4 = [pmm(d, d) for d in d2]
        xs = [x + pmm(x, d) for x, d in zip(xs, d4)]
        for keep in (same16 & ~same8, same32 & ~same16, ~same32):
            ys = [pmm(jnp.where(keep, lm, 0.0), x) for lm, x in zip(lms, xs)]
            xs = [x - pmm(x, y) for x, y in zip(xs, ys)]
        return xs

    def chunk_terms(c, d):
        r0 = pl.multiple_of(c * CHUNK, CHUNK)
        cols = cols_ref[pl.ds(r0, CHUNK), :]
        rows = rows_ref[c]

        def lanes_of(col):
            return jnp.broadcast_to(cols[:, col:col + 1], (CHUNK, 2 * CHUNK))
        gcol = [lanes_of(2 * d + h) for h in (0, 1)]
        bcol = [lanes_of(4 + 2 * d + h) for h in (0, 1)]
        gl = [rows[4 + 2 * d + h:5 + 2 * d + h] for h in (0, 1)]
        return r0, rows, gcol, bcol, gl

    acc_ref[...] = jnp.zeros_like(acc_ref)
    rep = s0f_ref.shape[1]
    if zero_init:
        s_ref[...] = jnp.zeros_like(s_ref)
    else:
        for sub in range(n_sub):
            for h in range(rep):
                s_ref[2 * rep * sub + h] = s0f_ref[sub, h]
                s_ref[2 * rep * sub + rep + h] = s0b_ref[sub, h]

    def solve(i, carry):
        chunks = [i + part * (n_all // ways) for part in range(ways)]
        qcs, kcs, vcs, kkms, qkms = [], [], [], [], []
        for c in chunks:
            r0 = pl.multiple_of(c * CHUNK, CHUNK)
            kb = k_ref[pl.ds(r0, CHUNK), :]
            qb = q_ref[pl.ds(r0, CHUNK), :]
            kk2 = jnp.concatenate([kb, kb], axis=0)
            kcs.append(kb.astype(F32))
            qcs.append(qb.astype(F32))
            vcs.append(v_ref[pl.ds(r0, CHUNK), :])
            kkms.append(lax.dot_general(kb, kk2, nt, preferred_element_type=F32))
            qkms.append(lax.dot_general(qb, kk2, nt, preferred_element_type=F32))
            kt_ref[c] = jnp.transpose(kcs[-1]).astype(BF16)
        chains = [(j, d) for j in range(ways) for d in (0, 1)]
        lms, rhs2s, a2s, decays = [], [], [], []
        for j, d in chains:
            c = chunks[j]
            _, rows, gcol, bcol, gl = chunk_terms(c, d)
            gcol2 = jnp.where(left, gcol[0], gcol[1])
            bcol2 = jnp.where(left, bcol[0], bcol[1])
            grow2 = jnp.where(left1, rows[2 * d:2 * d + 1], rows[2 * d + 1:2 * d + 2])
            dm = jnp.exp(jnp.where(incl[d], gcol2 - grow2, -jnp.inf))
            a2s.append(qkms[j] * dm)
            eg = [jnp.exp(g) for g in gcol]
            decays.append((eg, [jnp.exp(gl[h] - gcol[h]) for h in (0, 1)]))
            lms.append(jnp.where(strict[d], kkms[j] * dm * bcol2, 0.0))
            rhs = [jnp.concatenate([vcs[j][:, h * DV:(h + 1) * DV] * bcol[h], kcs[j] * (bcol[h] * eg[h])], axis=1)
                   for h in (0, 1)]
            rhs2s.append(jnp.concatenate(rhs, axis=0).astype(BF16))
        t2s = inv_unit_tri(lms)
        sols = [[_mmul(th.astype(BF16), rhs2) for th in split(t2)]
                for t2, rhs2 in zip(t2s, rhs2s)]
        aws = [[_mmul(ah.astype(BF16), jnp.concatenate(sol, axis=0).astype(BF16)) for ah in split(a2)]
               for a2, sol in zip(a2s, sols)]
        for (j, d), sol, aw, (eg, ek) in zip(chains, sols, aws, decays):
            c = chunks[j]
            zz, pr = [], []
            for h in (0, 1):
                zz += [sol[h][:, :DV] * ek[h], sol[h][:, DV:] * ek[h]]
                pr += [qcs[j] * eg[h] - aw[h][:, DV:], aw[h][:, :DV]]
            zz_ref[d, c] = jnp.concatenate(zz, axis=1).astype(BF16)
            pr_ref[d, c] = jnp.concatenate(pr, axis=1).astype(BF16)
        return carry

    lax.fori_loop(0, n_all // ways, solve, 0)

    def step(i, carry):
        lanes = [(sub, d) for sub in range(n_sub) for d in (0, 1)]
        where = [sub * n_chunks + (i if d == 0 else n_chunks - 1 - i) for sub, d in lanes]
        ngs, prs, gls = [], [], []
        for c, (_, d) in zip(where, lanes):
            rows = rows_ref[c]
            gls.append([rows[4 + 2 * d + h:5 + 2 * d + h] for h in (0, 1)])
            ngs.append(_mmul(kt_ref[c], zz_ref[d, c]))
            prs.append(pr_ref[d, c])
        streams = [(n, h) for n in range(len(lanes)) for h in (0, 1)]
        outs = []
        for n, h in streams:
            pg = jnp.concatenate([prs[n][:, 2 * h * DV:(2 * h + 1) * DV],
                                  ngs[n][:, (2 * h + 1) * DV:(2 * h + 2) * DV].astype(BF16)], axis=0)
            outs.append(_mmul(pg, s_ref[2 * n + h].astype(BF16)))
        for (n, h), out in zip(streams, outs):
            s = 2 * n + h
            r0 = pl.multiple_of(where[n] * CHUNK, CHUNK)
            acc_ref[pl.ds(r0, CHUNK), h * DV:(h + 1) * DV] += (
                out[:CHUNK] + prs[n][:, (2 * h + 1) * DV:(2 * h + 2) * DV].astype(F32))
            s_ref[s] = s_ref[s] * jnp.exp(gls[n][h]) - out[CHUNK:] + ngs[n][:, 2 * h * DV:(2 * h + 1) * DV]
        return carry

    lax.fori_loop(0, n_chunks, step, 0)

    nw = nw_ref[...]

    def finish(j, carry):
        r0 = pl.multiple_of(j * CHUNK, CHUNK)
        o = acc_ref[pl.ds(r0, CHUNK), :]
        z = z_ref[pl.ds(r0, CHUNK), :]
        for h in (0, 1):
            oh = o[:, h * DV:(h + 1) * DV]
            zh = z[:, h * DV:(h + 1) * DV]
            y = oh * lax.rsqrt(jnp.mean(oh * oh, axis=-1, keepdims=True) + EPS) * nw * (zh * jax.nn.sigmoid(zh))
            o_ref[pl.ds(r0, CHUNK), h * DV:(h + 1) * DV] = y.astype(o_ref.dtype)
        return carry

    lax.fori_loop(0, n_all, finish, 0)
    for sub in range(n_sub):
        for h in range(rep):
            sf_ref[sub, h] = s_ref[2 * rep * sub + h]
            sb_ref[sub, h] = s_ref[2 * rep * sub + rep + h]


def _deltanet(qkvn, proj, cols, rows, norm_w, s0f, s0b, o_prev, *, n_seq, t_seq, n_sub, row_blk0, zero_init):
    n_chunks = t_seq // CHUNK
    t_blk = n_sub * t_seq
    c_blk = n_sub * n_chunks
    nq = QK_W // DK
    z_blk0 = QKV_W // (2 * DV)
    rep = NV_HEADS // NK_HEADS
    state_spec = pl.BlockSpec((n_sub, rep, DK, DV), lambda b, kh: (b, kh, 0, 0))
    in_specs = [
        pl.BlockSpec((t_blk, DK), lambda b, kh: (row_blk0 + b, kh)),
        pl.BlockSpec((t_blk, DK), lambda b, kh: (row_blk0 + b, nq + kh)),
        pl.BlockSpec((t_blk, rep * DV), lambda b, kh: (row_blk0 + b, nq + kh)),
        pl.BlockSpec((t_blk, rep * DV), lambda b, kh: (row_blk0 + b, z_blk0 + kh)),
        pl.BlockSpec((None, t_blk, 8), lambda b, kh: (kh, row_blk0 + b, 0)),
        pl.BlockSpec((None, c_blk, 8, 2 * CHUNK), lambda b, kh: (kh, row_blk0 + b, 0, 0)),
        pl.BlockSpec((1, DV), lambda b, kh: (0, 0)),
        pl.BlockSpec((n_sub, rep, DK, DV), lambda b, kh: (0 if zero_init else b, kh, 0, 0)),
        pl.BlockSpec((n_sub, rep, DK, DV), lambda b, kh: (0 if zero_init else b, kh, 0, 0)),
        pl.BlockSpec(memory_space=pl.ANY),
    ]
    out_specs = [pl.BlockSpec((t_blk, rep * DV), lambda b, kh: (row_blk0 + b, kh)), state_spec, state_spec]
    out_shape = [
        jax.ShapeDtypeStruct((qkvn.shape[0], V_W), BF16),
        jax.ShapeDtypeStruct((n_seq, NV_HEADS, DK, DV), F32),
        jax.ShapeDtypeStruct((n_seq, NV_HEADS, DK, DV), F32),
    ]
    args = [qkvn, qkvn, qkvn, proj, cols, rows, norm_w.reshape(1, DV), s0f, s0b, o_prev]
    return pl.pallas_call(
        functools.partial(_delta_body, n_sub=n_sub, n_chunks=n_chunks, zero_init=zero_init),
        grid=(n_seq // n_sub, NK_HEADS),
        input_output_aliases={len(args) - 1: 0},
        name="deltanet",
        in_specs=in_specs,
        out_specs=out_specs,
        out_shape=out_shape,
        scratch_shapes=[pltpu.VMEM((t_blk, rep * DV), F32), pltpu.VMEM((2 * rep * n_sub, DK, DV), F32),
                        pltpu.VMEM((2, c_blk, CHUNK, 2 * rep * DV), BF16),
                        pltpu.VMEM((2, c_blk, CHUNK, 2 * rep * DV), BF16),
                        pltpu.VMEM((c_blk, DK, CHUNK), BF16)],
        compiler_params=pltpu.CompilerParams(dimension_semantics=("parallel", "parallel"),
                                             vmem_limit_bytes=VMEM_LIMIT_BYTES),
    )(*args)


def _decay_tables(ab, a_log, dt_bias):
    n = ab.shape[0]
    nh2 = 2 * NV_HEADS
    rep = NV_HEADS // NK_HEADS
    beta = jax.nn.sigmoid(ab[:, :nh2])
    g = -jnp.exp(a_log.reshape(nh2)) * jax.nn.softplus(ab[:, nh2:] + dt_bias.reshape(nh2))
    gch = g.reshape(n // CHUNK, CHUNK, 2, NV_HEADS)
    gcf = jnp.cumsum(gch[:, :, 0], axis=1)
    gcb = jnp.flip(jnp.cumsum(jnp.flip(gch[:, :, 1], 1), axis=1), 1)
    gc = jnp.stack([gcf, gcb], axis=2)
    gl = jnp.stack([gcf[:, -1], gcb[:, 0]], axis=1)

    def per_kh(a):
        a = a.reshape(a.shape[:-2] + (2, NK_HEADS, rep))
        return jnp.moveaxis(a, -3, -2).reshape(a.shape[:-3] + (NK_HEADS, 2 * rep))
    gc_k = per_kh(gc)
    cols = jnp.concatenate([gc_k.reshape(n, NK_HEADS, 2 * rep), per_kh(beta.reshape(n, 2, NV_HEADS))], axis=-1)
    cols = jnp.transpose(cols, (1, 0, 2))
    gr = jnp.transpose(gc_k, (2, 0, 3, 1))
    gr = jnp.concatenate([gr, gr], axis=-1)
    glr = jnp.broadcast_to(jnp.transpose(per_kh(gl), (1, 0, 2))[..., None], gr.shape)
    rows = jnp.concatenate([gr, glr], axis=2)
    return cols, rows


TOK_TILE = 512
MIX_TILE = 256
COMBINE_TILE = 128


def _group_of_tile(i, tile, n_prompt, t_sample):
    first = n_prompt // tile
    return jnp.where(i < first, 0, 1 + (i - first) // (t_sample // tile))


def _token_specs(tile, n_prompt, t_sample, d):
    first = n_prompt // tile
    per_seq = t_sample // tile
    return [
        pl.BlockSpec((tile, d), lambda i, *_: (jnp.minimum(i, first - 1), 0)),
        pl.BlockSpec((tile, d), lambda i, *_: (jnp.maximum(i - first, 0), 0)),
        pl.BlockSpec((tile, d), lambda i, *_: (jnp.maximum(i - first, 0) % per_seq, 0)),
        pl.BlockSpec((None, 1, 6 * d), lambda i, *_: (_group_of_tile(i, tile, n_prompt, t_sample), 0, 0)),
    ]


def _token_rows(i, first, xp_ref, xs_ref, pos_ref):
    return jnp.where(i < first, xp_ref[...], xs_ref[...] + pos_ref[...])


def _modulate_body(xp_ref, xs_ref, pos_ref, mod_ref, h_ref, *, first):
    d = h_ref.shape[-1]
    x = _token_rows(pl.program_id(0), first, xp_ref, xs_ref, pos_ref)
    h_ref[...] = (x * (1.0 + mod_ref[:, d:2 * d]) + mod_ref[:, 0:d]).astype(h_ref.dtype)


def _modulate(xp, xs, pos, mod, *, t_sample):
    n_p, d = xp.shape
    n = n_p + xs.shape[0]
    return pl.pallas_call(
        functools.partial(_modulate_body, first=n_p // TOK_TILE),
        grid=(n // TOK_TILE,),
        in_specs=_token_specs(TOK_TILE, n_p, t_sample, d),
        out_specs=pl.BlockSpec((TOK_TILE, d), lambda i: (i, 0)),
        out_shape=jax.ShapeDtypeStruct((n, d), BF16),
        compiler_params=pltpu.CompilerParams(dimension_semantics=("parallel",), vmem_limit_bytes=VMEM_LIMIT_BYTES),
        name="modulate",
    )(xp, xs, pos, mod)


def _gconv_body(ub_ref, uc_ref, ux_ref, w_ref, o_ref, *, t_first, t_rest):
    t_seq = jnp.where(pl.program_id(0) == 0, t_first, t_rest)
    x = uc_ref[...] * ux_ref[...]
    n = x.shape[0]
    tpos = lax.broadcasted_iota(jnp.int32, x.shape, 0) & (t_seq - 1)
    xm = jnp.where(tpos == 0, 0.0, pltpu.roll(x, 1, axis=0))
    xp = jnp.where(tpos == t_seq - 1, 0.0, pltpu.roll(x, n - 1, axis=0))
    w = w_ref[...]
    o_ref[...] = (ub_ref[...] * (xm * w[0:1] + x * w[1:2] + xp * w[2:3])).astype(o_ref.dtype)


def _gated_conv(proj, conv_w, col0, *, t_first, t_rest):
    n = proj.shape[0]
    nblk = CONV_CH // DK
    b0 = col0 // DK
    return pl.pallas_call(
        functools.partial(_gconv_body, t_first=t_first, t_rest=t_rest),
        grid=(n // PREP_ROWS, nblk),
        in_specs=[pl.BlockSpec((PREP_ROWS, DK), lambda i, j: (i, b0 + j)),
                  pl.BlockSpec((PREP_ROWS, DK), lambda i, j: (i, b0 + nblk + j)),
                  pl.BlockSpec((PREP_ROWS, DK), lambda i, j: (i, b0 + 2 * nblk + j)),
                  pl.BlockSpec((3, DK), lambda i, j: (0, j))],
        out_specs=pl.BlockSpec((PREP_ROWS, DK), lambda i, j: (i, j)),
        out_shape=jax.ShapeDtypeStruct((n, CONV_CH), BF16),
        compiler_params=pltpu.CompilerParams(dimension_semantics=("parallel", "parallel"),
                                             vmem_limit_bytes=VMEM_LIMIT_BYTES),
        name="gated_conv",
    )(proj, proj, proj, conv_w)


BR_TM = 512
BR_TN = 512


def _branch_body(o_ref, g_ref, wa_ref, wb_ref, ra_ref, rb_ref, m_ref, wa_bf, wb_bf):
    @pl.when(pl.program_id(1) == 0)
    def _():
        wa_bf[...] = wa_ref[...].astype(BF16)
        wb_bf[...] = wb_ref[...].astype(BF16)
    ya = jnp.dot(o_ref[...], wa_bf[...], preferred_element_type=F32)
    yb = jnp.dot(g_ref[...], wb_bf[...], preferred_element_type=F32)
    m_ref[...] = (jax.nn.sigmoid(ra_ref[...]) * ya + jax.nn.sigmoid(rb_ref[...]) * yb).astype(m_ref.dtype)


def _branch_merge(o, g, w_a, w_b, proj, ra_col, rb_col):
    n = o.shape[0]
    d = w_a.shape[1]
    el = pl.Element
    return pl.pallas_call(
        _branch_body,
        grid=(d // BR_TN, n // BR_TM),
        in_specs=[pl.BlockSpec((BR_TM, o.shape[1]), lambda j, i: (i, 0)),
                  pl.BlockSpec((BR_TM, g.shape[1]), lambda j, i: (i, 0)),
                  pl.BlockSpec((w_a.shape[0], BR_TN), lambda j, i: (0, j)),
                  pl.BlockSpec((w_b.shape[0], BR_TN), lambda j, i: (0, j)),
                  pl.BlockSpec((el(BR_TM), el(BR_TN)),
                               lambda j, i: (i * BR_TM, pl.multiple_of(ra_col + j * BR_TN, DK))),
                  pl.BlockSpec((el(BR_TM), el(BR_TN)),
                               lambda j, i: (i * BR_TM, pl.multiple_of(rb_col + j * BR_TN, DK)))],
        out_specs=pl.BlockSpec((BR_TM, BR_TN), lambda j, i: (i, j)),
        out_shape=jax.ShapeDtypeStruct((n, d), BF16),
        scratch_shapes=[pltpu.VMEM((w_a.shape[0], BR_TN), BF16), pltpu.VMEM((w_b.shape[0], BR_TN), BF16)],
        compiler_params=pltpu.CompilerParams(dimension_semantics=("arbitrary", "arbitrary"),
                                             vmem_limit_bytes=VMEM_LIMIT_BYTES),
        name="branch_merge",
    )(o, g, w_a, w_b, proj, proj)


def _ln_rows(v, g, b):
    mu = jnp.mean(v, axis=-1, keepdims=True)
    c = v - mu
    return c * lax.rsqrt(jnp.mean(c * c, axis=-1, keepdims=True) + EPS) * g + b


def _mix_body(m_ref, wo_ref, xp_ref, xs_ref, pos_ref, mod_ref, g_ref, b_ref, rw_ref, rb_ref,
              x1_ref, h2_ref, lg_ref, *, first):
    d = x1_ref.shape[-1]
    x = _token_rows(pl.program_id(0), first, xp_ref, xs_ref, pos_ref)
    mixed = jnp.dot(m_ref[...], wo_ref[...], preferred_element_type=F32)
    x1 = _ln_rows(DEEPNORM_ALPHA * x + mod_ref[:, 2 * d:3 * d] * mixed, g_ref[...], b_ref[...])
    h2 = x1 * (1.0 + mod_ref[:, 4 * d:5 * d]) + mod_ref[:, 3 * d:4 * d]
    x1_ref[...] = x1
    h2_ref[...] = _pack_bf16_pairs(h2)
    lg_ref[...] = jnp.dot(h2, rw_ref[...], precision=lax.Precision.HIGHEST, preferred_element_type=F32) + rb_ref[...]


def _mix_norm_route(m, w_o_bf, xp, xs, pos, mod, ln_g, ln_b, router_w, router_b, *, t_sample):
    n, d = m.shape
    n_p = xp.shape[0]
    ne = router_w.shape[1]
    const = lambda i: (0, 0)
    return pl.pallas_call(
        functools.partial(_mix_body, first=n_p // MIX_TILE),
        grid=(n // MIX_TILE,),
        in_specs=[pl.BlockSpec((MIX_TILE, d), lambda i: (i, 0)), pl.BlockSpec((d, d), const)]
        + _token_specs(MIX_TILE, n_p, t_sample, d)
        + [pl.BlockSpec((1, d), const), pl.BlockSpec((1, d), const),
           pl.BlockSpec((d, ne), const), pl.BlockSpec((1, ne), const)],
        out_specs=[pl.BlockSpec((MIX_TILE, d), lambda i: (i, 0)), pl.BlockSpec((MIX_TILE, d // 2), lambda i: (i, 0)),
                   pl.BlockSpec((MIX_TILE, ne), lambda i: (i, 0))],
        out_shape=[jax.ShapeDtypeStruct((n, d), F32), jax.ShapeDtypeStruct((n, d // 2), jnp.uint32),
                   jax.ShapeDtypeStruct((n, ne), F32)],
        compiler_params=pltpu.CompilerParams(dimension_semantics=("parallel",), vmem_limit_bytes=VMEM_LIMIT_BYTES),
        name="mix_norm_route",
    )(m, w_o_bf, xp, xs, pos, mod, ln_g.reshape(1, d), ln_b.reshape(1, d), router_w, router_b.reshape(1, ne))


DISPATCH_ROWS = MOE_TM


def _dispatch_body(tok_ref, nv_ref, h_hbm, xs_ref, buf, sem):
    i = pl.program_id(0)
    n_valid = nv_ref[0]

    def issue(step, which):
        base = step * DISPATCH_ROWS

        def body(pair, carry):
            for u in range(2):
                r = 2 * pair + u
                pltpu.make_async_copy(h_hbm.at[pl.ds(tok_ref[base + r], 1), :], buf.at[which, pl.ds(r, 1), :],
                                      sem.at[which]).start(priority=u)
            return carry
        lax.fori_loop(0, DISPATCH_ROWS // 2, body, 0)

    @pl.when(jnp.logical_and(i == 0, n_valid > 0))
    def _():
        issue(0, 0)

    @pl.when(i + 1 < n_valid)
    def _():
        issue(i + 1, (i + 1) % 2)

    @pl.when(i < n_valid)
    def _():
        cur = i % 2
        pltpu.make_async_copy(h_hbm.at[pl.ds(0, DISPATCH_ROWS), :], buf.at[cur], sem.at[cur]).wait()
        xs_ref[...] = buf[cur]

    @pl.when(i >= n_valid)
    def _():
        xs_ref[...] = jnp.zeros_like(xs_ref)


def _dispatch(slot_tok, n_valid, h):
    n_slots = slot_tok.shape[0]
    d = h.shape[1]
    grid_spec = pltpu.PrefetchScalarGridSpec(
        num_scalar_prefetch=2,
        grid=(n_slots // DISPATCH_ROWS,),
        in_specs=[pl.BlockSpec(memory_space=pl.ANY)],
        out_specs=pl.BlockSpec((DISPATCH_ROWS, d), lambda i, t, nv: (i, 0)),
        scratch_shapes=[pltpu.VMEM((2, DISPATCH_ROWS, d), h.dtype), pltpu.SemaphoreType.DMA((2,))],
    )
    return pl.pallas_call(
        _dispatch_body,
        grid_spec=grid_spec,
        out_shape=jax.ShapeDtypeStruct((n_slots, d), h.dtype),
        compiler_params=pltpu.CompilerParams(dimension_semantics=("arbitrary",), vmem_limit_bytes=VMEM_LIMIT_BYTES),
        name="dispatch",
    )(slot_tok, n_valid, h)


def _combine_body(slot_ref, y_hbm, p_ref, x1_ref, mod_ref, g_ref, b_ref, o_ref, buf, sem, *, n_tiles):
    tt = COMBINE_TILE
    d = o_ref.shape[-1]
    i = pl.program_id(0)

    def row_copy(src_row, dst_row, which):
        return pltpu.make_async_copy(y_hbm.at[pl.ds(src_row, 1), :], buf.at[which, pl.ds(dst_row, 1), :],
                                     sem.at[which])

    def issue(tile, which):
        base = tile * (tt * TOP_K)

        def body(t, carry):
            for k in range(TOP_K):
                row_copy(slot_ref[base + t * TOP_K + k], k * tt + t, which).start(priority=k % 2)
            return carry
        lax.fori_loop(0, tt, body, 0)

    @pl.when(i == 0)
    def _():
        issue(0, 0)

    @pl.when(i + 1 < n_tiles)
    def _():
        issue(i + 1, (i + 1) % 2)

    cur = i % 2
    pltpu.make_async_copy(y_hbm.at[pl.ds(0, TOP_K * tt), :], buf.at[cur], sem.at[cur]).wait()
    p = p_ref[...]
    ff = p[:, 0:1] * buf[cur, 0:tt, :]
    for k in range(1, TOP_K):
        ff = ff + p[:, k:k + 1] * buf[cur, k * tt:(k + 1) * tt, :]
    o_ref[...] = _ln_rows(DEEPNORM_ALPHA * x1_ref[...] + mod_ref[:, 5 * d:6 * d] * ff, g_ref[...], b_ref[...])


def _combine_norm(slot, y, top_p, x1, mod, ln_g, ln_b, *, n_prompt, t_sample):
    n, d = x1.shape
    tt = COMBINE_TILE
    n_tiles = n // tt
    grid_spec = pltpu.PrefetchScalarGridSpec(
        num_scalar_prefetch=1,
        grid=(n_tiles,),
        in_specs=[pl.BlockSpec(memory_space=pl.ANY),
                  pl.BlockSpec((tt, TOP_K), lambda i, s: (i, 0)),
                  pl.BlockSpec((tt, d), lambda i, s: (i, 0)),
                  pl.BlockSpec((None, 1, 6 * d), lambda i, s: (_group_of_tile(i, tt, n_prompt, t_sample), 0, 0)),
                  pl.BlockSpec((1, d), lambda i, s: (0, 0)),
                  pl.BlockSpec((1, d), lambda i, s: (0, 0))],
        out_specs=pl.BlockSpec((tt, d), lambda i, s: (i, 0)),
        scratch_shapes=[pltpu.VMEM((2, TOP_K * tt, d), F32), pltpu.SemaphoreType.DMA((2,))],
    )
    return pl.pallas_call(
        functools.partial(_combine_body, n_tiles=n_tiles),
        grid_spec=grid_spec,
        out_shape=jax.ShapeDtypeStruct((n, d), F32),
        compiler_params=pltpu.CompilerParams(dimension_semantics=("arbitrary",), vmem_limit_bytes=VMEM_LIMIT_BYTES),
        name="combine_norm",
    )(slot, y, top_p, x1, mod, ln_g.reshape(1, d), ln_b.reshape(1, d))


def _grid_pos_embed(rows, dim):
    quarter = dim // 4
    freqs = 1.0 / (POS_BASE ** (jnp.arange(quarter, dtype=F32) / quarter))

    def axis_embed(n):
        ang = jnp.arange(n, dtype=F32)[:, None] * freqs[None, :]
        return jnp.concatenate([jnp.sin(ang), jnp.cos(ang)], axis=-1)
    er = axis_embed(rows)
    ec = axis_embed(GRID_W)
    pos = jnp.concatenate([jnp.broadcast_to(er[:, None, :], (rows, GRID_W, dim // 2)),
                           jnp.broadcast_to(ec[None, :, :], (rows, GRID_W, dim // 2))], axis=-1)
    return pos.reshape(rows * GRID_W, dim)


def kernel(x_prompt, x_sample, c, state_fwd, state_bwd, c_ctx, w_mod, b_mod, w_in, conv_qkv, a_log, dt_bias,
           norm_o, w_a_out, conv_b, w_b_out, w_o, ln1_g, ln1_b, router_w, router_b, w_gu, b_gu, w_down, b_down,
           ln2_g, ln2_b):
    nb_p, t_p, d = x_prompt.shape
    nb_s, t_s, _ = x_sample.shape
    n_p = nb_p * t_p
    n_s = nb_s * t_s
    assert n_p == PREP_ROWS and t_s == PREP_ROWS
    xp = x_prompt.reshape(n_p, d)
    xs = x_sample.reshape(n_s, d)
    pos = _grid_pos_embed(t_s // GRID_W, d)
    offs = tuple(int(o) for o in np.cumsum(IN_SIZES)[:-1])

    l = 0
    cvec = jnp.concatenate([c_ctx[None, :], c], axis=0)
    cpad = jnp.zeros((8, d), F32).at[:1 + nb_s].set(jax.nn.silu(cvec))
    mod = (_mm(cpad, w_mod[l], tm=8, tn=512)[:1 + nb_s] + b_mod[l])[:, None, :]

    h = _modulate(xp, xs, pos, mod, t_sample=t_s)
    proj = _mm(h, w_in[l], tm=2048, tn=384)

    qkvn = _qkv_prep(proj, conv_qkv[l], t_first=t_p, t_rest=t_s)
    cols, rws = _decay_tables(proj[:, offs[1]:offs[3]], a_log[l], dt_bias[l])
    zero_state = jnp.zeros((PROMPT_SUB, NV_HEADS, DK, DV), F32)
    o_all = jnp.zeros((n_p + n_s, V_W), BF16)
    o_all, sf, sb = _deltanet(qkvn, proj, cols, rws, norm_o[l], zero_state, zero_state, o_all,
                              n_seq=nb_p, t_seq=t_p, n_sub=PROMPT_SUB, row_blk0=0, zero_init=True)
    o_all, _, _ = _deltanet(qkvn, proj, cols, rws, norm_o[l], state_fwd[:, l], state_bwd[:, l], o_all,
                            n_seq=nb_s, t_seq=t_s, n_sub=1, row_blk0=n_p // t_s, zero_init=False)

    gconv = _gated_conv(proj, conv_b[l], offs[3], t_first=t_p, t_rest=t_s)
    m = _branch_merge(o_all, gconv, w_a_out[l], w_b_out[l], proj, offs[6], offs[7])
    x1, h2, logits = _mix_norm_route(m, w_o[l].astype(BF16), xp, xs, pos, mod, ln1_g[l], ln1_b[l],
                                     router_w[l], router_b[l], t_sample=t_s)
    slot, slot_tok, top_p, block_e, n_valid = _route(logits)
    y = _moe_blocks(_dispatch(slot_tok, n_valid, h2), block_e, n_valid, w_gu[l], b_gu[l], w_down[l], b_down[l])
    x2 = _combine_norm(slot, y, top_p, x1, mod, ln2_g[l], ln2_b[l], n_prompt=n_p, t_sample=t_s)
    y_prompt = x2[:n_p].reshape(nb_p, t_p, d)
    y_sample = x2[n_p:].reshape(nb_s, t_s, d)
    return (y_prompt, y_sample, sf[:, None], sb[:, None])
```

```python
import functools

import jax
import jax.numpy as jnp
import numpy as np
from jax import lax
from jax.experimental import pallas as pl
from jax.experimental.pallas import tpu as pltpu

F32 = jnp.float32
BF16 = jnp.bfloat16

D_MODEL = 2048
GRID_W = 64
NK_HEADS = 16
NV_HEADS = 32
DK = 128
DV = 128
QK_W = NK_HEADS * DK
V_W = NV_HEADS * DV
QKV_W = 2 * QK_W + V_W
CHUNK = 64
CONV_CH = D_MODEL
IN_SIZES = (QKV_W, V_W, 2 * NV_HEADS, 2 * NV_HEADS, CONV_CH, CONV_CH, CONV_CH, D_MODEL, D_MODEL)
N_EXPERTS = 32
TOP_K = 4
D_FF = D_MODEL
SWIGLU_LIMIT = 7.0
SWIGLU_ALPHA = 1.702
DEPTH = 1
DEEPNORM_ALPHA = (2 * DEPTH) ** 0.25
EPS = 1e-6
POS_BASE = 10000.0

VMEM_LIMIT_BYTES = 56 * 1024 * 1024


def _mm_body(x_ref, w_ref, o_ref):
    o_ref[...] = jnp.dot(x_ref[...].astype(BF16), w_ref[...].astype(BF16), preferred_element_type=F32)


def _mm(x, w, *, tm, tn):
    m, k = x.shape
    n = w.shape[1]
    assert m % tm == 0
    return pl.pallas_call(
        _mm_body,
        name="matmul",
        grid=(m // tm, pl.cdiv(n, tn)),
        in_specs=[pl.BlockSpec((tm, k), lambda i, j: (i, 0)),
                  pl.BlockSpec((k, tn), lambda i, j: (0, j))],
        out_specs=pl.BlockSpec((tm, tn), lambda i, j: (i, j)),
        out_shape=jax.ShapeDtypeStruct((m, n), F32),
        compiler_params=pltpu.CompilerParams(dimension_semantics=("parallel", "parallel"),
                                             vmem_limit_bytes=VMEM_LIMIT_BYTES),
    )(x, w)


MOE_TM = 512
MOE_TF = 512


def _pack_bf16_pairs(x):
    n = x.shape[1] // 2
    lo = lax.bitcast_convert_type(x[:, :n].astype(BF16).astype(F32), jnp.uint32)
    hi = lax.bitcast_convert_type(x[:, n:].astype(BF16).astype(F32), jnp.uint32)
    return (lo >> 16) | hi


def _unpack_bf16_pairs(w):
    lo = lax.bitcast_convert_type(w << 16, F32)
    hi = lax.bitcast_convert_type(w & jnp.uint32(0xFFFF0000), F32)
    return jnp.concatenate([lo, hi], axis=1).astype(BF16)


def _moe_body(be_ref, nv_ref, tok_ref, h_hbm, wg_ref, wu_ref, bg_ref, bu_ref, wd_ref, bd_ref, o_ref,
              acc_ref, xbuf, sem):
    i = pl.program_id(0)
    f = pl.program_id(1)
    n_valid = nv_ref[0]
    valid = i < n_valid

    def gather(block):
        which = block % 2
        base = block * MOE_TM

        def body(pair, carry):
            for u in range(2):
                r = 2 * pair + u
                pltpu.make_async_copy(h_hbm.at[pl.ds(tok_ref[base + r], 1), :], xbuf.at[which, pl.ds(r, 1), :],
                                      sem.at[which]).start()
            return carry
        lax.fori_loop(0, MOE_TM // 2, body, 0)

    @pl.when(jnp.logical_and(jnp.logical_and(i == 0, f == 0), valid))
    def _():
        gather(0)

    @pl.when(jnp.logical_and(f == 1, i + 1 < n_valid))
    def _():
        gather(i + 1)

    @pl.when(f == 0)
    def _():
        acc_ref[...] = jnp.zeros_like(acc_ref)

    @pl.when(jnp.logical_and(f == 0, valid))
    def _():
        pltpu.make_async_copy(h_hbm.at[pl.ds(0, MOE_TM), :], xbuf.at[i % 2], sem.at[i % 2]).wait()

    @pl.when(valid)
    def _():
        x = _unpack_bf16_pairs(xbuf[i % 2])
        g = jnp.dot(x, wg_ref[...].astype(BF16), preferred_element_type=F32) + bg_ref[...]
        u = jnp.dot(x, wu_ref[...].astype(BF16), preferred_element_type=F32) + bu_ref[...]
        gate = jnp.minimum(g, SWIGLU_LIMIT)
        up = jnp.clip(u, -SWIGLU_LIMIT, SWIGLU_LIMIT)
        act = (up + 1.0) * gate * jax.nn.sigmoid(SWIGLU_ALPHA * gate)
        acc_ref[...] += jnp.dot(act.astype(BF16), wd_ref[...].astype(BF16), preferred_element_type=F32)

    @pl.when(f == pl.num_programs(1) - 1)
    def _():
        o_ref[...] = jnp.where(valid, acc_ref[...] + bd_ref[...], 0.0)


def _moe_blocks(h_packed, slot_tok, block_e, n_valid, w_gu, b_gu, w_down, b_down):
    n_slots = slot_tok.shape[0]
    d = w_down.shape[-1]
    n_blocks = n_slots // MOE_TM
    nf = D_FF // MOE_TF

    def f_eff(i, f, nv):
        return jnp.where(i < nv[0], f, nf - 1)

    grid_spec = pltpu.PrefetchScalarGridSpec(
        num_scalar_prefetch=3,
        grid=(n_blocks, nf),
        in_specs=[
            pl.BlockSpec(memory_space=pl.ANY),
            pl.BlockSpec((None, d, MOE_TF), lambda i, f, be, nv, tok: (be[i], 0, f_eff(i, f, nv))),
            pl.BlockSpec((None, d, MOE_TF), lambda i, f, be, nv, tok: (be[i], 0, nf + f_eff(i, f, nv))),
            pl.BlockSpec((None, 1, MOE_TF), lambda i, f, be, nv, tok: (be[i], 0, f_eff(i, f, nv))),
            pl.BlockSpec((None, 1, MOE_TF), lambda i, f, be, nv, tok: (be[i], 0, nf + f_eff(i, f, nv))),
            pl.BlockSpec((None, MOE_TF, d), lambda i, f, be, nv, tok: (be[i], f_eff(i, f, nv), 0)),
            pl.BlockSpec((None, 1, d), lambda i, f, be, nv, tok: (be[i], 0, 0)),
        ],
        out_specs=pl.BlockSpec((MOE_TM, d), lambda i, f, be, nv, tok: (i, 0)),
        scratch_shapes=[pltpu.VMEM((MOE_TM, d), F32), pltpu.VMEM((2, MOE_TM, h_packed.shape[1]), h_packed.dtype),
                        pltpu.SemaphoreType.DMA((2,))],
    )
    return pl.pallas_call(
        _moe_body,
        name="moe_experts",
        grid_spec=grid_spec,
        out_shape=jax.ShapeDtypeStruct((n_slots, d), F32),
        compiler_params=pltpu.CompilerParams(dimension_semantics=("arbitrary", "arbitrary"),
                                             vmem_limit_bytes=VMEM_LIMIT_BYTES),
    )(block_e, n_valid, slot_tok, h_packed, w_gu, w_gu, b_gu[:, None, :], b_gu[:, None, :], w_down,
      b_down[:, None, :])


def _route(logits):
    n_tok = logits.shape[0]
    n_assign = n_tok * TOP_K
    top_logit, top_e = lax.top_k(logits, TOP_K)
    top_p = jax.nn.softmax(top_logit, axis=-1)
    flat_e = top_e.reshape(-1)
    onehot = (flat_e[:, None] == jnp.arange(N_EXPERTS)[None, :]).astype(jnp.int32)
    csum = jnp.cumsum(onehot, axis=0)
    counts = csum[-1]
    rank = jnp.take_along_axis(csum, flat_e[:, None], axis=1)[:, 0] - 1
    padded = (counts + MOE_TM - 1) // MOE_TM * MOE_TM
    pad_end = jnp.cumsum(padded)
    pad_start = pad_end - padded
    slot = (pad_start[flat_e] + rank).astype(jnp.int32)
    n_blocks = -(-(n_assign + N_EXPERTS * (MOE_TM - 1)) // MOE_TM)
    n_slots = n_blocks * MOE_TM
    slot_tok = jnp.zeros((n_slots,), jnp.int32).at[slot].set(jnp.arange(n_assign, dtype=jnp.int32) // TOP_K)
    n_valid = (pad_end[-1] // MOE_TM).astype(jnp.int32)
    blk = jnp.arange(n_blocks, dtype=jnp.int32)
    block_e = jnp.searchsorted(pad_end, jnp.minimum(blk, n_valid - 1) * MOE_TM, side='right').astype(jnp.int32)
    block_e = jnp.minimum(block_e, N_EXPERTS - 1)
    return slot, slot_tok, top_p, block_e, n_valid.reshape(1)


PREP_ROWS = 4096


def _prep_body(x_ref, w_ref, o_ref, *, t_first, t_rest):
    i = pl.program_id(0)
    j = pl.program_id(1)
    t_seq = jnp.where(i == 0, t_first, t_rest)
    x = x_ref[...]
    n = x.shape[0]
    row = lax.broadcasted_iota(jnp.int32, x.shape, 0)
    tpos = row & (t_seq - 1)
    xm = jnp.where(tpos == 0, 0.0, pltpu.roll(x, 1, axis=0))
    xp = jnp.where(tpos == t_seq - 1, 0.0, pltpu.roll(x, n - 1, axis=0))
    w = w_ref[...]
    y = xm * w[0:1] + x * w[1:2] + xp * w[2:3]
    y = y * jax.nn.sigmoid(y)
    nq = QK_W // DK

    @pl.when(j < 2 * nq)
    def _():
        rs = lax.rsqrt(jnp.sum(y * y, axis=-1, keepdims=True) + EPS)
        o_ref[...] = (y * (rs * jnp.where(j < nq, DK ** -0.5, 1.0))).astype(o_ref.dtype)

    @pl.when(j >= 2 * nq)
    def _():
        o_ref[...] = y.astype(o_ref.dtype)


def _qkv_prep(proj, conv_w, *, t_first, t_rest):
    n = proj.shape[0]
    return pl.pallas_call(
        functools.partial(_prep_body, t_first=t_first, t_rest=t_rest),
        name="qkv_prep",
        grid=(n // PREP_ROWS, QKV_W // DK),
        in_specs=[pl.BlockSpec((PREP_ROWS, DK), lambda i, j: (i, j)),
                  pl.BlockSpec((3, DK), lambda i, j: (0, j))],
        out_specs=pl.BlockSpec((PREP_ROWS, DK), lambda i, j: (i, j)),
        out_shape=jax.ShapeDtypeStruct((n, QKV_W), BF16),
        compiler_params=pltpu.CompilerParams(dimension_semantics=("parallel", "parallel"),
                                             vmem_limit_bytes=VMEM_LIMIT_BYTES),
    )(proj, conv_w)


PROMPT_SUB = 2
SOLVE_WAYS = 8


def _mmul(a, b):
    return jnp.dot(a, b, preferred_element_type=F32)


def _delta_body(q_ref, k_ref, v_ref, z_ref, cols_ref, rows_ref, nw_ref, s0f_ref, s0b_ref, o_prev_ref,
                o_ref, sf_ref, sb_ref, acc_ref, s_ref, zz_ref, pr_ref, kt_ref, *, n_sub, n_chunks, zero_init):
    del o_prev_ref
    n_all = n_sub * n_chunks
    ways = min(SOLVE_WAYS, n_all)
    lane = lax.broadcasted_iota(jnp.int32, (CHUNK, 2 * CHUNK), 1)
    row = lax.broadcasted_iota(jnp.int32, (CHUNK, 2 * CHUNK), 0)
    pos = lane & (CHUNK - 1)
    left = lane < CHUNK
    left1 = lax.broadcasted_iota(jnp.int32, (1, 2 * CHUNK), 1) < CHUNK
    eye2 = (pos == row).astype(F32)
    incl = (pos <= row, pos >= row)
    strict = (pos < row, pos > row)
    same8 = (row >> 3) == (pos >> 3)
    same16 = (row >> 4) == (pos >> 4)
    same32 = (row >> 5) == (pos >> 5)
    nt = (((1,), (1,)), ((), ()))
    tn = (((0,), (0,)), ((), ()))

    def split(m):
        return jnp.where(left, m, 0.0), jnp.where(left, 0.0, m)

    def bdiag(m):
        return jnp.concatenate(split(m), axis=0).astype(BF16)

    def pmm(a, b):
        return _mmul(a.astype(BF16), bdiag(b))

    def inv_unit_tri(lms):
        d8 = [jnp.where(same8, lm, 0.0) for lm in lms]
        xs = [eye2 - d for d in d8]
        d2 = [pmm(d, d) for d in d8]
        xs = [x + pmm(x, d) for x, d in zip(xs, d2)]
        d4 = [pmm(d, d) for d in d2]
        xs = [x + pmm(x, d) for x, d in zip(xs, d4)]
        for keep in (same16 & ~same8, same32 & ~same16, ~same32):
            ys = [pmm(jnp.where(keep, lm, 0.0), x) for lm, x in zip(lms, xs)]
            xs = [x - pmm(x, y) for x, y in zip(xs, ys)]
        return xs

    def chunk_terms(c, d):
        r0 = pl.multiple_of(c * CHUNK, CHUNK)
        cols = cols_ref[pl.ds(r0, CHUNK), :]
        rows = rows_ref[c]

        def lanes_of(col):
            return jnp.broadcast_to(cols[:, col:col + 1], (CHUNK, 2 * CHUNK))
        gcol = [lanes_of(2 * d + h) for h in (0, 1)]
        bcol = [lanes_of(4 + 2 * d + h) for h in (0, 1)]
        gl = [rows[4 + 2 * d + h:5 + 2 * d + h] for h in (0, 1)]
        return r0, rows, gcol, bcol, gl

    acc_ref[...] = jnp.zeros_like(acc_ref)
    rep = s0f_ref.shape[1]
    if zero_init:
        s_ref[...] = jnp.zeros_like(s_ref)
    else:
        for sub in range(n_sub):
            for h in range(rep):
                s_ref[2 * rep * sub + h] = s0f_ref[sub, h]
                s_ref[2 * rep * sub + rep + h] = s0b_ref[sub, h]

    def solve(i, carry):
        chunks = [i + part * (n_all // ways) for part in range(ways)]
        qcs, kcs, vcs, kkms, qkms = [], [], [], [], []
        for c in chunks:
            r0 = pl.multiple_of(c * CHUNK, CHUNK)
            kb = k_ref[pl.ds(r0, CHUNK), :]
            qb = q_ref[pl.ds(r0, CHUNK), :]
            kk2 = jnp.concatenate([kb, kb], axis=0)
            kcs.append(kb.astype(F32))
            qcs.append(qb.astype(F32))
            vcs.append(v_ref[pl.ds(r0, CHUNK), :])
            kkms.append(lax.dot_general(kb, kk2, nt, preferred_element_type=F32))
            qkms.append(lax.dot_general(qb, kk2, nt, preferred_element_type=F32))
            kt_ref[c] = jnp.transpose(kcs[-1]).astype(BF16)
        chains = [(j, d) for j in range(ways) for d in (0, 1)]
        lms, rhs2s, a2s, decays = [], [], [], []
        for j, d in chains:
            c = chunks[j]
            _, rows, gcol, bcol, gl = chunk_terms(c, d)
            gcol2 = jnp.where(left, gcol[0], gcol[1])
            bcol2 = jnp.where(left, bcol[0], bcol[1])
            grow2 = jnp.where(left1, rows[2 * d:2 * d + 1], rows[2 * d + 1:2 * d + 2])
            dm = jnp.exp(jnp.where(incl[d], gcol2 - grow2, -jnp.inf))
            a2s.append(qkms[j] * dm)
            eg = [jnp.exp(g) for g in gcol]
            decays.append((eg, [jnp.exp(gl[h] - gcol[h]) for h in (0, 1)]))
            lms.append(jnp.where(strict[d], kkms[j] * dm * bcol2, 0.0))
            rhs = [jnp.concatenate([vcs[j][:, h * DV:(h + 1) * DV] * bcol[h], kcs[j] * (bcol[h] * eg[h])], axis=1)
                   for h in (0, 1)]
            rhs2s.append(jnp.concatenate(rhs, axis=0).astype(BF16))
        t2s = inv_unit_tri(lms)
        sols = [[_mmul(th.astype(BF16), rhs2) for th in split(t2)]
                for t2, rhs2 in zip(t2s, rhs2s)]
        aws = [[_mmul(ah.astype(BF16), jnp.concatenate(sol, axis=0).astype(BF16)) for ah in split(a2)]
               for a2, sol in zip(a2s, sols)]
        for (j, d), sol, aw, (eg, ek) in zip(chains, sols, aws, decays):
            c = chunks[j]
            zz, pr = [], []
            for h in (0, 1):
                zz += [sol[h][:, :DV] * ek[h], sol[h][:, DV:] * ek[h]]
                pr += [qcs[j] * eg[h] - aw[h][:, DV:], aw[h][:, :DV]]
            zz_ref[d, c] = jnp.concatenate(zz, axis=1).astype(BF16)
            pr_ref[d, c] = jnp.concatenate(pr, axis=1).astype(BF16)
        return carry

    lax.fori_loop(0, n_all // ways, solve, 0)

    def step(i, carry):
        lanes = [(sub, d) for sub in range(n_sub) for d in (0, 1)]
        where = [sub * n_chunks + (i if d == 0 else n_chunks - 1 - i) for sub, d in lanes]
        ngs, prs, gls = [], [], []
        for c, (_, d) in zip(where, lanes):
            rows = rows_ref[c]
            gls.append([rows[4 + 2 * d + h:5 + 2 * d + h] for h in (0, 1)])
            ngs.append(_mmul(kt_ref[c], zz_ref[d, c]))
            prs.append(pr_ref[d, c])
        streams = [(n, h) for n in range(len(lanes)) for h in (0, 1)]
        outs = []
        for n, h in streams:
            pg = jnp.concatenate([prs[n][:, 2 * h * DV:(2 * h + 1) * DV],
                                  ngs[n][:, (2 * h + 1) * DV:(2 * h + 2) * DV].astype(BF16)], axis=0)
            outs.append(_mmul(pg, s_ref[2 * n + h].astype(BF16)))
        for (n, h), out in zip(streams, outs):
            s = 2 * n + h
            r0 = pl.multiple_of(where[n] * CHUNK, CHUNK)
            acc_ref[pl.ds(r0, CHUNK), h * DV:(h + 1) * DV] += (
                out[:CHUNK] + prs[n][:, (2 * h + 1) * DV:(2 * h + 2) * DV].astype(F32))
            s_ref[s] = s_ref[s] * jnp.exp(gls[n][h]) - out[CHUNK:] + ngs[n][:, 2 * h * DV:(2 * h + 1) * DV]
        return carry

    lax.fori_loop(0, n_chunks, step, 0)

    nw = nw_ref[...]

    def finish(j, carry):
        r0 = pl.multiple_of(j * CHUNK, CHUNK)
        o = acc_ref[pl.ds(r0, CHUNK), :]
        z = z_ref[pl.ds(r0, CHUNK), :]
        for h in (0, 1):
            oh = o[:, h * DV:(h + 1) * DV]
            zh = z[:, h * DV:(h + 1) * DV]
            y = oh * lax.rsqrt(jnp.mean(oh * oh, axis=-1, keepdims=True) + EPS) * nw * (zh * jax.nn.sigmoid(zh))
            o_ref[pl.ds(r0, CHUNK), h * DV:(h + 1) * DV] = y.astype(o_ref.dtype)
        return carry

    lax.fori_loop(0, n_all, finish, 0)
    for sub in range(n_sub):
        for h in range(rep):
            sf_ref[sub, h] = s_ref[2 * rep * sub + h]
            sb_ref[sub, h] = s_ref[2 * rep * sub + rep + h]


def _deltanet(qkvn, proj, cols, rows, norm_w, s0f, s0b, o_prev, *, n_seq, t_seq, n_sub, row_blk0, zero_init):
    n_chunks = t_seq // CHUNK
    t_blk = n_sub * t_seq
    c_blk = n_sub * n_chunks
    nq = QK_W // DK
    z_blk0 = QKV_W // (2 * DV)
    rep = NV_HEADS // NK_HEADS
    state_spec = pl.BlockSpec((n_sub, rep, DK, DV), lambda b, kh: (b, kh, 0, 0))
    in_specs = [
        pl.BlockSpec((t_blk, DK), lambda b, kh: (row_blk0 + b, kh)),
        pl.BlockSpec((t_blk, DK), lambda b, kh: (row_blk0 + b, nq + kh)),
        pl.BlockSpec((t_blk, rep * DV), lambda b, kh: (row_blk0 + b, nq + kh)),
        pl.BlockSpec((t_blk, rep * DV), lambda b, kh: (row_blk0 + b, z_blk0 + kh)),
        pl.BlockSpec((None, t_blk, 8), lambda b, kh: (kh, row_blk0 + b, 0)),
        pl.BlockSpec((None, c_blk, 8, 2 * CHUNK), lambda b, kh: (kh, row_blk0 + b, 0, 0)),
        pl.BlockSpec((1, DV), lambda b, kh: (0, 0)),
        pl.BlockSpec((n_sub, rep, DK, DV), lambda b, kh: (0 if zero_init else b, kh, 0, 0)),
        pl.BlockSpec((n_sub, rep, DK, DV), lambda b, kh: (0 if zero_init else b, kh, 0, 0)),
        pl.BlockSpec(memory_space=pl.ANY),
    ]
    out_specs = [pl.BlockSpec((t_blk, rep * DV), lambda b, kh: (row_blk0 + b, kh)), state_spec, state_spec]
    out_shape = [
        jax.ShapeDtypeStruct((qkvn.shape[0], V_W), BF16),
        jax.ShapeDtypeStruct((n_seq, NV_HEADS, DK, DV), F32),
        jax.ShapeDtypeStruct((n_seq, NV_HEADS, DK, DV), F32),
    ]
    args = [qkvn, qkvn, qkvn, proj, cols, rows, norm_w.reshape(1, DV), s0f, s0b, o_prev]
    return pl.pallas_call(
        functools.partial(_delta_body, n_sub=n_sub, n_chunks=n_chunks, zero_init=zero_init),
        grid=(n_seq // n_sub, NK_HEADS),
        input_output_aliases={len(args) - 1: 0},
        name="deltanet",
        in_specs=in_specs,
        out_specs=out_specs,
        out_shape=out_shape,
        scratch_shapes=[pltpu.VMEM((t_blk, rep * DV), F32), pltpu.VMEM((2 * rep * n_sub, DK, DV), F32),
                        pltpu.VMEM((2, c_blk, CHUNK, 2 * rep * DV), BF16),
                        pltpu.VMEM((2, c_blk, CHUNK, 2 * rep * DV), BF16),
                        pltpu.VMEM((c_blk, DK, CHUNK), BF16)],
        compiler_params=pltpu.CompilerParams(dimension_semantics=("parallel", "parallel"),
                                             vmem_limit_bytes=VMEM_LIMIT_BYTES),
    )(*args)


def _decay_tables(ab, a_log, dt_bias):
    n = ab.shape[0]
    nh2 = 2 * NV_HEADS
    rep = NV_HEADS // NK_HEADS
    beta = jax.nn.sigmoid(ab[:, :nh2])
    g = -jnp.exp(a_log.reshape(nh2)) * jax.nn.softplus(ab[:, nh2:] + dt_bias.reshape(nh2))
    gch = g.reshape(n // CHUNK, CHUNK, 2, NV_HEADS)
    gcf = jnp.cumsum(gch[:, :, 0], axis=1)
    gcb = jnp.flip(jnp.cumsum(jnp.flip(gch[:, :, 1], 1), axis=1), 1)
    gc = jnp.stack([gcf, gcb], axis=2)
    gl = jnp.stack([gcf[:, -1], gcb[:, 0]], axis=1)

    def per_kh(a):
        a = a.reshape(a.shape[:-2] + (2, NK_HEADS, rep))
        return jnp.moveaxis(a, -3, -2).reshape(a.shape[:-3] + (NK_HEADS, 2 * rep))
    gc_k = per_kh(gc)
    cols = jnp.concatenate([gc_k.reshape(n, NK_HEADS, 2 * rep), per_kh(beta.reshape(n, 2, NV_HEADS))], axis=-1)
    cols = jnp.transpose(cols, (1, 0, 2))
    gr = jnp.transpose(gc_k, (2, 0, 3, 1))
    gr = jnp.concatenate([gr, gr], axis=-1)
    glr = jnp.broadcast_to(jnp.transpose(per_kh(gl), (1, 0, 2))[..., None], gr.shape)
    rows = jnp.concatenate([gr, glr], axis=2)
    return cols, rows


TOK_TILE = 512
MIX_TILE = 256
COMBINE_TILE = 128


def _group_of_tile(i, tile, n_prompt, t_sample):
    first = n_prompt // tile
    return jnp.where(i < first, 0, 1 + (i - first) // (t_sample // tile))


def _token_specs(tile, n_prompt, t_sample, d):
    first = n_prompt // tile
    per_seq = t_sample // tile
    return [
        pl.BlockSpec((tile, d), lambda i, *_: (jnp.minimum(i, first - 1), 0)),
        pl.BlockSpec((tile, d), lambda i, *_: (jnp.maximum(i - first, 0), 0)),
        pl.BlockSpec((tile, d), lambda i, *_: (jnp.maximum(i - first, 0) % per_seq, 0)),
        pl.BlockSpec((None, 1, 6 * d), lambda i, *_: (_group_of_tile(i, tile, n_prompt, t_sample), 0, 0)),
    ]


def _token_rows(i, first, xp_ref, xs_ref, pos_ref):
    return jnp.where(i < first, xp_ref[...], xs_ref[...] + pos_ref[...])


def _modulate_body(xp_ref, xs_ref, pos_ref, mod_ref, h_ref, *, first):
    d = h_ref.shape[-1]
    x = _token_rows(pl.program_id(0), first, xp_ref, xs_ref, pos_ref)
    h_ref[...] = (x * (1.0 + mod_ref[:, d:2 * d]) + mod_ref[:, 0:d]).astype(h_ref.dtype)


def _modulate(xp, xs, pos, mod, *, t_sample):
    n_p, d = xp.shape
    n = n_p + xs.shape[0]
    return pl.pallas_call(
        functools.partial(_modulate_body, first=n_p // TOK_TILE),
        grid=(n // TOK_TILE,),
        in_specs=_token_specs(TOK_TILE, n_p, t_sample, d),
        out_specs=pl.BlockSpec((TOK_TILE, d), lambda i: (i, 0)),
        out_shape=jax.ShapeDtypeStruct((n, d), BF16),
        compiler_params=pltpu.CompilerParams(dimension_semantics=("parallel",), vmem_limit_bytes=VMEM_LIMIT_BYTES),
        name="modulate",
    )(xp, xs, pos, mod)


def _gconv_body(ub_ref, uc_ref, ux_ref, w_ref, o_ref, *, t_first, t_rest):
    t_seq = jnp.where(pl.program_id(0) == 0, t_first, t_rest)
    x = uc_ref[...] * ux_ref[...]
    n = x.shape[0]
    tpos = lax.broadcasted_iota(jnp.int32, x.shape, 0) & (t_seq - 1)
    xm = jnp.where(tpos == 0, 0.0, pltpu.roll(x, 1, axis=0))
    xp = jnp.where(tpos == t_seq - 1, 0.0, pltpu.roll(x, n - 1, axis=0))
    w = w_ref[...]
    o_ref[...] = (ub_ref[...] * (xm * w[0:1] + x * w[1:2] + xp * w[2:3])).astype(o_ref.dtype)


def _gated_conv(proj, conv_w, col0, *, t_first, t_rest):
    n = proj.shape[0]
    nblk = CONV_CH // DK
    b0 = col0 // DK
    return pl.pallas_call(
        functools.partial(_gconv_body, t_first=t_first, t_rest=t_rest),
        grid=(n // PREP_ROWS, nblk),
        in_specs=[pl.BlockSpec((PREP_ROWS, DK), lambda i, j: (i, b0 + j)),
                  pl.BlockSpec((PREP_ROWS, DK), lambda i, j: (i, b0 + nblk + j)),
                  pl.BlockSpec((PREP_ROWS, DK), lambda i, j: (i, b0 + 2 * nblk + j)),
                  pl.BlockSpec((3, DK), lambda i, j: (0, j))],
        out_specs=pl.BlockSpec((PREP_ROWS, DK), lambda i, j: (i, j)),
        out_shape=jax.ShapeDtypeStruct((n, CONV_CH), BF16),
        compiler_params=pltpu.CompilerParams(dimension_semantics=("parallel", "parallel"),
                                             vmem_limit_bytes=VMEM_LIMIT_BYTES),
        name="gated_conv",
    )(proj, proj, proj, conv_w)


BR_TM = 512
BR_TN = 512


def _branch_body(o_ref, g_ref, wa_ref, wb_ref, ra_ref, rb_ref, m_ref, wa_bf, wb_bf):
    @pl.when(pl.program_id(1) == 0)
    def _():
        wa_bf[...] = wa_ref[...].astype(BF16)
        wb_bf[...] = wb_ref[...].astype(BF16)
    ya = jnp.dot(o_ref[...], wa_bf[...], preferred_element_type=F32)
    yb = jnp.dot(g_ref[...], wb_bf[...], preferred_element_type=F32)
    m_ref[...] = (jax.nn.sigmoid(ra_ref[...]) * ya + jax.nn.sigmoid(rb_ref[...]) * yb).astype(m_ref.dtype)


def _branch_merge(o, g, w_a, w_b, proj, ra_col, rb_col):
    n = o.shape[0]
    d = w_a.shape[1]
    el = pl.Element
    return pl.pallas_call(
        _branch_body,
        grid=(d // BR_TN, n // BR_TM),
        in_specs=[pl.BlockSpec((BR_TM, o.shape[1]), lambda j, i: (i, 0)),
                  pl.BlockSpec((BR_TM, g.shape[1]), lambda j, i: (i, 0)),
                  pl.BlockSpec((w_a.shape[0], BR_TN), lambda j, i: (0, j)),
                  pl.BlockSpec((w_b.shape[0], BR_TN), lambda j, i: (0, j)),
                  pl.BlockSpec((el(BR_TM), el(BR_TN)),
                               lambda j, i: (i * BR_TM, pl.multiple_of(ra_col + j * BR_TN, DK))),
                  pl.BlockSpec((el(BR_TM), el(BR_TN)),
                               lambda j, i: (i * BR_TM, pl.multiple_of(rb_col + j * BR_TN, DK)))],
        out_specs=pl.BlockSpec((BR_TM, BR_TN), lambda j, i: (i, j)),
        out_shape=jax.ShapeDtypeStruct((n, d), BF16),
        scratch_shapes=[pltpu.VMEM((w_a.shape[0], BR_TN), BF16), pltpu.VMEM((w_b.shape[0], BR_TN), BF16)],
        compiler_params=pltpu.CompilerParams(dimension_semantics=("arbitrary", "arbitrary"),
                                             vmem_limit_bytes=VMEM_LIMIT_BYTES),
        name="branch_merge",
    )(o, g, w_a, w_b, proj, proj)


def _ln_rows(v, g, b):
    mu = jnp.mean(v, axis=-1, keepdims=True)
    c = v - mu
    return c * lax.rsqrt(jnp.mean(c * c, axis=-1, keepdims=True) + EPS) * g + b


def _mix_body(m_ref, wo_ref, xp_ref, xs_ref, pos_ref, mod_ref, g_ref, b_ref, rw_ref, rb_ref,
              x1_ref, h2_ref, lg_ref, *, first):
    d = x1_ref.shape[-1]
    x = _token_rows(pl.program_id(0), first, xp_ref, xs_ref, pos_ref)
    mixed = jnp.dot(m_ref[...], wo_ref[...], preferred_element_type=F32)
    x1 = _ln_rows(DEEPNORM_ALPHA * x + mod_ref[:, 2 * d:3 * d] * mixed, g_ref[...], b_ref[...])
    h2 = x1 * (1.0 + mod_ref[:, 4 * d:5 * d]) + mod_ref[:, 3 * d:4 * d]
    x1_ref[...] = x1
    h2_ref[...] = _pack_bf16_pairs(h2)
    lg_ref[...] = jnp.dot(h2, rw_ref[...], precision=lax.Precision.HIGHEST, preferred_element_type=F32) + rb_ref[...]


def _mix_norm_route(m, w_o_bf, xp, xs, pos, mod, ln_g, ln_b, router_w, router_b, *, t_sample):
    n, d = m.shape
    n_p = xp.shape[0]
    ne = router_w.shape[1]
    const = lambda i: (0, 0)
    return pl.pallas_call(
        functools.partial(_mix_body, first=n_p // MIX_TILE),
        grid=(n // MIX_TILE,),
        in_specs=[pl.BlockSpec((MIX_TILE, d), lambda i: (i, 0)), pl.BlockSpec((d, d), const)]
        + _token_specs(MIX_TILE, n_p, t_sample, d)
        + [pl.BlockSpec((1, d), const), pl.BlockSpec((1, d), const),
           pl.BlockSpec((d, ne), const), pl.BlockSpec((1, ne), const)],
        out_specs=[pl.BlockSpec((MIX_TILE, d), lambda i: (i, 0)), pl.BlockSpec((MIX_TILE, d // 2), lambda i: (i, 0)),
                   pl.BlockSpec((MIX_TILE, ne), lambda i: (i, 0))],
        out_shape=[jax.ShapeDtypeStruct((n, d), F32), jax.ShapeDtypeStruct((n, d // 2), jnp.uint32),
                   jax.ShapeDtypeStruct((n, ne), F32)],
        compiler_params=pltpu.CompilerParams(dimension_semantics=("parallel",), vmem_limit_bytes=VMEM_LIMIT_BYTES),
        name="mix_norm_route",
    )(m, w_o_bf, xp, xs, pos, mod, ln_g.reshape(1, d), ln_b.reshape(1, d), router_w, router_b.reshape(1, ne))


def _combine_body(slot_ref, y_hbm, p_ref, x1_ref, mod_ref, g_ref, b_ref, o_ref, buf, sem, *, n_tiles):
    tt = COMBINE_TILE
    d = o_ref.shape[-1]
    i = pl.program_id(0)

    def row_copy(src_row, dst_row, which):
        return pltpu.make_async_copy(y_hbm.at[pl.ds(src_row, 1), :], buf.at[which, pl.ds(dst_row, 1), :],
                                     sem.at[which])

    def issue(tile, which):
        base = tile * (tt * TOP_K)

        def body(t, carry):
            for k in range(TOP_K):
                row_copy(slot_ref[base + t * TOP_K + k], k * tt + t, which).start(priority=k % 2)
            return carry
        lax.fori_loop(0, tt, body, 0)

    @pl.when(i == 0)
    def _():
        issue(0, 0)

    @pl.when(i + 1 < n_tiles)
    def _():
        issue(i + 1, (i + 1) % 2)

    cur = i % 2
    pltpu.make_async_copy(y_hbm.at[pl.ds(0, TOP_K * tt), :], buf.at[cur], sem.at[cur]).wait()
    p = p_ref[...]
    ff = p[:, 0:1] * buf[cur, 0:tt, :]
    for k in range(1, TOP_K):
        ff = ff + p[:, k:k + 1] * buf[cur, k * tt:(k + 1) * tt, :]
    o_ref[...] = _ln_rows(DEEPNORM_ALPHA * x1_ref[...] + mod_ref[:, 5 * d:6 * d] * ff, g_ref[...], b_ref[...])


def _combine_norm(slot, y, top_p, x1, mod, ln_g, ln_b, *, n_prompt, t_sample):
    n, d = x1.shape
    tt = COMBINE_TILE
    n_tiles = n // tt
    grid_spec = pltpu.PrefetchScalarGridSpec(
        num_scalar_prefetch=1,
        grid=(n_tiles,),
        in_specs=[pl.BlockSpec(memory_space=pl.ANY),
                  pl.BlockSpec((tt, TOP_K), lambda i, s: (i, 0)),
                  pl.BlockSpec((tt, d), lambda i, s: (i, 0)),
                  pl.BlockSpec((None, 1, 6 * d), lambda i, s: (_group_of_tile(i, tt, n_prompt, t_sample), 0, 0)),
                  pl.BlockSpec((1, d), lambda i, s: (0, 0)),
                  pl.BlockSpec((1, d), lambda i, s: (0, 0))],
        out_specs=pl.BlockSpec((tt, d), lambda i, s: (i, 0)),
        scratch_shapes=[pltpu.VMEM((2, TOP_K * tt, d), F32), pltpu.SemaphoreType.DMA((2,))],
    )
    return pl.pallas_call(
        functools.partial(_combine_body, n_tiles=n_tiles),
        grid_spec=grid_spec,
        out_shape=jax.ShapeDtypeStruct((n, d), F32),
        compiler_params=pltpu.CompilerParams(dimension_semantics=("arbitrary",), vmem_limit_bytes=VMEM_LIMIT_BYTES),
        name="combine_norm",
    )(slot, y, top_p, x1, mod, ln_g.reshape(1, d), ln_b.reshape(1, d))


def _grid_pos_embed(rows, dim):
    quarter = dim // 4
    freqs = 1.0 / (POS_BASE ** (jnp.arange(quarter, dtype=F32) / quarter))

    def axis_embed(n):
        ang = jnp.arange(n, dtype=F32)[:, None] * freqs[None, :]
        return jnp.concatenate([jnp.sin(ang), jnp.cos(ang)], axis=-1)
    er = axis_embed(rows)
    ec = axis_embed(GRID_W)
    pos = jnp.concatenate([jnp.broadcast_to(er[:, None, :], (rows, GRID_W, dim // 2)),
                           jnp.broadcast_to(ec[None, :, :], (rows, GRID_W, dim // 2))], axis=-1)
    return pos.reshape(rows * GRID_W, dim)


def kernel(x_prompt, x_sample, c, state_fwd, state_bwd, c_ctx, w_mod, b_mod, w_in, conv_qkv, a_log, dt_bias,
           norm_o, w_a_out, conv_b, w_b_out, w_o, ln1_g, ln1_b, router_w, router_b, w_gu, b_gu, w_down, b_down,
           ln2_g, ln2_b):
    nb_p, t_p, d = x_prompt.shape
    nb_s, t_s, _ = x_sample.shape
    n_p = nb_p * t_p
    n_s = nb_s * t_s
    assert n_p == PREP_ROWS and t_s == PREP_ROWS
    xp = x_prompt.reshape(n_p, d)
    xs = x_sample.reshape(n_s, d)
    pos = _grid_pos_embed(t_s // GRID_W, d)
    offs = tuple(int(o) for o in np.cumsum(IN_SIZES)[:-1])

    l = 0
    cvec = jnp.concatenate([c_ctx[None, :], c], axis=0)
    cpad = jnp.zeros((8, d), F32).at[:1 + nb_s].set(jax.nn.silu(cvec))
    mod = (_mm(cpad, w_mod[l], tm=8, tn=512)[:1 + nb_s] + b_mod[l])[:, None, :]

    h = _modulate(xp, xs, pos, mod, t_sample=t_s)
    proj = _mm(h, w_in[l], tm=2048, tn=512)

    qkvn = _qkv_prep(proj, conv_qkv[l], t_first=t_p, t_rest=t_s)
    cols, rws = _decay_tables(proj[:, offs[1]:offs[3]], a_log[l], dt_bias[l])
    zero_state = jnp.zeros((PROMPT_SUB, NV_HEADS, DK, DV), F32)
    o_all = jnp.zeros((n_p + n_s, V_W), BF16)
    o_all, sf, sb = _deltanet(qkvn, proj, cols, rws, norm_o[l], zero_state, zero_state, o_all,
                              n_seq=nb_p, t_seq=t_p, n_sub=PROMPT_SUB, row_blk0=0, zero_init=True)
    o_all, _, _ = _deltanet(qkvn, proj, cols, rws, norm_o[l], state_fwd[:, l], state_bwd[:, l], o_all,
                            n_seq=nb_s, t_seq=t_s, n_sub=1, row_blk0=n_p // t_s, zero_init=False)

    gconv = _gated_conv(proj, conv_b[l], offs[3], t_first=t_p, t_rest=t_s)
    m = _branch_merge(o_all, gconv, w_a_out[l], w_b_out[l], proj, offs[6], offs[7])
    x1, h2, logits = _mix_norm_route(m, w_o[l].astype(BF16), xp, xs, pos, mod, ln1_g[l], ln1_b[l],
                                     router_w[l], router_b[l], t_sample=t_s)
    slot, slot_tok, top_p, block_e, n_valid = _route(logits)
    y = _moe_blocks(h2, slot_tok, block_e, n_valid, w_gu[l], b_gu[l], w_down[l], b_down[l])
    x2 = _combine_norm(slot, y, top_p, x1, mod, ln2_g[l], ln2_b[l], n_prompt=n_p, t_sample=t_s)
    y_prompt = x2[:n_p].reshape(nb_p, t_p, d)
    y_sample = x2[n_p:].reshape(nb_s, t_s, d)
    return (y_prompt, y_sample, sf[:, None], sb[:, None])
```

```python
import functools

import jax
import jax.numpy as jnp
import numpy as np
from jax import lax
from jax.experimental import pallas as pl
from jax.experimental.pallas import tpu as pltpu

F32 = jnp.float32
BF16 = jnp.bfloat16

D_MODEL = 2048
GRID_W = 64
NK_HEADS = 16
NV_HEADS = 32
DK = 128
DV = 128
QK_W = NK_HEADS * DK
V_W = NV_HEADS * DV
QKV_W = 2 * QK_W + V_W
CHUNK = 64
CONV_CH = D_MODEL
IN_SIZES = (QKV_W, V_W, 2 * NV_HEADS, 2 * NV_HEADS, CONV_CH, CONV_CH, CONV_CH, D_MODEL, D_MODEL)
N_EXPERTS = 32
TOP_K = 4
D_FF = D_MODEL
SWIGLU_LIMIT = 7.0
SWIGLU_ALPHA = 1.702
DEPTH = 1
DEEPNORM_ALPHA = (2 * DEPTH) ** 0.25
EPS = 1e-6
POS_BASE = 10000.0

VMEM_LIMIT_BYTES = 56 * 1024 * 1024


def _mm_body(x_ref, w_ref, o_ref):
    o_ref[...] = jnp.dot(x_ref[...].astype(BF16), w_ref[...].astype(BF16), preferred_element_type=F32)


def _mm(x, w, *, tm, tn):
    m, k = x.shape
    n = w.shape[1]
    assert m % tm == 0
    return pl.pallas_call(
        _mm_body,
        name="matmul",
        grid=(m // tm, pl.cdiv(n, tn)),
        in_specs=[pl.BlockSpec((tm, k), lambda i, j: (i, 0)),
                  pl.BlockSpec((k, tn), lambda i, j: (0, j))],
        out_specs=pl.BlockSpec((tm, tn), lambda i, j: (i, j)),
        out_shape=jax.ShapeDtypeStruct((m, n), F32),
        compiler_params=pltpu.CompilerParams(dimension_semantics=("parallel", "parallel"),
                                             vmem_limit_bytes=VMEM_LIMIT_BYTES),
    )(x, w)


MOE_TM = 512
MOE_TF = 512


def _pack_bf16_pairs(x):
    n = x.shape[1] // 2
    lo = lax.bitcast_convert_type(x[:, :n].astype(BF16).astype(F32), jnp.uint32)
    hi = lax.bitcast_convert_type(x[:, n:].astype(BF16).astype(F32), jnp.uint32)
    return (lo >> 16) | hi


def _unpack_bf16_pairs(w):
    lo = lax.bitcast_convert_type(w << 16, F32)
    hi = lax.bitcast_convert_type(w & jnp.uint32(0xFFFF0000), F32)
    return jnp.concatenate([lo, hi], axis=1).astype(BF16)


def _moe_body(be_ref, nv_ref, tok_ref, h_hbm, wg_ref, wu_ref, bg_ref, bu_ref, wd_ref, bd_ref, o_ref,
              acc_ref, xbuf, sem):
    i = pl.program_id(0)
    f = pl.program_id(1)
    nf = pl.num_programs(1)
    n_valid = nv_ref[0]
    valid = i < n_valid
    part = MOE_TM // (D_FF // MOE_TF)

    def row_copy(slot_row, buf_row, which):
        return pltpu.make_async_copy(h_hbm.at[pl.ds(tok_ref[slot_row], 1), :], xbuf.at[which, pl.ds(buf_row, 1), :],
                                     sem.at[which])

    def block_wait(which):
        pltpu.make_async_copy(h_hbm.at[pl.ds(0, MOE_TM), :], xbuf.at[which], sem.at[which]).wait()

    @pl.when(jnp.logical_and(jnp.logical_and(i == 0, f == 0), valid))
    def _():
        def body(r, carry):
            row_copy(r, r, 0).start()
            return carry
        lax.fori_loop(0, MOE_TM, body, 0)

    @pl.when(f == 0)
    def _():
        acc_ref[...] = jnp.zeros_like(acc_ref)

    @pl.when(jnp.logical_and(f == 0, valid))
    def _():
        block_wait(i % 2)

    @pl.when(valid)
    def _():
        nxt = jnp.minimum(i + 1, n_valid - 1)
        for r in range(part):
            row_copy(nxt * MOE_TM + f * part + r, f * part + r, (i + 1) % 2).start()
        x = _unpack_bf16_pairs(xbuf[i % 2])
        g = jnp.dot(x, wg_ref[...].astype(BF16), preferred_element_type=F32) + bg_ref[...]
        u = jnp.dot(x, wu_ref[...].astype(BF16), preferred_element_type=F32) + bu_ref[...]
        gate = jnp.minimum(g, SWIGLU_LIMIT)
        up = jnp.clip(u, -SWIGLU_LIMIT, SWIGLU_LIMIT)
        act = (up + 1.0) * gate * jax.nn.sigmoid(SWIGLU_ALPHA * gate)
        acc_ref[...] += jnp.dot(act.astype(BF16), wd_ref[...].astype(BF16), preferred_element_type=F32)

    @pl.when(f == nf - 1)
    def _():
        o_ref[...] = jnp.where(valid, acc_ref[...] + bd_ref[...], 0.0)

    @pl.when(jnp.logical_and(f == nf - 1, i == n_valid - 1))
    def _():
        block_wait((i + 1) % 2)


def _moe_blocks(h_packed, slot_tok, block_e, n_valid, w_gu, b_gu, w_down, b_down):
    n_slots = slot_tok.shape[0]
    d = w_down.shape[-1]
    n_blocks = n_slots // MOE_TM
    nf = D_FF // MOE_TF

    def f_eff(i, f, nv):
        return jnp.where(i < nv[0], f, nf - 1)

    grid_spec = pltpu.PrefetchScalarGridSpec(
        num_scalar_prefetch=3,
        grid=(n_blocks, nf),
        in_specs=[
            pl.BlockSpec(memory_space=pl.ANY),
            pl.BlockSpec((None, d, MOE_TF), lambda i, f, be, nv, tok: (be[i], 0, f_eff(i, f, nv))),
            pl.BlockSpec((None, d, MOE_TF), lambda i, f, be, nv, tok: (be[i], 0, nf + f_eff(i, f, nv))),
            pl.BlockSpec((None, 1, MOE_TF), lambda i, f, be, nv, tok: (be[i], 0, f_eff(i, f, nv))),
            pl.BlockSpec((None, 1, MOE_TF), lambda i, f, be, nv, tok: (be[i], 0, nf + f_eff(i, f, nv))),
            pl.BlockSpec((None, MOE_TF, d), lambda i, f, be, nv, tok: (be[i], f_eff(i, f, nv), 0)),
            pl.BlockSpec((None, 1, d), lambda i, f, be, nv, tok: (be[i], 0, 0)),
        ],
        out_specs=pl.BlockSpec((MOE_TM, d), lambda i, f, be, nv, tok: (i, 0)),
        scratch_shapes=[pltpu.VMEM((MOE_TM, d), F32), pltpu.VMEM((2, MOE_TM, h_packed.shape[1]), h_packed.dtype),
                        pltpu.SemaphoreType.DMA((2,))],
    )
    return pl.pallas_call(
        _moe_body,
        name="moe_experts",
        grid_spec=grid_spec,
        out_shape=jax.ShapeDtypeStruct((n_slots, d), F32),
        compiler_params=pltpu.CompilerParams(dimension_semantics=("arbitrary", "arbitrary"),
                                             vmem_limit_bytes=VMEM_LIMIT_BYTES),
    )(block_e, n_valid, slot_tok, h_packed, w_gu, w_gu, b_gu[:, None, :], b_gu[:, None, :], w_down,
      b_down[:, None, :])


def _route(logits):
    n_tok = logits.shape[0]
    n_assign = n_tok * TOP_K
    top_logit, top_e = lax.top_k(logits, TOP_K)
    top_p = jax.nn.softmax(top_logit, axis=-1)
    flat_e = top_e.reshape(-1)
    onehot = (flat_e[:, None] == jnp.arange(N_EXPERTS)[None, :]).astype(jnp.int32)
    csum = jnp.cumsum(onehot, axis=0)
    counts = csum[-1]
    rank = jnp.take_along_axis(csum, flat_e[:, None], axis=1)[:, 0] - 1
    padded = (counts + MOE_TM - 1) // MOE_TM * MOE_TM
    pad_end = jnp.cumsum(padded)
    pad_start = pad_end - padded
    slot = (pad_start[flat_e] + rank).astype(jnp.int32)
    n_blocks = -(-(n_assign + N_EXPERTS * (MOE_TM - 1)) // MOE_TM)
    n_slots = n_blocks * MOE_TM
    slot_tok = jnp.zeros((n_slots,), jnp.int32).at[slot].set(jnp.arange(n_assign, dtype=jnp.int32) // TOP_K)
    n_valid = (pad_end[-1] // MOE_TM).astype(jnp.int32)
    blk = jnp.arange(n_blocks, dtype=jnp.int32)
    block_e = jnp.searchsorted(pad_end, jnp.minimum(blk, n_valid - 1) * MOE_TM, side='right').astype(jnp.int32)
    block_e = jnp.minimum(block_e, N_EXPERTS - 1)
    return slot, slot_tok, top_p, block_e, n_valid.reshape(1)


PREP_ROWS = 4096


def _prep_body(x_ref, w_ref, o_ref, *, t_first, t_rest):
    i = pl.program_id(0)
    j = pl.program_id(1)
    t_seq = jnp.where(i == 0, t_first, t_rest)
    x = x_ref[...]
    n = x.shape[0]
    row = lax.broadcasted_iota(jnp.int32, x.shape, 0)
    tpos = row & (t_seq - 1)
    xm = jnp.where(tpos == 0, 0.0, pltpu.roll(x, 1, axis=0))
    xp = jnp.where(tpos == t_seq - 1, 0.0, pltpu.roll(x, n - 1, axis=0))
    w = w_ref[...]
    y = xm * w[0:1] + x * w[1:2] + xp * w[2:3]
    y = y * jax.nn.sigmoid(y)
    rs = lax.rsqrt(jnp.sum(y * y, axis=-1, keepdims=True) + EPS)
    nq = QK_W // DK
    fac = jnp.where(j < nq, rs * (DK ** -0.5), jnp.where(j < 2 * nq, rs, 1.0))
    o_ref[...] = (y * fac).astype(o_ref.dtype)


def _qkv_prep(proj, conv_w, *, t_first, t_rest):
    n = proj.shape[0]
    return pl.pallas_call(
        functools.partial(_prep_body, t_first=t_first, t_rest=t_rest),
        name="qkv_prep",
        grid=(n // PREP_ROWS, QKV_W // DK),
        in_specs=[pl.BlockSpec((PREP_ROWS, DK), lambda i, j: (i, j)),
                  pl.BlockSpec((3, DK), lambda i, j: (0, j))],
        out_specs=pl.BlockSpec((PREP_ROWS, DK), lambda i, j: (i, j)),
        out_shape=jax.ShapeDtypeStruct((n, QKV_W), BF16),
        compiler_params=pltpu.CompilerParams(dimension_semantics=("parallel", "parallel"),
                                             vmem_limit_bytes=VMEM_LIMIT_BYTES),
    )(proj, conv_w)


PROMPT_SUB = 2
SOLVE_WAYS = 8


def _mmul(a, b):
    return jnp.dot(a, b, preferred_element_type=F32)


def _delta_body(q_ref, k_ref, v_ref, z_ref, cols_ref, rows_ref, nw_ref, s0f_ref, s0b_ref, o_prev_ref,
                o_ref, sf_ref, sb_ref, acc_ref, s_ref, zz_ref, pr_ref, kt_ref, *, n_sub, n_chunks, zero_init):
    del o_prev_ref
    n_all = n_sub * n_chunks
    ways = min(SOLVE_WAYS, n_all)
    lane = lax.broadcasted_iota(jnp.int32, (CHUNK, 2 * CHUNK), 1)
    row = lax.broadcasted_iota(jnp.int32, (CHUNK, 2 * CHUNK), 0)
    pos = lane & (CHUNK - 1)
    left = lane < CHUNK
    left1 = lax.broadcasted_iota(jnp.int32, (1, 2 * CHUNK), 1) < CHUNK
    eye2 = (pos == row).astype(F32)
    incl = (pos <= row, pos >= row)
    strict = (pos < row, pos > row)
    same8 = (row >> 3) == (pos >> 3)
    same16 = (row >> 4) == (pos >> 4)
    same32 = (row >> 5) == (pos >> 5)
    nt = (((1,), (1,)), ((), ()))
    tn = (((0,), (0,)), ((), ()))

    def split(m):
        return jnp.where(left, m, 0.0), jnp.where(left, 0.0, m)

    def bdiag(m):
        return jnp.concatenate(split(m), axis=0).astype(BF16)

    def pmm(a, b):
        return _mmul(a.astype(BF16), bdiag(b))

    def inv_unit_tri(lms):
        d8 = [jnp.where(same8, lm, 0.0) for lm in lms]
        xs = [eye2 - d for d in d8]
        d2 = [pmm(d, d) for d in d8]
        xs = [x + pmm(x, d) for x, d in zip(xs, d2)]
        d4 = [pmm(d, d) for d in d2]
        xs = [x + pmm(x, d) for x, d in zip(xs, d4)]
        for keep in (same16 & ~same8, same32 & ~same16, ~same32):
            ys = [pmm(jnp.where(keep, lm, 0.0), x) for lm, x in zip(lms, xs)]
            xs = [x - pmm(x, y) for x, y in zip(xs, ys)]
        return xs

    def chunk_terms(c, d):
        r0 = pl.multiple_of(c * CHUNK, CHUNK)
        cols = cols_ref[pl.ds(r0, CHUNK), :]
        rows = rows_ref[c]

        def lanes_of(col):
            return jnp.broadcast_to(cols[:, col:col + 1], (CHUNK, 2 * CHUNK))
        gcol = [lanes_of(2 * d + h) for h in (0, 1)]
        bcol = [lanes_of(4 + 2 * d + h) for h in (0, 1)]
        gl = [rows[4 + 2 * d + h:5 + 2 * d + h] for h in (0, 1)]
        return r0, rows, gcol, bcol, gl

    acc_ref[...] = jnp.zeros_like(acc_ref)
    rep = s0f_ref.shape[1]
    if zero_init:
        s_ref[...] = jnp.zeros_like(s_ref)
    else:
        for sub in range(n_sub):
            for h in range(rep):
                s_ref[2 * rep * sub + h] = s0f_ref[sub, h]
                s_ref[2 * rep * sub + rep + h] = s0b_ref[sub, h]

    def solve(i, carry):
        chunks = [i + part * (n_all // ways) for part in range(ways)]
        qcs, kcs, vcs, kkms, qkms = [], [], [], [], []
        for c in chunks:
            r0 = pl.multiple_of(c * CHUNK, CHUNK)
            kb = k_ref[pl.ds(r0, CHUNK), :]
            qb = q_ref[pl.ds(r0, CHUNK), :]
            kk2 = jnp.concatenate([kb, kb], axis=0)
            kcs.append(kb.astype(F32))
            qcs.append(qb.astype(F32))
            vcs.append(v_ref[pl.ds(r0, CHUNK), :])
            kkms.append(lax.dot_general(kb, kk2, nt, preferred_element_type=F32))
            qkms.append(lax.dot_general(qb, kk2, nt, preferred_element_type=F32))
            kt_ref[c] = jnp.transpose(kcs[-1]).astype(BF16)
        chains = [(j, d) for j in range(ways) for d in (0, 1)]
        lms, rhs2s, a2s, decays = [], [], [], []
        for j, d in chains:
            c = chunks[j]
            _, rows, gcol, bcol, gl = chunk_terms(c, d)
            gcol2 = jnp.where(left, gcol[0], gcol[1])
            bcol2 = jnp.where(left, bcol[0], bcol[1])
            grow2 = jnp.where(left1, rows[2 * d:2 * d + 1], rows[2 * d + 1:2 * d + 2])
            dm = jnp.exp(jnp.where(incl[d], gcol2 - grow2, -jnp.inf))
            a2s.append(qkms[j] * dm)
            eg = [jnp.exp(g) for g in gcol]
            decays.append((eg, [jnp.exp(gl[h] - gcol[h]) for h in (0, 1)]))
            lms.append(jnp.where(strict[d], kkms[j] * dm * bcol2, 0.0))
            rhs = [jnp.concatenate([vcs[j][:, h * DV:(h + 1) * DV] * bcol[h], kcs[j] * (bcol[h] * eg[h])], axis=1)
                   for h in (0, 1)]
            rhs2s.append(jnp.concatenate(rhs, axis=0).astype(BF16))
        t2s = inv_unit_tri(lms)
        sols = [[_mmul(th.astype(BF16), rhs2) for th in split(t2)]
                for t2, rhs2 in zip(t2s, rhs2s)]
        aws = [[_mmul(ah.astype(BF16), jnp.concatenate(sol, axis=0).astype(BF16)) for ah in split(a2)]
               for a2, sol in zip(a2s, sols)]
        for (j, d), sol, aw, (eg, ek) in zip(chains, sols, aws, decays):
            c = chunks[j]
            zz, pr = [], []
            for h in (0, 1):
                zz += [sol[h][:, :DV] * ek[h], sol[h][:, DV:] * ek[h]]
                pr += [qcs[j] * eg[h] - aw[h][:, DV:], aw[h][:, :DV]]
            zz_ref[d, c] = jnp.concatenate(zz, axis=1).astype(BF16)
            pr_ref[d, c] = jnp.concatenate(pr, axis=1).astype(BF16)
        return carry

    lax.fori_loop(0, n_all // ways, solve, 0)

    def step(i, carry):
        lanes = [(sub, d) for sub in range(n_sub) for d in (0, 1)]
        where = [sub * n_chunks + (i if d == 0 else n_chunks - 1 - i) for sub, d in lanes]
        ngs, prs, gls = [], [], []
        for c, (_, d) in zip(where, lanes):
            rows = rows_ref[c]
            gls.append([rows[4 + 2 * d + h:5 + 2 * d + h] for h in (0, 1)])
            ngs.append(_mmul(kt_ref[c], zz_ref[d, c]))
            prs.append(pr_ref[d, c])
        streams = [(n, h) for n in range(len(lanes)) for h in (0, 1)]
        outs = []
        for n, h in streams:
            pg = jnp.concatenate([prs[n][:, 2 * h * DV:(2 * h + 1) * DV],
                                  ngs[n][:, (2 * h + 1) * DV:(2 * h + 2) * DV].astype(BF16)], axis=0)
            outs.append(_mmul(pg, s_ref[2 * n + h].astype(BF16)))
        for (n, h), out in zip(streams, outs):
            s = 2 * n + h
            r0 = pl.multiple_of(where[n] * CHUNK, CHUNK)
            acc_ref[pl.ds(r0, CHUNK), h * DV:(h + 1) * DV] += (
                out[:CHUNK] + prs[n][:, (2 * h + 1) * DV:(2 * h + 2) * DV].astype(F32))
            s_ref[s] = s_ref[s] * jnp.exp(gls[n][h]) - out[CHUNK:] + ngs[n][:, 2 * h * DV:(2 * h + 1) * DV]
        return carry

    lax.fori_loop(0, n_chunks, step, 0)

    nw = nw_ref[...]

    def finish(j, carry):
        r0 = pl.multiple_of(j * CHUNK, CHUNK)
        o = acc_ref[pl.ds(r0, CHUNK), :]
        z = z_ref[pl.ds(r0, CHUNK), :]
        for h in (0, 1):
            oh = o[:, h * DV:(h + 1) * DV]
            zh = z[:, h * DV:(h + 1) * DV]
            y = oh * lax.rsqrt(jnp.mean(oh * oh, axis=-1, keepdims=True) + EPS) * nw * (zh * jax.nn.sigmoid(zh))
            o_ref[pl.ds(r0, CHUNK), h * DV:(h + 1) * DV] = y.astype(o_ref.dtype)
        return carry

    lax.fori_loop(0, n_all, finish, 0)
    for sub in range(n_sub):
        for h in range(rep):
            sf_ref[sub, h] = s_ref[2 * rep * sub + h]
            sb_ref[sub, h] = s_ref[2 * rep * sub + rep + h]


def _deltanet(qkvn, proj, cols, rows, norm_w, s0f, s0b, o_prev, *, n_seq, t_seq, n_sub, row_blk0, zero_init):
    n_chunks = t_seq // CHUNK
    t_blk = n_sub * t_seq
    c_blk = n_sub * n_chunks
    nq = QK_W // DK
    z_blk0 = QKV_W // (2 * DV)
    rep = NV_HEADS // NK_HEADS
    state_spec = pl.BlockSpec((n_sub, None, rep, DK, DV), lambda b, kh: (b, 0, kh, 0, 0))
    in_specs = [
        pl.BlockSpec((t_blk, DK), lambda b, kh: (row_blk0 + b, kh)),
        pl.BlockSpec((t_blk, DK), lambda b, kh: (row_blk0 + b, nq + kh)),
        pl.BlockSpec((t_blk, rep * DV), lambda b, kh: (row_blk0 + b, nq + kh)),
        pl.BlockSpec((t_blk, rep * DV), lambda b, kh: (row_blk0 + b, z_blk0 + kh)),
        pl.BlockSpec((None, t_blk, 8), lambda b, kh: (kh, row_blk0 + b, 0)),
        pl.BlockSpec((None, c_blk, 8, 2 * CHUNK), lambda b, kh: (kh, row_blk0 + b, 0, 0)),
        pl.BlockSpec((1, DV), lambda b, kh: (0, 0)),
        pl.BlockSpec((n_sub, rep, DK, DV), lambda b, kh: (0 if zero_init else b, kh, 0, 0)),
        pl.BlockSpec((n_sub, rep, DK, DV), lambda b, kh: (0 if zero_init else b, kh, 0, 0)),
        pl.BlockSpec(memory_space=pl.ANY),
    ]
    out_specs = [pl.BlockSpec((t_blk, rep * DV), lambda b, kh: (row_blk0 + b, kh)), state_spec, state_spec]
    out_shape = [
        jax.ShapeDtypeStruct((qkvn.shape[0], V_W), BF16),
        jax.ShapeDtypeStruct((n_seq, DEPTH, NV_HEADS, DK, DV), F32),
        jax.ShapeDtypeStruct((n_seq, DEPTH, NV_HEADS, DK, DV), F32),
    ]
    args = [qkvn, qkvn, qkvn, proj, cols, rows, norm_w.reshape(1, DV), s0f, s0b, o_prev]
    return pl.pallas_call(
        functools.partial(_delta_body, n_sub=n_sub, n_chunks=n_chunks, zero_init=zero_init),
        grid=(n_seq // n_sub, NK_HEADS),
        input_output_aliases={len(args) - 1: 0},
        name="deltanet",
        in_specs=in_specs,
        out_specs=out_specs,
        out_shape=out_shape,
        scratch_shapes=[pltpu.VMEM((t_blk, rep * DV), F32), pltpu.VMEM((2 * rep * n_sub, DK, DV), F32),
                        pltpu.VMEM((2, c_blk, CHUNK, 2 * rep * DV), BF16),
                        pltpu.VMEM((2, c_blk, CHUNK, 2 * rep * DV), BF16),
                        pltpu.VMEM((c_blk, DK, CHUNK), BF16)],
        compiler_params=pltpu.CompilerParams(dimension_semantics=("parallel", "parallel"),
                                             vmem_limit_bytes=VMEM_LIMIT_BYTES),
    )(*args)


def _decay_tables(ab, a_log, dt_bias):
    n = ab.shape[0]
    nh2 = 2 * NV_HEADS
    rep = NV_HEADS // NK_HEADS
    beta = jax.nn.sigmoid(ab[:, :nh2])
    g = -jnp.exp(a_log.reshape(nh2)) * jax.nn.softplus(ab[:, nh2:] + dt_bias.reshape(nh2))
    gch = g.reshape(n // CHUNK, CHUNK, 2, NV_HEADS)
    gcf = jnp.cumsum(gch[:, :, 0], axis=1)
    gcb = jnp.flip(jnp.cumsum(jnp.flip(gch[:, :, 1], 1), axis=1), 1)
    gc = jnp.stack([gcf, gcb], axis=2)
    gl = jnp.stack([gcf[:, -1], gcb[:, 0]], axis=1)

    def per_kh(a):
        a = a.reshape(a.shape[:-2] + (2, NK_HEADS, rep))
        return jnp.moveaxis(a, -3, -2).reshape(a.shape[:-3] + (NK_HEADS, 2 * rep))
    gc_k = per_kh(gc)
    cols = jnp.concatenate([gc_k.reshape(n, NK_HEADS, 2 * rep), per_kh(beta.reshape(n, 2, NV_HEADS))], axis=-1)
    cols = jnp.transpose(cols, (1, 0, 2))
    gr = jnp.transpose(gc_k, (2, 0, 3, 1))
    gr = jnp.concatenate([gr, gr], axis=-1)
    glr = jnp.broadcast_to(jnp.transpose(per_kh(gl), (1, 0, 2))[..., None], gr.shape)
    rows = jnp.concatenate([gr, glr], axis=2)
    return cols, rows


TOK_TILE = 512
MIX_TILE = 256
COMBINE_TILE = 128


def _group_of_tile(i, tile, n_prompt, t_sample):
    first = n_prompt // tile
    return jnp.where(i < first, 0, 1 + (i - first) // (t_sample // tile))


def _token_specs(tile, n_prompt, t_sample, d):
    first = n_prompt // tile
    per_seq = t_sample // tile
    return [
        pl.BlockSpec((tile, d), lambda i, *_: (jnp.minimum(i, first - 1), 0)),
        pl.BlockSpec((tile, d), lambda i, *_: (jnp.maximum(i - first, 0), 0)),
        pl.BlockSpec((tile, d), lambda i, *_: (jnp.maximum(i - first, 0) % per_seq, 0)),
        pl.BlockSpec((None, 1, 6 * d), lambda i, *_: (_group_of_tile(i, tile, n_prompt, t_sample), 0, 0)),
    ]


def _token_rows(i, first, xp_ref, xs_ref, pos_ref):
    return jnp.where(i < first, xp_ref[...], xs_ref[...] + pos_ref[...])


def _modulate_body(xp_ref, xs_ref, pos_ref, mod_ref, h_ref, *, first):
    d = h_ref.shape[-1]
    x = _token_rows(pl.program_id(0), first, xp_ref, xs_ref, pos_ref)
    h_ref[...] = (x * (1.0 + mod_ref[:, d:2 * d]) + mod_ref[:, 0:d]).astype(h_ref.dtype)


def _modulate(xp, xs, pos, mod, *, t_sample):
    n_p, d = xp.shape
    n = n_p + xs.shape[0]
    return pl.pallas_call(
        functools.partial(_modulate_body, first=n_p // TOK_TILE),
        grid=(n // TOK_TILE,),
        in_specs=_token_specs(TOK_TILE, n_p, t_sample, d),
        out_specs=pl.BlockSpec((TOK_TILE, d), lambda i: (i, 0)),
        out_shape=jax.ShapeDtypeStruct((n, d), BF16),
        compiler_params=pltpu.CompilerParams(dimension_semantics=("parallel",), vmem_limit_bytes=VMEM_LIMIT_BYTES),
        name="modulate",
    )(xp, xs, pos, mod)


def _gconv_body(ub_ref, uc_ref, ux_ref, w_ref, o_ref, *, t_first, t_rest):
    t_seq = jnp.where(pl.program_id(0) == 0, t_first, t_rest)
    x = uc_ref[...] * ux_ref[...]
    n = x.shape[0]
    tpos = lax.broadcasted_iota(jnp.int32, x.shape, 0) & (t_seq - 1)
    xm = jnp.where(tpos == 0, 0.0, pltpu.roll(x, 1, axis=0))
    xp = jnp.where(tpos == t_seq - 1, 0.0, pltpu.roll(x, n - 1, axis=0))
    w = w_ref[...]
    o_ref[...] = (ub_ref[...] * (xm * w[0:1] + x * w[1:2] + xp * w[2:3])).astype(o_ref.dtype)


def _gated_conv(proj, conv_w, col0, *, t_first, t_rest):
    n = proj.shape[0]
    nblk = CONV_CH // DK
    b0 = col0 // DK
    return pl.pallas_call(
        functools.partial(_gconv_body, t_first=t_first, t_rest=t_rest),
        grid=(n // PREP_ROWS, nblk),
        in_specs=[pl.BlockSpec((PREP_ROWS, DK), lambda i, j: (i, b0 + j)),
                  pl.BlockSpec((PREP_ROWS, DK), lambda i, j: (i, b0 + nblk + j)),
                  pl.BlockSpec((PREP_ROWS, DK), lambda i, j: (i, b0 + 2 * nblk + j)),
                  pl.BlockSpec((3, DK), lambda i, j: (0, j))],
        out_specs=pl.BlockSpec((PREP_ROWS, DK), lambda i, j: (i, j)),
        out_shape=jax.ShapeDtypeStruct((n, CONV_CH), BF16),
        compiler_params=pltpu.CompilerParams(dimension_semantics=("parallel", "parallel"),
                                             vmem_limit_bytes=VMEM_LIMIT_BYTES),
        name="gated_conv",
    )(proj, proj, proj, conv_w)


BR_TM = 512
BR_TN = 512


def _branch_body(o_ref, g_ref, wa_ref, wb_ref, ra_ref, rb_ref, m_ref, wa_bf, wb_bf):
    @pl.when(pl.program_id(1) == 0)
    def _():
        wa_bf[...] = wa_ref[...].astype(BF16)
        wb_bf[...] = wb_ref[...].astype(BF16)
    ya = jnp.dot(o_ref[...], wa_bf[...], preferred_element_type=F32)
    yb = jnp.dot(g_ref[...], wb_bf[...], preferred_element_type=F32)
    m_ref[...] = (jax.nn.sigmoid(ra_ref[...]) * ya + jax.nn.sigmoid(rb_ref[...]) * yb).astype(m_ref.dtype)


def _branch_merge(o, g, w_a, w_b, proj, ra_col, rb_col):
    n = o.shape[0]
    d = w_a.shape[1]
    el = pl.Element
    return pl.pallas_call(
        _branch_body,
        grid=(d // BR_TN, n // BR_TM),
        in_specs=[pl.BlockSpec((BR_TM, o.shape[1]), lambda j, i: (i, 0)),
                  pl.BlockSpec((BR_TM, g.shape[1]), lambda j, i: (i, 0)),
                  pl.BlockSpec((w_a.shape[0], BR_TN), lambda j, i: (0, j)),
                  pl.BlockSpec((w_b.shape[0], BR_TN), lambda j, i: (0, j)),
                  pl.BlockSpec((el(BR_TM), el(BR_TN)),
                               lambda j, i: (i * BR_TM, pl.multiple_of(ra_col + j * BR_TN, DK))),
                  pl.BlockSpec((el(BR_TM), el(BR_TN)),
                               lambda j, i: (i * BR_TM, pl.multiple_of(rb_col + j * BR_TN, DK)))],
        out_specs=pl.BlockSpec((BR_TM, BR_TN), lambda j, i: (i, j)),
        out_shape=jax.ShapeDtypeStruct((n, d), BF16),
        scratch_shapes=[pltpu.VMEM((w_a.shape[0], BR_TN), BF16), pltpu.VMEM((w_b.shape[0], BR_TN), BF16)],
        compiler_params=pltpu.CompilerParams(dimension_semantics=("arbitrary", "arbitrary"),
                                             vmem_limit_bytes=VMEM_LIMIT_BYTES),
        name="branch_merge",
    )(o, g, w_a, w_b, proj, proj)


def _ln_rows(v, g, b):
    mu = jnp.mean(v, axis=-1, keepdims=True)
    c = v - mu
    return c * lax.rsqrt(jnp.mean(c * c, axis=-1, keepdims=True) + EPS) * g + b


def _mix_body(m_ref, wo_ref, xp_ref, xs_ref, pos_ref, mod_ref, g_ref, b_ref, rw_ref, rb_ref,
              x1_ref, h2_ref, lg_ref, *, first):
    d = x1_ref.shape[-1]
    x = _token_rows(pl.program_id(0), first, xp_ref, xs_ref, pos_ref)
    mixed = jnp.dot(m_ref[...], wo_ref[...], preferred_element_type=F32)
    x1 = _ln_rows(DEEPNORM_ALPHA * x + mod_ref[:, 2 * d:3 * d] * mixed, g_ref[...], b_ref[...])
    h2 = x1 * (1.0 + mod_ref[:, 4 * d:5 * d]) + mod_ref[:, 3 * d:4 * d]
    x1_ref[...] = x1
    h2_ref[...] = _pack_bf16_pairs(h2)
    lg_ref[...] = jnp.dot(h2, rw_ref[...], precision=lax.Precision.HIGHEST, preferred_element_type=F32) + rb_ref[...]


def _mix_norm_route(m, w_o_bf, xp, xs, pos, mod, ln_g, ln_b, router_w, router_b, *, t_sample):
    n, d = m.shape
    n_p = xp.shape[0]
    ne = router_w.shape[1]
    const = lambda i: (0, 0)
    return pl.pallas_call(
        functools.partial(_mix_body, first=n_p // MIX_TILE),
        grid=(n // MIX_TILE,),
        in_specs=[pl.BlockSpec((MIX_TILE, d), lambda i: (i, 0)), pl.BlockSpec((d, d), const)]
        + _token_specs(MIX_TILE, n_p, t_sample, d)
        + [pl.BlockSpec((1, d), const), pl.BlockSpec((1, d), const),
           pl.BlockSpec((d, ne), const), pl.BlockSpec((1, ne), const)],
        out_specs=[pl.BlockSpec((MIX_TILE, d), lambda i: (i, 0)), pl.BlockSpec((MIX_TILE, d // 2), lambda i: (i, 0)),
                   pl.BlockSpec((MIX_TILE, ne), lambda i: (i, 0))],
        out_shape=[jax.ShapeDtypeStruct((n, d), F32), jax.ShapeDtypeStruct((n, d // 2), jnp.uint32),
                   jax.ShapeDtypeStruct((n, ne), F32)],
        compiler_params=pltpu.CompilerParams(dimension_semantics=("parallel",), vmem_limit_bytes=VMEM_LIMIT_BYTES),
        name="mix_norm_route",
    )(m, w_o_bf, xp, xs, pos, mod, ln_g.reshape(1, d), ln_b.reshape(1, d), router_w, router_b.reshape(1, ne))


def _combine_body(slot_ref, y_hbm, p_ref, x1_ref, mod_ref, g_ref, b_ref, o_ref, buf, sem, *, n_tiles):
    tt = COMBINE_TILE
    d = o_ref.shape[-1]
    i = pl.program_id(0)

    def row_copy(src_row, dst_row, which):
        return pltpu.make_async_copy(y_hbm.at[pl.ds(src_row, 1), :], buf.at[which, pl.ds(dst_row, 1), :],
                                     sem.at[which])

    def issue(tile, which):
        base = tile * (tt * TOP_K)

        def body(t, carry):
            for k in range(TOP_K):
                row_copy(slot_ref[base + t * TOP_K + k], k * tt + t, which).start(priority=k % 2)
            return carry
        lax.fori_loop(0, tt, body, 0)

    @pl.when(i == 0)
    def _():
        issue(0, 0)

    @pl.when(i + 1 < n_tiles)
    def _():
        issue(i + 1, (i + 1) % 2)

    cur = i % 2
    pltpu.make_async_copy(y_hbm.at[pl.ds(0, TOP_K * tt), :], buf.at[cur], sem.at[cur]).wait()
    p = p_ref[...]
    ff = p[:, 0:1] * buf[cur, 0:tt, :]
    for k in range(1, TOP_K):
        ff = ff + p[:, k:k + 1] * buf[cur, k * tt:(k + 1) * tt, :]
    o_ref[...] = _ln_rows(DEEPNORM_ALPHA * x1_ref[...] + mod_ref[:, 5 * d:6 * d] * ff, g_ref[...], b_ref[...])


def _combine_norm(slot, y, top_p, x1, mod, ln_g, ln_b, *, n_prompt, t_sample):
    n, d = x1.shape
    tt = COMBINE_TILE
    n_tiles = n // tt
    grid_spec = pltpu.PrefetchScalarGridSpec(
        num_scalar_prefetch=1,
        grid=(n_tiles,),
        in_specs=[pl.BlockSpec(memory_space=pl.ANY),
                  pl.BlockSpec((tt, TOP_K), lambda i, s: (i, 0)),
                  pl.BlockSpec((tt, d), lambda i, s: (i, 0)),
                  pl.BlockSpec((None, 1, 6 * d), lambda i, s: (_group_of_tile(i, tt, n_prompt, t_sample), 0, 0)),
                  pl.BlockSpec((1, d), lambda i, s: (0, 0)),
                  pl.BlockSpec((1, d), lambda i, s: (0, 0))],
        out_specs=pl.BlockSpec((tt, d), lambda i, s: (i, 0)),
        scratch_shapes=[pltpu.VMEM((2, TOP_K * tt, d), F32), pltpu.SemaphoreType.DMA((2,))],
    )
    return pl.pallas_call(
        functools.partial(_combine_body, n_tiles=n_tiles),
        grid_spec=grid_spec,
        out_shape=jax.ShapeDtypeStruct((n, d), F32),
        compiler_params=pltpu.CompilerParams(dimension_semantics=("arbitrary",), vmem_limit_bytes=VMEM_LIMIT_BYTES),
        name="combine_norm",
    )(slot, y, top_p, x1, mod, ln_g.reshape(1, d), ln_b.reshape(1, d))


def _grid_pos_embed(rows, dim):
    quarter = dim // 4
    freqs = 1.0 / (POS_BASE ** (jnp.arange(quarter, dtype=F32) / quarter))

    def axis_embed(n):
        ang = jnp.arange(n, dtype=F32)[:, None] * freqs[None, :]
        return jnp.concatenate([jnp.sin(ang), jnp.cos(ang)], axis=-1)
    er = axis_embed(rows)
    ec = axis_embed(GRID_W)
    pos = jnp.concatenate([jnp.broadcast_to(er[:, None, :], (rows, GRID_W, dim // 2)),
                           jnp.broadcast_to(ec[None, :, :], (rows, GRID_W, dim // 2))], axis=-1)
    return pos.reshape(rows * GRID_W, dim)


def kernel(x_prompt, x_sample, c, state_fwd, state_bwd, c_ctx, w_mod, b_mod, w_in, conv_qkv, a_log, dt_bias,
           norm_o, w_a_out, conv_b, w_b_out, w_o, ln1_g, ln1_b, router_w, router_b, w_gu, b_gu, w_down, b_down,
           ln2_g, ln2_b):
    nb_p, t_p, d = x_prompt.shape
    nb_s, t_s, _ = x_sample.shape
    n_p = nb_p * t_p
    n_s = nb_s * t_s
    assert n_p == PREP_ROWS and t_s == PREP_ROWS
    xp = x_prompt.reshape(n_p, d)
    xs = x_sample.reshape(n_s, d)
    pos = _grid_pos_embed(t_s // GRID_W, d)
    offs = tuple(int(o) for o in np.cumsum(IN_SIZES)[:-1])

    l = 0
    cvec = jnp.concatenate([c_ctx[None, :], c], axis=0)
    cpad = jnp.zeros((8, d), F32).at[:1 + nb_s].set(jax.nn.silu(cvec))
    mod = (_mm(cpad, w_mod[l], tm=8, tn=512)[:1 + nb_s] + b_mod[l])[:, None, :]

    h = _modulate(xp, xs, pos, mod, t_sample=t_s)
    proj = _mm(h, w_in[l], tm=2048, tn=512)

    qkvn = _qkv_prep(proj, conv_qkv[l], t_first=t_p, t_rest=t_s)
    cols, rws = _decay_tables(proj[:, offs[1]:offs[3]], a_log[l], dt_bias[l])
    zero_state = jnp.zeros((PROMPT_SUB, NV_HEADS, DK, DV), F32)
    o_all = jnp.zeros((n_p + n_s, V_W), BF16)
    o_all, sf, sb = _deltanet(qkvn, proj, cols, rws, norm_o[l], zero_state, zero_state, o_all,
                              n_seq=nb_p, t_seq=t_p, n_sub=PROMPT_SUB, row_blk0=0, zero_init=True)
    o_all, _, _ = _deltanet(qkvn, proj, cols, rws, norm_o[l], state_fwd[:, l], state_bwd[:, l], o_all,
                            n_seq=nb_s, t_seq=t_s, n_sub=1, row_blk0=n_p // t_s, zero_init=False)

    gconv = _gated_conv(proj, conv_b[l], offs[3], t_first=t_p, t_rest=t_s)
    m = _branch_merge(o_all, gconv, w_a_out[l], w_b_out[l], proj, offs[6], offs[7])
    x1, h2, logits = _mix_norm_route(m, w_o[l].astype(BF16), xp, xs, pos, mod, ln1_g[l], ln1_b[l],
                                     router_w[l], router_b[l], t_sample=t_s)
    slot, slot_tok, top_p, block_e, n_valid = _route(logits)
    y = _moe_blocks(h2, slot_tok, block_e, n_valid, w_gu[l], b_gu[l], w_down[l], b_down[l])
    x2 = _combine_norm(slot, y, top_p, x1, mod, ln2_g[l], ln2_b[l], n_prompt=n_p, t_sample=t_s)
    y_prompt = x2[:n_p].reshape(nb_p, t_p, d)
    y_sample = x2[n_p:].reshape(nb_s, t_s, d)
    return (y_prompt, y_sample, sf, sb)
```

```python
import functools

import jax
import jax.numpy as jnp
import numpy as np
from jax import lax
from jax.experimental import pallas as pl
from jax.experimental.pallas import tpu as pltpu

F32 = jnp.float32
BF16 = jnp.bfloat16

D_MODEL = 2048
GRID_W = 64
NK_HEADS = 16
NV_HEADS = 32
DK = 128
DV = 128
QK_W = NK_HEADS * DK
V_W = NV_HEADS * DV
QKV_W = 2 * QK_W + V_W
CHUNK = 64
CONV_CH = D_MODEL
IN_SIZES = (QKV_W, V_W, 2 * NV_HEADS, 2 * NV_HEADS, CONV_CH, CONV_CH, CONV_CH, D_MODEL, D_MODEL)
N_EXPERTS = 32
TOP_K = 4
D_FF = D_MODEL
SWIGLU_LIMIT = 7.0
SWIGLU_ALPHA = 1.702
DEPTH = 1
DEEPNORM_ALPHA = (2 * DEPTH) ** 0.25
EPS = 1e-6
POS_BASE = 10000.0

VMEM_LIMIT_BYTES = 56 * 1024 * 1024


def _mm_body(x_ref, w_ref, o_ref):
    o_ref[...] = jnp.dot(x_ref[...].astype(BF16), w_ref[...].astype(BF16), preferred_element_type=F32)


def _mm(x, w, *, tm, tn):
    m, k = x.shape
    n = w.shape[1]
    assert m % tm == 0
    return pl.pallas_call(
        _mm_body,
        name="matmul",
        grid=(m // tm, pl.cdiv(n, tn)),
        in_specs=[pl.BlockSpec((tm, k), lambda i, j: (i, 0)),
                  pl.BlockSpec((k, tn), lambda i, j: (0, j))],
        out_specs=pl.BlockSpec((tm, tn), lambda i, j: (i, j)),
        out_shape=jax.ShapeDtypeStruct((m, n), F32),
        compiler_params=pltpu.CompilerParams(dimension_semantics=("parallel", "parallel"),
                                             vmem_limit_bytes=VMEM_LIMIT_BYTES),
    )(x, w)


MOE_TM = 512
MOE_TF = 512


def _pack_bf16_pairs(x):
    n = x.shape[1] // 2
    lo = lax.bitcast_convert_type(x[:, :n].astype(BF16).astype(F32), jnp.uint32)
    hi = lax.bitcast_convert_type(x[:, n:].astype(BF16).astype(F32), jnp.uint32)
    return (lo >> 16) | hi


def _unpack_bf16_pairs(w):
    lo = lax.bitcast_convert_type(w << 16, F32)
    hi = lax.bitcast_convert_type(w & jnp.uint32(0xFFFF0000), F32)
    return jnp.concatenate([lo, hi], axis=1).astype(BF16)


def _moe_body(be_ref, nv_ref, tok_ref, h_hbm, wg_ref, wu_ref, bg_ref, bu_ref, wd_ref, bd_ref, o_ref,
              acc_ref, xbuf, sem):
    i = pl.program_id(0)
    f = pl.program_id(1)
    nf = pl.num_programs(1)
    n_valid = nv_ref[0]
    valid = i < n_valid
    part = MOE_TM // (D_FF // MOE_TF)

    def row_copy(slot_row, buf_row, which):
        return pltpu.make_async_copy(h_hbm.at[pl.ds(tok_ref[slot_row], 1), :], xbuf.at[which, pl.ds(buf_row, 1), :],
                                     sem.at[which])

    def block_wait(which):
        pltpu.make_async_copy(h_hbm.at[pl.ds(0, MOE_TM), :], xbuf.at[which], sem.at[which]).wait()

    @pl.when(jnp.logical_and(jnp.logical_and(i == 0, f == 0), valid))
    def _():
        def body(r, carry):
            row_copy(r, r, 0).start()
            return carry
        lax.fori_loop(0, MOE_TM, body, 0)

    @pl.when(f == 0)
    def _():
        acc_ref[...] = jnp.zeros_like(acc_ref)

    @pl.when(jnp.logical_and(f == 0, valid))
    def _():
        block_wait(i % 2)

    @pl.when(valid)
    def _():
        nxt = jnp.minimum(i + 1, n_valid - 1)
        for r in range(part):
            row_copy(nxt * MOE_TM + f * part + r, f * part + r, (i + 1) % 2).start()
        x = _unpack_bf16_pairs(xbuf[i % 2])
        g = jnp.dot(x, wg_ref[...].astype(BF16), preferred_element_type=F32) + bg_ref[...]
        u = jnp.dot(x, wu_ref[...].astype(BF16), preferred_element_type=F32) + bu_ref[...]
        gate = jnp.minimum(g, SWIGLU_LIMIT)
        up = jnp.clip(u, -SWIGLU_LIMIT, SWIGLU_LIMIT)
        act = (up + 1.0) * gate * jax.nn.sigmoid(SWIGLU_ALPHA * gate)
        acc_ref[...] += jnp.dot(act.astype(BF16), wd_ref[...].astype(BF16), preferred_element_type=F32)

    @pl.when(f == nf - 1)
    def _():
        o_ref[...] = jnp.where(valid, acc_ref[...] + bd_ref[...], 0.0)

    @pl.when(jnp.logical_and(f == nf - 1, i == n_valid - 1))
    def _():
        block_wait((i + 1) % 2)


def _moe_blocks(h_packed, slot_tok, block_e, n_valid, w_gu, b_gu, w_down, b_down):
    n_slots = slot_tok.shape[0]
    d = w_down.shape[-1]
    n_blocks = n_slots // MOE_TM
    nf = D_FF // MOE_TF

    def f_eff(i, f, nv):
        return jnp.where(i < nv[0], f, nf - 1)

    grid_spec = pltpu.PrefetchScalarGridSpec(
        num_scalar_prefetch=3,
        grid=(n_blocks, nf),
        in_specs=[
            pl.BlockSpec(memory_space=pl.ANY),
            pl.BlockSpec((None, d, MOE_TF), lambda i, f, be, nv, tok: (be[i], 0, f_eff(i, f, nv))),
            pl.BlockSpec((None, d, MOE_TF), lambda i, f, be, nv, tok: (be[i], 0, nf + f_eff(i, f, nv))),
            pl.BlockSpec((None, 1, MOE_TF), lambda i, f, be, nv, tok: (be[i], 0, f_eff(i, f, nv))),
            pl.BlockSpec((None, 1, MOE_TF), lambda i, f, be, nv, tok: (be[i], 0, nf + f_eff(i, f, nv))),
            pl.BlockSpec((None, MOE_TF, d), lambda i, f, be, nv, tok: (be[i], f_eff(i, f, nv), 0)),
            pl.BlockSpec((None, 1, d), lambda i, f, be, nv, tok: (be[i], 0, 0)),
        ],
        out_specs=pl.BlockSpec((MOE_TM, d), lambda i, f, be, nv, tok: (i, 0)),
        scratch_shapes=[pltpu.VMEM((MOE_TM, d), F32), pltpu.VMEM((2, MOE_TM, h_packed.shape[1]), h_packed.dtype),
                        pltpu.SemaphoreType.DMA((2,))],
    )
    return pl.pallas_call(
        _moe_body,
        name="moe_experts",
        grid_spec=grid_spec,
        out_shape=jax.ShapeDtypeStruct((n_slots, d), F32),
        compiler_params=pltpu.CompilerParams(dimension_semantics=("arbitrary", "arbitrary"),
                                             vmem_limit_bytes=VMEM_LIMIT_BYTES),
    )(block_e, n_valid, slot_tok, h_packed, w_gu, w_gu, b_gu[:, None, :], b_gu[:, None, :], w_down,
      b_down[:, None, :])


def _route(logits):
    n_tok = logits.shape[0]
    n_assign = n_tok * TOP_K
    top_logit, top_e = lax.top_k(logits, TOP_K)
    top_p = jax.nn.softmax(top_logit, axis=-1)
    flat_e = top_e.reshape(-1)
    onehot = (flat_e[:, None] == jnp.arange(N_EXPERTS)[None, :]).astype(jnp.int32)
    csum = jnp.cumsum(onehot, axis=0)
    counts = csum[-1]
    rank = jnp.take_along_axis(csum, flat_e[:, None], axis=1)[:, 0] - 1
    padded = (counts + MOE_TM - 1) // MOE_TM * MOE_TM
    pad_end = jnp.cumsum(padded)
    pad_start = pad_end - padded
    slot = (pad_start[flat_e] + rank).astype(jnp.int32)
    n_blocks = -(-(n_assign + N_EXPERTS * (MOE_TM - 1)) // MOE_TM)
    n_slots = n_blocks * MOE_TM
    n_valid = (pad_end[-1] // MOE_TM).astype(jnp.int32)
    blk = jnp.arange(n_blocks, dtype=jnp.int32)
    block_e = jnp.searchsorted(pad_end, jnp.minimum(blk, n_valid - 1) * MOE_TM, side='right').astype(jnp.int32)
    block_e = jnp.minimum(block_e, N_EXPERTS - 1)
    order = jnp.argsort(flat_e, stable=True).astype(jnp.int32)
    grp_start = jnp.cumsum(counts) - counts
    s = jnp.arange(n_slots, dtype=jnp.int32)
    e_slot = jnp.repeat(block_e, MOE_TM)
    r = s - pad_start[e_slot]
    src = jnp.minimum(grp_start[e_slot] + r, n_assign - 1)
    slot_tok = jnp.where(r < counts[e_slot], order[src] // TOP_K, 0).astype(jnp.int32)
    return slot, slot_tok, top_p, block_e, n_valid.reshape(1)


PREP_ROWS = 4096


def _prep_body(x_ref, w_ref, o_ref, *, t_first, t_rest):
    i = pl.program_id(0)
    j = pl.program_id(1)
    t_seq = jnp.where(i == 0, t_first, t_rest)
    x = x_ref[...]
    n = x.shape[0]
    row = lax.broadcasted_iota(jnp.int32, x.shape, 0)
    tpos = row & (t_seq - 1)
    xm = jnp.where(tpos == 0, 0.0, pltpu.roll(x, 1, axis=0))
    xp = jnp.where(tpos == t_seq - 1, 0.0, pltpu.roll(x, n - 1, axis=0))
    w = w_ref[...]
    y = xm * w[0:1] + x * w[1:2] + xp * w[2:3]
    y = y * jax.nn.sigmoid(y)
    rs = lax.rsqrt(jnp.sum(y * y, axis=-1, keepdims=True) + EPS)
    nq = QK_W // DK
    fac = jnp.where(j < nq, rs * (DK ** -0.5), jnp.where(j < 2 * nq, rs, 1.0))
    o_ref[...] = (y * fac).astype(o_ref.dtype)


def _qkv_prep(proj, conv_w, *, t_first, t_rest):
    n = proj.shape[0]
    return pl.pallas_call(
        functools.partial(_prep_body, t_first=t_first, t_rest=t_rest),
        name="qkv_prep",
        grid=(n // PREP_ROWS, QKV_W // DK),
        in_specs=[pl.BlockSpec((PREP_ROWS, DK), lambda i, j: (i, j)),
                  pl.BlockSpec((3, DK), lambda i, j: (0, j))],
        out_specs=pl.BlockSpec((PREP_ROWS, DK), lambda i, j: (i, j)),
        out_shape=jax.ShapeDtypeStruct((n, QKV_W), BF16),
        compiler_params=pltpu.CompilerParams(dimension_semantics=("parallel", "parallel"),
                                             vmem_limit_bytes=VMEM_LIMIT_BYTES),
    )(proj, conv_w)


PROMPT_SUB = 4
SOLVE_WAYS = 8


def _mmul(a, b):
    return jnp.dot(a, b, preferred_element_type=F32)


def _delta_body(q_ref, k_ref, v_ref, z_ref, cols_ref, rows_ref, nw_ref, s0f_ref, s0b_ref, o_prev_ref,
                o_ref, sf_ref, sb_ref, acc_ref, s_ref, zz_ref, pr_ref, kt_ref, *, n_sub, n_chunks, zero_init):
    del o_prev_ref
    n_all = n_sub * n_chunks
    ways = min(SOLVE_WAYS, n_all)
    lane = lax.broadcasted_iota(jnp.int32, (CHUNK, 2 * CHUNK), 1)
    row = lax.broadcasted_iota(jnp.int32, (CHUNK, 2 * CHUNK), 0)
    pos = lane & (CHUNK - 1)
    left = lane < CHUNK
    left1 = lax.broadcasted_iota(jnp.int32, (1, 2 * CHUNK), 1) < CHUNK
    eye2 = (pos == row).astype(F32)
    incl = (pos <= row, pos >= row)
    strict = (pos < row, pos > row)
    same8 = (row >> 3) == (pos >> 3)
    same16 = (row >> 4) == (pos >> 4)
    same32 = (row >> 5) == (pos >> 5)
    nt = (((1,), (1,)), ((), ()))
    tn = (((0,), (0,)), ((), ()))

    def split(m):
        return jnp.where(left, m, 0.0), jnp.where(left, 0.0, m)

    def bdiag(m):
        return jnp.concatenate(split(m), axis=0).astype(BF16)

    def pmm(a, b):
        return _mmul(a.astype(BF16), bdiag(b))

    def inv_unit_tri(lms):
        d8 = [jnp.where(same8, lm, 0.0) for lm in lms]
        xs = [eye2 - d for d in d8]
        d2 = [pmm(d, d) for d in d8]
        xs = [x + pmm(x, d) for x, d in zip(xs, d2)]
        d4 = [pmm(d, d) for d in d2]
        xs = [x + pmm(x, d) for x, d in zip(xs, d4)]
        for keep in (same16 & ~same8, same32 & ~same16, ~same32):
            ys = [pmm(jnp.where(keep, lm, 0.0), x) for lm, x in zip(lms, xs)]
            xs = [x - pmm(x, y) for x, y in zip(xs, ys)]
        return xs

    def chunk_terms(c, d):
        r0 = pl.multiple_of(c * CHUNK, CHUNK)
        cols = cols_ref[pl.ds(r0, CHUNK), :]
        rows = rows_ref[c]

        def lanes_of(col):
            return jnp.broadcast_to(cols[:, col:col + 1], (CHUNK, 2 * CHUNK))
        gcol = [lanes_of(2 * d + h) for h in (0, 1)]
        bcol = [lanes_of(4 + 2 * d + h) for h in (0, 1)]
        gl = [rows[4 + 2 * d + h:5 + 2 * d + h] for h in (0, 1)]
        return r0, rows, gcol, bcol, gl

    acc_ref[...] = jnp.zeros_like(acc_ref)
    rep = s0f_ref.shape[1]
    if zero_init:
        s_ref[...] = jnp.zeros_like(s_ref)
    else:
        for sub in range(n_sub):
            for h in range(rep):
                s_ref[2 * rep * sub + h] = s0f_ref[sub, h]
                s_ref[2 * rep * sub + rep + h] = s0b_ref[sub, h]

    def solve(i, carry):
        chunks = [i + part * (n_all // ways) for part in range(ways)]
        qcs, kcs, vcs, kkms, qkms = [], [], [], [], []
        for c in chunks:
            r0 = pl.multiple_of(c * CHUNK, CHUNK)
            kb = k_ref[pl.ds(r0, CHUNK), :]
            qb = q_ref[pl.ds(r0, CHUNK), :]
            kk2 = jnp.concatenate([kb, kb], axis=0)
            kcs.append(kb.astype(F32))
            qcs.append(qb.astype(F32))
            vcs.append(v_ref[pl.ds(r0, CHUNK), :])
            kkms.append(lax.dot_general(kb, kk2, nt, preferred_element_type=F32))
            qkms.append(lax.dot_general(qb, kk2, nt, preferred_element_type=F32))
            kt_ref[c] = jnp.transpose(kcs[-1]).astype(BF16)
        chains = [(j, d) for j in range(ways) for d in (0, 1)]
        lms, rhs2s, a2s, decays = [], [], [], []
        for j, d in chains:
            c = chunks[j]
            _, rows, gcol, bcol, gl = chunk_terms(c, d)
            gcol2 = jnp.where(left, gcol[0], gcol[1])
            bcol2 = jnp.where(left, bcol[0], bcol[1])
            grow2 = jnp.where(left1, rows[2 * d:2 * d + 1], rows[2 * d + 1:2 * d + 2])
            dm = jnp.exp(jnp.where(incl[d], gcol2 - grow2, -jnp.inf))
            a2s.append(qkms[j] * dm)
            eg = [jnp.exp(g) for g in gcol]
            decays.append((eg, [jnp.exp(gl[h] - gcol[h]) for h in (0, 1)]))
            lms.append(jnp.where(strict[d], kkms[j] * dm * bcol2, 0.0))
            rhs = [jnp.concatenate([vcs[j][:, h * DV:(h + 1) * DV] * bcol[h], kcs[j] * (bcol[h] * eg[h])], axis=1)
                   for h in (0, 1)]
            rhs2s.append(jnp.concatenate(rhs, axis=0).astype(BF16))
        t2s = inv_unit_tri(lms)
        sols = [[_mmul(th.astype(BF16), rhs2) for th in split(t2)]
                for t2, rhs2 in zip(t2s, rhs2s)]
        aws = [[_mmul(ah.astype(BF16), jnp.concatenate(sol, axis=0).astype(BF16)) for ah in split(a2)]
               for a2, sol in zip(a2s, sols)]
        for (j, d), sol, aw, (eg, ek) in zip(chains, sols, aws, decays):
            c = chunks[j]
            zz, pr = [], []
            for h in (0, 1):
                zz += [sol[h][:, :DV] * ek[h], sol[h][:, DV:] * ek[h]]
                pr += [qcs[j] * eg[h] - aw[h][:, DV:], aw[h][:, :DV]]
            zz_ref[d, c] = jnp.concatenate(zz, axis=1).astype(BF16)
            pr_ref[d, c] = jnp.concatenate(pr, axis=1).astype(BF16)
        return carry

    lax.fori_loop(0, n_all // ways, solve, 0)

    def step(i, carry):
        lanes = [(sub, d) for sub in range(n_sub) for d in (0, 1)]
        where = [sub * n_chunks + (i if d == 0 else n_chunks - 1 - i) for sub, d in lanes]
        ngs, prs, gls = [], [], []
        for c, (_, d) in zip(where, lanes):
            rows = rows_ref[c]
            gls.append([rows[4 + 2 * d + h:5 + 2 * d + h] for h in (0, 1)])
            ngs.append(_mmul(kt_ref[c], zz_ref[d, c]))
            prs.append(pr_ref[d, c])
        streams = [(n, h) for n in range(len(lanes)) for h in (0, 1)]
        outs = []
        for n, h in streams:
            pg = jnp.concatenate([prs[n][:, 2 * h * DV:(2 * h + 1) * DV],
                                  ngs[n][:, (2 * h + 1) * DV:(2 * h + 2) * DV].astype(BF16)], axis=0)
            outs.append(_mmul(pg, s_ref[2 * n + h].astype(BF16)))
        for (n, h), out in zip(streams, outs):
            s = 2 * n + h
            r0 = pl.multiple_of(where[n] * CHUNK, CHUNK)
            acc_ref[pl.ds(r0, CHUNK), h * DV:(h + 1) * DV] += (
                out[:CHUNK] + prs[n][:, (2 * h + 1) * DV:(2 * h + 2) * DV].astype(F32))
            s_ref[s] = s_ref[s] * jnp.exp(gls[n][h]) - out[CHUNK:] + ngs[n][:, 2 * h * DV:(2 * h + 1) * DV]
        return carry

    lax.fori_loop(0, n_chunks, step, 0)

    nw = nw_ref[...]

    def finish(j, carry):
        r0 = pl.multiple_of(j * CHUNK, CHUNK)
        o = acc_ref[pl.ds(r0, CHUNK), :]
        z = z_ref[pl.ds(r0, CHUNK), :]
        for h in (0, 1):
            oh = o[:, h * DV:(h + 1) * DV]
            zh = z[:, h * DV:(h + 1) * DV]
            y = oh * lax.rsqrt(jnp.mean(oh * oh, axis=-1, keepdims=True) + EPS) * nw * (zh * jax.nn.sigmoid(zh))
            o_ref[pl.ds(r0, CHUNK), h * DV:(h + 1) * DV] = y.astype(o_ref.dtype)
        return carry

    lax.fori_loop(0, n_all, finish, 0)
    for sub in range(n_sub):
        for h in range(rep):
            sf_ref[sub, h] = s_ref[2 * rep * sub + h]
            sb_ref[sub, h] = s_ref[2 * rep * sub + rep + h]


def _deltanet(qkvn, proj, cols, rows, norm_w, s0f, s0b, o_prev, *, n_seq, t_seq, n_sub, row_blk0, zero_init):
    n_chunks = t_seq // CHUNK
    t_blk = n_sub * t_seq
    c_blk = n_sub * n_chunks
    nq = QK_W // DK
    z_blk0 = QKV_W // (2 * DV)
    rep = NV_HEADS // NK_HEADS
    state_spec = pl.BlockSpec((n_sub, None, rep, DK, DV), lambda b, kh: (b, 0, kh, 0, 0))
    in_specs = [
        pl.BlockSpec((t_blk, DK), lambda b, kh: (row_blk0 + b, kh)),
        pl.BlockSpec((t_blk, DK), lambda b, kh: (row_blk0 + b, nq + kh)),
        pl.BlockSpec((t_blk, rep * DV), lambda b, kh: (row_blk0 + b, nq + kh)),
        pl.BlockSpec((t_blk, rep * DV), lambda b, kh: (row_blk0 + b, z_blk0 + kh)),
        pl.BlockSpec((None, t_blk, 8), lambda b, kh: (kh, row_blk0 + b, 0)),
        pl.BlockSpec((None, c_blk, 8, 2 * CHUNK), lambda b, kh: (kh, row_blk0 + b, 0, 0)),
        pl.BlockSpec((1, DV), lambda b, kh: (0, 0)),
        pl.BlockSpec((n_sub, rep, DK, DV), lambda b, kh: (0 if zero_init else b, kh, 0, 0)),
        pl.BlockSpec((n_sub, rep, DK, DV), lambda b, kh: (0 if zero_init else b, kh, 0, 0)),
        pl.BlockSpec(memory_space=pl.ANY),
    ]
    out_specs = [pl.BlockSpec((t_blk, rep * DV), lambda b, kh: (row_blk0 + b, kh)), state_spec, state_spec]
    out_shape = [
        jax.ShapeDtypeStruct((qkvn.shape[0], V_W), BF16),
        jax.ShapeDtypeStruct((n_seq, DEPTH, NV_HEADS, DK, DV), F32),
        jax.ShapeDtypeStruct((n_seq, DEPTH, NV_HEADS, DK, DV), F32),
    ]
    args = [qkvn, qkvn, qkvn, proj, cols, rows, norm_w.reshape(1, DV), s0f, s0b, o_prev]
    return pl.pallas_call(
        functools.partial(_delta_body, n_sub=n_sub, n_chunks=n_chunks, zero_init=zero_init),
        grid=(n_seq // n_sub, NK_HEADS),
        input_output_aliases={len(args) - 1: 0},
        name="deltanet",
        in_specs=in_specs,
        out_specs=out_specs,
        out_shape=out_shape,
        scratch_shapes=[pltpu.VMEM((t_blk, rep * DV), F32), pltpu.VMEM((2 * rep * n_sub, DK, DV), F32),
                        pltpu.VMEM((2, c_blk, CHUNK, 2 * rep * DV), BF16),
                        pltpu.VMEM((2, c_blk, CHUNK, 2 * rep * DV), BF16),
                        pltpu.VMEM((c_blk, DK, CHUNK), BF16)],
        compiler_params=pltpu.CompilerParams(dimension_semantics=("parallel", "parallel"),
                                             vmem_limit_bytes=VMEM_LIMIT_BYTES),
    )(*args)


def _decay_tables(ab, a_log, dt_bias):
    n = ab.shape[0]
    nh2 = 2 * NV_HEADS
    rep = NV_HEADS // NK_HEADS
    beta = jax.nn.sigmoid(ab[:, :nh2])
    g = -jnp.exp(a_log.reshape(nh2)) * jax.nn.softplus(ab[:, nh2:] + dt_bias.reshape(nh2))
    gch = g.reshape(n // CHUNK, CHUNK, 2, NV_HEADS)
    gcf = jnp.cumsum(gch[:, :, 0], axis=1)
    gcb = jnp.flip(jnp.cumsum(jnp.flip(gch[:, :, 1], 1), axis=1), 1)
    gc = jnp.stack([gcf, gcb], axis=2)
    gl = jnp.stack([gcf[:, -1], gcb[:, 0]], axis=1)

    def per_kh(a):
        a = a.reshape(a.shape[:-2] + (2, NK_HEADS, rep))
        return jnp.moveaxis(a, -3, -2).reshape(a.shape[:-3] + (NK_HEADS, 2 * rep))
    gc_k = per_kh(gc)
    cols = jnp.concatenate([gc_k.reshape(n, NK_HEADS, 2 * rep), per_kh(beta.reshape(n, 2, NV_HEADS))], axis=-1)
    cols = jnp.transpose(cols, (1, 0, 2))
    gr = jnp.transpose(gc_k, (2, 0, 3, 1))
    gr = jnp.concatenate([gr, gr], axis=-1)
    glr = jnp.broadcast_to(jnp.transpose(per_kh(gl), (1, 0, 2))[..., None], gr.shape)
    rows = jnp.concatenate([gr, glr], axis=2)
    return cols, rows


TOK_TILE = 512
MIX_TILE = 256
COMBINE_TILE = 128


def _group_of_tile(i, tile, n_prompt, t_sample):
    first = n_prompt // tile
    return jnp.where(i < first, 0, 1 + (i - first) // (t_sample // tile))


def _token_specs(tile, n_prompt, t_sample, d):
    first = n_prompt // tile
    per_seq = t_sample // tile
    return [
        pl.BlockSpec((tile, d), lambda i, *_: (jnp.minimum(i, first - 1), 0)),
        pl.BlockSpec((tile, d), lambda i, *_: (jnp.maximum(i - first, 0), 0)),
        pl.BlockSpec((tile, d), lambda i, *_: (jnp.maximum(i - first, 0) % per_seq, 0)),
        pl.BlockSpec((None, 1, 6 * d), lambda i, *_: (_group_of_tile(i, tile, n_prompt, t_sample), 0, 0)),
    ]


def _token_rows(i, first, xp_ref, xs_ref, pos_ref):
    return jnp.where(i < first, xp_ref[...], xs_ref[...] + pos_ref[...])


def _modulate_body(xp_ref, xs_ref, pos_ref, mod_ref, h_ref, *, first):
    d = h_ref.shape[-1]
    x = _token_rows(pl.program_id(0), first, xp_ref, xs_ref, pos_ref)
    h_ref[...] = (x * (1.0 + mod_ref[:, d:2 * d]) + mod_ref[:, 0:d]).astype(h_ref.dtype)


def _modulate(xp, xs, pos, mod, *, t_sample):
    n_p, d = xp.shape
    n = n_p + xs.shape[0]
    return pl.pallas_call(
        functools.partial(_modulate_body, first=n_p // TOK_TILE),
        grid=(n // TOK_TILE,),
        in_specs=_token_specs(TOK_TILE, n_p, t_sample, d),
        out_specs=pl.BlockSpec((TOK_TILE, d), lambda i: (i, 0)),
        out_shape=jax.ShapeDtypeStruct((n, d), BF16),
        compiler_params=pltpu.CompilerParams(dimension_semantics=("parallel",), vmem_limit_bytes=VMEM_LIMIT_BYTES),
        name="modulate",
    )(xp, xs, pos, mod)


def _gconv_body(ub_ref, uc_ref, ux_ref, w_ref, o_ref, *, t_first, t_rest):
    t_seq = jnp.where(pl.program_id(0) == 0, t_first, t_rest)
    x = uc_ref[...] * ux_ref[...]
    n = x.shape[0]
    tpos = lax.broadcasted_iota(jnp.int32, x.shape, 0) & (t_seq - 1)
    xm = jnp.where(tpos == 0, 0.0, pltpu.roll(x, 1, axis=0))
    xp = jnp.where(tpos == t_seq - 1, 0.0, pltpu.roll(x, n - 1, axis=0))
    w = w_ref[...]
    o_ref[...] = (ub_ref[...] * (xm * w[0:1] + x * w[1:2] + xp * w[2:3])).astype(o_ref.dtype)


def _gated_conv(proj, conv_w, col0, *, t_first, t_rest):
    n = proj.shape[0]
    nblk = CONV_CH // DK
    b0 = col0 // DK
    return pl.pallas_call(
        functools.partial(_gconv_body, t_first=t_first, t_rest=t_rest),
        grid=(n // PREP_ROWS, nblk),
        in_specs=[pl.BlockSpec((PREP_ROWS, DK), lambda i, j: (i, b0 + j)),
                  pl.BlockSpec((PREP_ROWS, DK), lambda i, j: (i, b0 + nblk + j)),
                  pl.BlockSpec((PREP_ROWS, DK), lambda i, j: (i, b0 + 2 * nblk + j)),
                  pl.BlockSpec((3, DK), lambda i, j: (0, j))],
        out_specs=pl.BlockSpec((PREP_ROWS, DK), lambda i, j: (i, j)),
        out_shape=jax.ShapeDtypeStruct((n, CONV_CH), BF16),
        compiler_params=pltpu.CompilerParams(dimension_semantics=("parallel", "parallel"),
                                             vmem_limit_bytes=VMEM_LIMIT_BYTES),
        name="gated_conv",
    )(proj, proj, proj, conv_w)


BR_TM = 512
BR_TN = 512


def _branch_body(o_ref, g_ref, wa_ref, wb_ref, ra_ref, rb_ref, m_ref, wa_bf, wb_bf):
    @pl.when(pl.program_id(1) == 0)
    def _():
        wa_bf[...] = wa_ref[...].astype(BF16)
        wb_bf[...] = wb_ref[...].astype(BF16)
    ya = jnp.dot(o_ref[...], wa_bf[...], preferred_element_type=F32)
    yb = jnp.dot(g_ref[...], wb_bf[...], preferred_element_type=F32)
    m_ref[...] = (jax.nn.sigmoid(ra_ref[...]) * ya + jax.nn.sigmoid(rb_ref[...]) * yb).astype(m_ref.dtype)


def _branch_merge(o, g, w_a, w_b, proj, ra_col, rb_col):
    n = o.shape[0]
    d = w_a.shape[1]
    el = pl.Element
    return pl.pallas_call(
        _branch_body,
        grid=(d // BR_TN, n // BR_TM),
        in_specs=[pl.BlockSpec((BR_TM, o.shape[1]), lambda j, i: (i, 0)),
                  pl.BlockSpec((BR_TM, g.shape[1]), lambda j, i: (i, 0)),
                  pl.BlockSpec((w_a.shape[0], BR_TN), lambda j, i: (0, j)),
                  pl.BlockSpec((w_b.shape[0], BR_TN), lambda j, i: (0, j)),
                  pl.BlockSpec((el(BR_TM), el(BR_TN)),
                               lambda j, i: (i * BR_TM, pl.multiple_of(ra_col + j * BR_TN, DK))),
                  pl.BlockSpec((el(BR_TM), el(BR_TN)),
                               lambda j, i: (i * BR_TM, pl.multiple_of(rb_col + j * BR_TN, DK)))],
        out_specs=pl.BlockSpec((BR_TM, BR_TN), lambda j, i: (i, j)),
        out_shape=jax.ShapeDtypeStruct((n, d), BF16),
        scratch_shapes=[pltpu.VMEM((w_a.shape[0], BR_TN), BF16), pltpu.VMEM((w_b.shape[0], BR_TN), BF16)],
        compiler_params=pltpu.CompilerParams(dimension_semantics=("arbitrary", "arbitrary"),
                                             vmem_limit_bytes=VMEM_LIMIT_BYTES),
        name="branch_merge",
    )(o, g, w_a, w_b, proj, proj)


def _ln_rows(v, g, b):
    mu = jnp.mean(v, axis=-1, keepdims=True)
    c = v - mu
    return c * lax.rsqrt(jnp.mean(c * c, axis=-1, keepdims=True) + EPS) * g + b


def _mix_body(m_ref, wo_ref, xp_ref, xs_ref, pos_ref, mod_ref, g_ref, b_ref, rw_ref, rb_ref,
              x1_ref, h2_ref, lg_ref, *, first):
    d = x1_ref.shape[-1]
    x = _token_rows(pl.program_id(0), first, xp_ref, xs_ref, pos_ref)
    mixed = jnp.dot(m_ref[...], wo_ref[...], preferred_element_type=F32)
    x1 = _ln_rows(DEEPNORM_ALPHA * x + mod_ref[:, 2 * d:3 * d] * mixed, g_ref[...], b_ref[...])
    h2 = x1 * (1.0 + mod_ref[:, 4 * d:5 * d]) + mod_ref[:, 3 * d:4 * d]
    x1_ref[...] = x1
    h2_ref[...] = _pack_bf16_pairs(h2)
    lg_ref[...] = jnp.dot(h2, rw_ref[...], precision=lax.Precision.HIGHEST, preferred_element_type=F32) + rb_ref[...]


def _mix_norm_route(m, w_o_bf, xp, xs, pos, mod, ln_g, ln_b, router_w, router_b, *, t_sample):
    n, d = m.shape
    n_p = xp.shape[0]
    ne = router_w.shape[1]
    const = lambda i: (0, 0)
    return pl.pallas_call(
        functools.partial(_mix_body, first=n_p // MIX_TILE),
        grid=(n // MIX_TILE,),
        in_specs=[pl.BlockSpec((MIX_TILE, d), lambda i: (i, 0)), pl.BlockSpec((d, d), const)]
        + _token_specs(MIX_TILE, n_p, t_sample, d)
        + [pl.BlockSpec((1, d), const), pl.BlockSpec((1, d), const),
           pl.BlockSpec((d, ne), const), pl.BlockSpec((1, ne), const)],
        out_specs=[pl.BlockSpec((MIX_TILE, d), lambda i: (i, 0)), pl.BlockSpec((MIX_TILE, d // 2), lambda i: (i, 0)),
                   pl.BlockSpec((MIX_TILE, ne), lambda i: (i, 0))],
        out_shape=[jax.ShapeDtypeStruct((n, d), F32), jax.ShapeDtypeStruct((n, d // 2), jnp.uint32),
                   jax.ShapeDtypeStruct((n, ne), F32)],
        compiler_params=pltpu.CompilerParams(dimension_semantics=("parallel",), vmem_limit_bytes=VMEM_LIMIT_BYTES),
        name="mix_norm_route",
    )(m, w_o_bf, xp, xs, pos, mod, ln_g.reshape(1, d), ln_b.reshape(1, d), router_w, router_b.reshape(1, ne))


def _combine_body(slot_ref, y_hbm, p_ref, x1_ref, mod_ref, g_ref, b_ref, op_ref, os_ref, buf, sem, *,
                  n_tiles, first):
    tt = COMBINE_TILE
    d = x1_ref.shape[-1]
    i = pl.program_id(0)

    def row_copy(src_row, dst_row, which):
        return pltpu.make_async_copy(y_hbm.at[pl.ds(src_row, 1), :], buf.at[which, pl.ds(dst_row, 1), :],
                                     sem.at[which])

    def issue(tile, which):
        base = tile * (tt * TOP_K)

        def body(t, carry):
            for k in range(TOP_K):
                row_copy(slot_ref[base + t * TOP_K + k], k * tt + t, which).start(priority=k % 2)
            return carry
        lax.fori_loop(0, tt, body, 0)

    @pl.when(i == 0)
    def _():
        issue(0, 0)

    @pl.when(i + 1 < n_tiles)
    def _():
        issue(i + 1, (i + 1) % 2)

    cur = i % 2
    pltpu.make_async_copy(y_hbm.at[pl.ds(0, TOP_K * tt), :], buf.at[cur], sem.at[cur]).wait()
    p = p_ref[...]
    ff = p[:, 0:1] * buf[cur, 0:tt, :]
    for k in range(1, TOP_K):
        ff = ff + p[:, k:k + 1] * buf[cur, k * tt:(k + 1) * tt, :]
    x2 = _ln_rows(DEEPNORM_ALPHA * x1_ref[...] + mod_ref[:, 5 * d:6 * d] * ff, g_ref[...], b_ref[...])

    @pl.when(i < first)
    def _():
        op_ref[...] = x2

    @pl.when(i >= first)
    def _():
        os_ref[...] = x2


def _combine_norm(slot, y, top_p, x1, mod, ln_g, ln_b, *, n_prompt, t_sample):
    n, d = x1.shape
    tt = COMBINE_TILE
    n_tiles = n // tt
    first = n_prompt // tt
    grid_spec = pltpu.PrefetchScalarGridSpec(
        num_scalar_prefetch=1,
        grid=(n_tiles,),
        in_specs=[pl.BlockSpec(memory_space=pl.ANY),
                  pl.BlockSpec((tt, TOP_K), lambda i, s: (i, 0)),
                  pl.BlockSpec((tt, d), lambda i, s: (i, 0)),
                  pl.BlockSpec((None, 1, 6 * d), lambda i, s: (_group_of_tile(i, tt, n_prompt, t_sample), 0, 0)),
                  pl.BlockSpec((1, d), lambda i, s: (0, 0)),
                  pl.BlockSpec((1, d), lambda i, s: (0, 0))],
        out_specs=[pl.BlockSpec((tt, d), lambda i, s: (jnp.minimum(i, first - 1), 0)),
                   pl.BlockSpec((tt, d), lambda i, s: (jnp.maximum(i - first, 0), 0))],
        scratch_shapes=[pltpu.VMEM((2, TOP_K * tt, d), F32), pltpu.SemaphoreType.DMA((2,))],
    )
    return pl.pallas_call(
        functools.partial(_combine_body, n_tiles=n_tiles, first=first),
        grid_spec=grid_spec,
        out_shape=[jax.ShapeDtypeStruct((n_prompt, d), F32), jax.ShapeDtypeStruct((n - n_prompt, d), F32)],
        compiler_params=pltpu.CompilerParams(dimension_semantics=("arbitrary",), vmem_limit_bytes=VMEM_LIMIT_BYTES),
        name="combine_norm",
    )(slot, y, top_p, x1, mod, ln_g.reshape(1, d), ln_b.reshape(1, d))


def _grid_pos_embed(rows, dim):
    quarter = dim // 4
    freqs = 1.0 / (POS_BASE ** (jnp.arange(quarter, dtype=F32) / quarter))

    def axis_embed(n):
        ang = jnp.arange(n, dtype=F32)[:, None] * freqs[None, :]
        return jnp.concatenate([jnp.sin(ang), jnp.cos(ang)], axis=-1)
    er = axis_embed(rows)
    ec = axis_embed(GRID_W)
    pos = jnp.concatenate([jnp.broadcast_to(er[:, None, :], (rows, GRID_W, dim // 2)),
                           jnp.broadcast_to(ec[None, :, :], (rows, GRID_W, dim // 2))], axis=-1)
    return pos.reshape(rows * GRID_W, dim)


def kernel(x_prompt, x_sample, c, state_fwd, state_bwd, c_ctx, w_mod, b_mod, w_in, conv_qkv, a_log, dt_bias,
           norm_o, w_a_out, conv_b, w_b_out, w_o, ln1_g, ln1_b, router_w, router_b, w_gu, b_gu, w_down, b_down,
           ln2_g, ln2_b):
    nb_p, t_p, d = x_prompt.shape
    nb_s, t_s, _ = x_sample.shape
    n_p = nb_p * t_p
    n_s = nb_s * t_s
    assert n_p == PREP_ROWS and t_s == PREP_ROWS
    xp = x_prompt.reshape(n_p, d)
    xs = x_sample.reshape(n_s, d)
    pos = _grid_pos_embed(t_s // GRID_W, d)
    offs = tuple(int(o) for o in np.cumsum(IN_SIZES)[:-1])

    l = 0
    cvec = jnp.concatenate([c_ctx[None, :], c], axis=0)
    cpad = jnp.zeros((8, d), F32).at[:1 + nb_s].set(jax.nn.silu(cvec))
    mod = (_mm(cpad, w_mod[l], tm=8, tn=512)[:1 + nb_s] + b_mod[l])[:, None, :]

    h = _modulate(xp, xs, pos, mod, t_sample=t_s)
    proj = _mm(h, w_in[l], tm=2048, tn=512)

    qkvn = _qkv_prep(proj, conv_qkv[l], t_first=t_p, t_rest=t_s)
    cols, rws = _decay_tables(proj[:, offs[1]:offs[3]], a_log[l], dt_bias[l])
    zero_state = jnp.zeros((PROMPT_SUB, NV_HEADS, DK, DV), F32)
    o_all = jnp.zeros((n_p + n_s, V_W), BF16)
    o_all, sf, sb = _deltanet(qkvn, proj, cols, rws, norm_o[l], zero_state, zero_state, o_all,
                              n_seq=nb_p, t_seq=t_p, n_sub=PROMPT_SUB, row_blk0=0, zero_init=True)
    o_all, _, _ = _deltanet(qkvn, proj, cols, rws, norm_o[l], state_fwd[:, l], state_bwd[:, l], o_all,
                            n_seq=nb_s, t_seq=t_s, n_sub=1, row_blk0=n_p // t_s, zero_init=False)

    gconv = _gated_conv(proj, conv_b[l], offs[3], t_first=t_p, t_rest=t_s)
    m = _branch_merge(o_all, gconv, w_a_out[l], w_b_out[l], proj, offs[6], offs[7])
    x1, h2, logits = _mix_norm_route(m, w_o[l].astype(BF16), xp, xs, pos, mod, ln1_g[l], ln1_b[l],
                                     router_w[l], router_b[l], t_sample=t_s)
    slot, slot_tok, top_p, block_e, n_valid = _route(logits)
    y = _moe_blocks(h2, slot_tok, block_e, n_valid, w_gu[l], b_gu[l], w_down[l], b_down[l])
    x2_p, x2_s = _combine_norm(slot, y, top_p, x1, mod, ln2_g[l], ln2_b[l], n_prompt=n_p, t_sample=t_s)
    return (x2_p.reshape(nb_p, t_p, d), x2_s.reshape(nb_s, t_s, d), sf, sb)
```

```python
import functools

import jax
import jax.numpy as jnp
import numpy as np
from jax import lax
from jax.experimental import pallas as pl
from jax.experimental.pallas import tpu as pltpu

F32 = jnp.float32
BF16 = jnp.bfloat16

D_MODEL = 2048
GRID_W = 64
NK_HEADS = 16
NV_HEADS = 32
DK = 128
DV = 128
QK_W = NK_HEADS * DK
V_W = NV_HEADS * DV
QKV_W = 2 * QK_W + V_W
CHUNK = 64
CONV_CH = D_MODEL
IN_SIZES = (QKV_W, V_W, 2 * NV_HEADS, 2 * NV_HEADS, CONV_CH, CONV_CH, CONV_CH, D_MODEL, D_MODEL)
N_EXPERTS = 32
TOP_K = 4
D_FF = D_MODEL
SWIGLU_LIMIT = 7.0
SWIGLU_ALPHA = 1.702
DEPTH = 1
DEEPNORM_ALPHA = (2 * DEPTH) ** 0.25
EPS = 1e-6
POS_BASE = 10000.0

VMEM_LIMIT_BYTES = 56 * 1024 * 1024


def _mm_body(x_ref, w_ref, o_ref):
    o_ref[...] = jnp.dot(x_ref[...].astype(BF16), w_ref[...].astype(BF16), preferred_element_type=F32)


def _mm(x, w, *, tm, tn):
    m, k = x.shape
    n = w.shape[1]
    assert m % tm == 0
    return pl.pallas_call(
        _mm_body,
        name="matmul",
        grid=(m // tm, pl.cdiv(n, tn)),
        in_specs=[pl.BlockSpec((tm, k), lambda i, j: (i, 0)),
                  pl.BlockSpec((k, tn), lambda i, j: (0, j))],
        out_specs=pl.BlockSpec((tm, tn), lambda i, j: (i, j)),
        out_shape=jax.ShapeDtypeStruct((m, n), F32),
        compiler_params=pltpu.CompilerParams(dimension_semantics=("parallel", "parallel"),
                                             vmem_limit_bytes=VMEM_LIMIT_BYTES),
    )(x, w)


MOE_TM = 512
MOE_TF = 512


def _pack_bf16_pairs(x):
    n = x.shape[1] // 2
    lo = lax.bitcast_convert_type(x[:, :n].astype(BF16).astype(F32), jnp.uint32)
    hi = lax.bitcast_convert_type(x[:, n:].astype(BF16).astype(F32), jnp.uint32)
    return (lo >> 16) | hi


def _unpack_bf16_pairs(w):
    lo = lax.bitcast_convert_type(w << 16, F32)
    hi = lax.bitcast_convert_type(w & jnp.uint32(0xFFFF0000), F32)
    return jnp.concatenate([lo, hi], axis=1).astype(BF16)


def _moe_body(be_ref, nv_ref, tok_ref, h_hbm, wg_ref, wu_ref, bg_ref, bu_ref, wd_ref, bd_ref, o_ref,
              acc_ref, xbuf, sem):
    i = pl.program_id(0)
    f = pl.program_id(1)
    nf = pl.num_programs(1)
    n_valid = nv_ref[0]
    valid = i < n_valid
    part = MOE_TM // (D_FF // MOE_TF)

    def row_copy(slot_row, buf_row, which):
        return pltpu.make_async_copy(h_hbm.at[pl.ds(tok_ref[slot_row], 1), :], xbuf.at[which, pl.ds(buf_row, 1), :],
                                     sem.at[which])

    def block_wait(which):
        pltpu.make_async_copy(h_hbm.at[pl.ds(0, MOE_TM), :], xbuf.at[which], sem.at[which]).wait()

    @pl.when(jnp.logical_and(jnp.logical_and(i == 0, f == 0), valid))
    def _():
        def body(r, carry):
            row_copy(r, r, 0).start()
            return carry
        lax.fori_loop(0, MOE_TM, body, 0)

    @pl.when(f == 0)
    def _():
        acc_ref[...] = jnp.zeros_like(acc_ref)

    @pl.when(jnp.logical_and(f == 0, valid))
    def _():
        block_wait(i % 2)

    @pl.when(valid)
    def _():
        nxt = jnp.minimum(i + 1, n_valid - 1)
        for r in range(part):
            row_copy(nxt * MOE_TM + f * part + r, f * part + r, (i + 1) % 2).start()
        x = _unpack_bf16_pairs(xbuf[i % 2])
        g = jnp.dot(x, wg_ref[...].astype(BF16), preferred_element_type=F32) + bg_ref[...]
        u = jnp.dot(x, wu_ref[...].astype(BF16), preferred_element_type=F32) + bu_ref[...]
        gate = jnp.minimum(g, SWIGLU_LIMIT)
        up = jnp.clip(u, -SWIGLU_LIMIT, SWIGLU_LIMIT)
        act = (up + 1.0) * gate * jax.nn.sigmoid(SWIGLU_ALPHA * gate)
        acc_ref[...] += jnp.dot(act.astype(BF16), wd_ref[...].astype(BF16), preferred_element_type=F32)

    @pl.when(f == nf - 1)
    def _():
        o_ref[...] = jnp.where(valid, acc_ref[...] + bd_ref[...], 0.0)

    @pl.when(jnp.logical_and(f == nf - 1, i == n_valid - 1))
    def _():
        block_wait((i + 1) % 2)


def _moe_blocks(h_packed, slot_tok, block_e, n_valid, w_gu, b_gu, w_down, b_down):
    n_slots = slot_tok.shape[0]
    d = w_down.shape[-1]
    n_blocks = n_slots // MOE_TM
    nf = D_FF // MOE_TF

    def f_eff(i, f, nv):
        return jnp.where(i < nv[0], f, nf - 1)

    grid_spec = pltpu.PrefetchScalarGridSpec(
        num_scalar_prefetch=3,
        grid=(n_blocks, nf),
        in_specs=[
            pl.BlockSpec(memory_space=pl.ANY),
            pl.BlockSpec((None, d, MOE_TF), lambda i, f, be, nv, tok: (be[i], 0, f_eff(i, f, nv))),
            pl.BlockSpec((None, d, MOE_TF), lambda i, f, be, nv, tok: (be[i], 0, nf + f_eff(i, f, nv))),
            pl.BlockSpec((None, 1, MOE_TF), lambda i, f, be, nv, tok: (be[i], 0, f_eff(i, f, nv))),
            pl.BlockSpec((None, 1, MOE_TF), lambda i, f, be, nv, tok: (be[i], 0, nf + f_eff(i, f, nv))),
            pl.BlockSpec((None, MOE_TF, d), lambda i, f, be, nv, tok: (be[i], f_eff(i, f, nv), 0)),
            pl.BlockSpec((None, 1, d), lambda i, f, be, nv, tok: (be[i], 0, 0)),
        ],
        out_specs=pl.BlockSpec((MOE_TM, d), lambda i, f, be, nv, tok: (i, 0)),
        scratch_shapes=[pltpu.VMEM((MOE_TM, d), F32), pltpu.VMEM((2, MOE_TM, h_packed.shape[1]), h_packed.dtype),
                        pltpu.SemaphoreType.DMA((2,))],
    )
    return pl.pallas_call(
        _moe_body,
        name="moe_experts",
        grid_spec=grid_spec,
        out_shape=jax.ShapeDtypeStruct((n_slots, d), F32),
        compiler_params=pltpu.CompilerParams(dimension_semantics=("arbitrary", "arbitrary"),
                                             vmem_limit_bytes=VMEM_LIMIT_BYTES),
    )(block_e, n_valid, slot_tok, h_packed, w_gu, w_gu, b_gu[:, None, :], b_gu[:, None, :], w_down,
      b_down[:, None, :])


def _route(logits):
    n_tok = logits.shape[0]
    n_assign = n_tok * TOP_K
    top_logit, top_e = lax.top_k(logits, TOP_K)
    top_p = jax.nn.softmax(top_logit, axis=-1)
    flat_e = top_e.reshape(-1)
    onehot = (flat_e[:, None] == jnp.arange(N_EXPERTS)[None, :]).astype(jnp.int32)
    csum = jnp.cumsum(onehot, axis=0)
    counts = csum[-1]
    rank = jnp.take_along_axis(csum, flat_e[:, None], axis=1)[:, 0] - 1
    padded = (counts + MOE_TM - 1) // MOE_TM * MOE_TM
    pad_end = jnp.cumsum(padded)
    pad_start = pad_end - padded
    slot = (pad_start[flat_e] + rank).astype(jnp.int32)
    n_blocks = -(-(n_assign + N_EXPERTS * (MOE_TM - 1)) // MOE_TM)
    n_slots = n_blocks * MOE_TM
    n_valid = (pad_end[-1] // MOE_TM).astype(jnp.int32)
    blk = jnp.arange(n_blocks, dtype=jnp.int32)
    block_e = jnp.searchsorted(pad_end, jnp.minimum(blk, n_valid - 1) * MOE_TM, side='right').astype(jnp.int32)
    block_e = jnp.minimum(block_e, N_EXPERTS - 1)
    order = jnp.argsort(flat_e, stable=True).astype(jnp.int32)
    grp_start = jnp.cumsum(counts) - counts
    s = jnp.arange(n_slots, dtype=jnp.int32)
    e_slot = jnp.repeat(block_e, MOE_TM)
    r = s - pad_start[e_slot]
    src = jnp.minimum(grp_start[e_slot] + r, n_assign - 1)
    slot_tok = jnp.where(r < counts[e_slot], order[src] // TOP_K, 0).astype(jnp.int32)
    return slot, slot_tok, top_p, block_e, n_valid.reshape(1)


PREP_ROWS = 4096


def _prep_body(x_ref, w_ref, o_ref, *, t_first, t_rest, unit_norm):
    i = pl.program_id(0)
    j = pl.program_id(1)
    t_seq = jnp.where(i == 0, t_first, t_rest)
    x = x_ref[...]
    n = x.shape[0]
    row = lax.broadcasted_iota(jnp.int32, x.shape, 0)
    tpos = row & (t_seq - 1)
    xm = jnp.where(tpos == 0, 0.0, pltpu.roll(x, 1, axis=0))
    xp = jnp.where(tpos == t_seq - 1, 0.0, pltpu.roll(x, n - 1, axis=0))
    w = w_ref[...]
    y = xm * w[0:1] + x * w[1:2] + xp * w[2:3]
    y = y * jax.nn.sigmoid(y)
    if unit_norm:
        rs = lax.rsqrt(jnp.sum(y * y, axis=-1, keepdims=True) + EPS)
        y = y * (rs * jnp.where(j < QK_W // DK, DK ** -0.5, 1.0))
    o_ref[...] = y.astype(o_ref.dtype)


def _conv_silu(proj, conv_w, col0, width, *, t_first, t_rest, unit_norm):
    n = proj.shape[0]
    b0 = col0 // DK
    return pl.pallas_call(
        functools.partial(_prep_body, t_first=t_first, t_rest=t_rest, unit_norm=unit_norm),
        name="qk_prep" if unit_norm else "v_prep",
        grid=(n // PREP_ROWS, width // DK),
        in_specs=[pl.BlockSpec((PREP_ROWS, DK), lambda i, j: (i, b0 + j)),
                  pl.BlockSpec((3, DK), lambda i, j: (0, b0 + j))],
        out_specs=pl.BlockSpec((PREP_ROWS, DK), lambda i, j: (i, j)),
        out_shape=jax.ShapeDtypeStruct((n, width), BF16),
        compiler_params=pltpu.CompilerParams(dimension_semantics=("parallel", "parallel"),
                                             vmem_limit_bytes=VMEM_LIMIT_BYTES),
    )(proj, conv_w)


PROMPT_SUB = 4
SOLVE_WAYS = 8


def _mmul(a, b):
    return jnp.dot(a, b, preferred_element_type=F32)


def _delta_body(q_ref, k_ref, v_ref, z_ref, cols_ref, rows_ref, nw_ref, s0f_ref, s0b_ref, o_prev_ref,
                o_ref, sf_ref, sb_ref, acc_ref, s_ref, zz_ref, pr_ref, kt_ref, *, n_sub, n_chunks, zero_init):
    del o_prev_ref
    n_all = n_sub * n_chunks
    ways = min(SOLVE_WAYS, n_all)
    lane = lax.broadcasted_iota(jnp.int32, (CHUNK, 2 * CHUNK), 1)
    row = lax.broadcasted_iota(jnp.int32, (CHUNK, 2 * CHUNK), 0)
    pos = lane & (CHUNK - 1)
    left = lane < CHUNK
    left1 = lax.broadcasted_iota(jnp.int32, (1, 2 * CHUNK), 1) < CHUNK
    eye2 = (pos == row).astype(F32)
    incl = (pos <= row, pos >= row)
    strict = (pos < row, pos > row)
    same8 = (row >> 3) == (pos >> 3)
    same16 = (row >> 4) == (pos >> 4)
    same32 = (row >> 5) == (pos >> 5)
    nt = (((1,), (1,)), ((), ()))
    tn = (((0,), (0,)), ((), ()))

    def split(m):
        return jnp.where(left, m, 0.0), jnp.where(left, 0.0, m)

    def bdiag(m):
        return jnp.concatenate(split(m), axis=0).astype(BF16)

    def pmm(a, b):
        return _mmul(a.astype(BF16), bdiag(b))

    def inv_unit_tri(lms):
        d8 = [jnp.where(same8, lm, 0.0) for lm in lms]
        xs = [eye2 - d for d in d8]
        d2 = [pmm(d, d) for d in d8]
        xs = [x + pmm(x, d) for x, d in zip(xs, d2)]
        d4 = [pmm(d, d) for d in d2]
        xs = [x + pmm(x, d) for x, d in zip(xs, d4)]
        for keep in (same16 & ~same8, same32 & ~same16, ~same32):
            ys = [pmm(jnp.where(keep, lm, 0.0), x) for lm, x in zip(lms, xs)]
            xs = [x - pmm(x, y) for x, y in zip(xs, ys)]
        return xs

    def chunk_terms(c, d):
        r0 = pl.multiple_of(c * CHUNK, CHUNK)
        cols = cols_ref[pl.ds(r0, CHUNK), :]
        rows = rows_ref[c]

        def lanes_of(col):
            return jnp.broadcast_to(cols[:, col:col + 1], (CHUNK, 2 * CHUNK))
        gcol = [lanes_of(2 * d + h) for h in (0, 1)]
        bcol = [lanes_of(4 + 2 * d + h) for h in (0, 1)]
        gl = [rows[4 + 2 * d + h:5 + 2 * d + h] for h in (0, 1)]
        return r0, rows, gcol, bcol, gl

    assert n_chunks % 2 == 0
    rep = s0f_ref.shape[1]
    if zero_init:
        s_ref[...] = jnp.zeros_like(s_ref)
    else:
        for sub in range(n_sub):
            for h in range(rep):
                s_ref[2 * rep * sub + h] = s0f_ref[sub, h]
                s_ref[2 * rep * sub + rep + h] = s0b_ref[sub, h]

    def solve(i, carry):
        chunks = [i + part * (n_all // ways) for part in range(ways)]
        qcs, kcs, vcs, kkms, qkms = [], [], [], [], []
        for c in chunks:
            r0 = pl.multiple_of(c * CHUNK, CHUNK)
            kb = k_ref[pl.ds(r0, CHUNK), :]
            qb = q_ref[pl.ds(r0, CHUNK), :]
            kk2 = jnp.concatenate([kb, kb], axis=0)
            kcs.append(kb.astype(F32))
            qcs.append(qb.astype(F32))
            vcs.append(v_ref[pl.ds(r0, CHUNK), :])
            kkms.append(lax.dot_general(kb, kk2, nt, preferred_element_type=F32))
            qkms.append(lax.dot_general(qb, kk2, nt, preferred_element_type=F32))
            kt_ref[c] = jnp.transpose(kcs[-1]).astype(BF16)
        chains = [(j, d) for j in range(ways) for d in (0, 1)]
        lms, rhs2s, a2s, decays = [], [], [], []
        for j, d in chains:
            c = chunks[j]
            _, rows, gcol, bcol, gl = chunk_terms(c, d)
            gcol2 = jnp.where(left, gcol[0], gcol[1])
            bcol2 = jnp.where(left, bcol[0], bcol[1])
            grow2 = jnp.where(left1, rows[2 * d:2 * d + 1], rows[2 * d + 1:2 * d + 2])
            dm = jnp.exp(jnp.where(incl[d], gcol2 - grow2, -jnp.inf))
            a2s.append(qkms[j] * dm)
            eg = [jnp.exp(g) for g in gcol]
            decays.append((eg, [jnp.exp(gl[h] - gcol[h]) for h in (0, 1)]))
            lms.append(jnp.where(strict[d], kkms[j] * dm * bcol2, 0.0))
            rhs = [jnp.concatenate([vcs[j][:, h * DV:(h + 1) * DV] * bcol[h], kcs[j] * (bcol[h] * eg[h])], axis=1)
                   for h in (0, 1)]
            rhs2s.append(jnp.concatenate(rhs, axis=0).astype(BF16))
        t2s = inv_unit_tri(lms)
        sols = [[_mmul(th.astype(BF16), rhs2) for th in split(t2)]
                for t2, rhs2 in zip(t2s, rhs2s)]
        aws = [[_mmul(ah.astype(BF16), jnp.concatenate(sol, axis=0).astype(BF16)) for ah in split(a2)]
               for a2, sol in zip(a2s, sols)]
        for (j, d), sol, aw, (eg, ek) in zip(chains, sols, aws, decays):
            c = chunks[j]
            zz, pr = [], []
            for h in (0, 1):
                zz += [sol[h][:, :DV] * ek[h], sol[h][:, DV:] * ek[h]]
                pr += [qcs[j] * eg[h] - aw[h][:, DV:], aw[h][:, :DV]]
            zz_ref[d, c] = jnp.concatenate(zz, axis=1).astype(BF16)
            pr_ref[d, c] = jnp.concatenate(pr, axis=1).astype(BF16)
        return carry

    lax.fori_loop(0, n_all // ways, solve, 0)

    def step(i, carry, *, second_half):
        lanes = [(sub, d) for sub in range(n_sub) for d in (0, 1)]
        where = [sub * n_chunks + (i if d == 0 else n_chunks - 1 - i) for sub, d in lanes]
        ngs, prs, gls = [], [], []
        for c, (_, d) in zip(where, lanes):
            rows = rows_ref[c]
            gls.append([rows[4 + 2 * d + h:5 + 2 * d + h] for h in (0, 1)])
            ngs.append(_mmul(kt_ref[c], zz_ref[d, c]))
            prs.append(pr_ref[d, c])
        streams = [(n, h) for n in range(len(lanes)) for h in (0, 1)]
        outs = []
        for n, h in streams:
            pg = jnp.concatenate([prs[n][:, 2 * h * DV:(2 * h + 1) * DV],
                                  ngs[n][:, (2 * h + 1) * DV:(2 * h + 2) * DV].astype(BF16)], axis=0)
            outs.append(_mmul(pg, s_ref[2 * n + h].astype(BF16)))
        for (n, h), out in zip(streams, outs):
            s = 2 * n + h
            r0 = pl.multiple_of(where[n] * CHUNK, CHUNK)
            o_dir = out[:CHUNK] + prs[n][:, (2 * h + 1) * DV:(2 * h + 2) * DV].astype(F32)
            if second_half:
                o = acc_ref[pl.ds(r0, CHUNK), h * DV:(h + 1) * DV] + o_dir
                zh = z_ref[pl.ds(r0, CHUNK), h * DV:(h + 1) * DV]
                y = o * lax.rsqrt(jnp.mean(o * o, axis=-1, keepdims=True) + EPS) * nw * (zh * jax.nn.sigmoid(zh))
                o_ref[pl.ds(r0, CHUNK), h * DV:(h + 1) * DV] = y.astype(o_ref.dtype)
            else:
                acc_ref[pl.ds(r0, CHUNK), h * DV:(h + 1) * DV] = o_dir
            s_ref[s] = s_ref[s] * jnp.exp(gls[n][h]) - out[CHUNK:] + ngs[n][:, 2 * h * DV:(2 * h + 1) * DV]
        return carry

    nw = nw_ref[...]
    lax.fori_loop(0, n_chunks // 2, functools.partial(step, second_half=False), 0)
    lax.fori_loop(n_chunks // 2, n_chunks, functools.partial(step, second_half=True), 0)
    for sub in range(n_sub):
        for h in range(rep):
            sf_ref[sub, h] = s_ref[2 * rep * sub + h]
            sb_ref[sub, h] = s_ref[2 * rep * sub + rep + h]


def _deltanet(qk, v, proj, cols, rows, norm_w, s0f, s0b, o_prev, *, n_seq, t_seq, n_sub, row_blk0, zero_init):
    n_chunks = t_seq // CHUNK
    t_blk = n_sub * t_seq
    c_blk = n_sub * n_chunks
    nq = QK_W // DK
    z_blk0 = QKV_W // (2 * DV)
    rep = NV_HEADS // NK_HEADS
    state_spec = pl.BlockSpec((n_sub, None, rep, DK, DV), lambda b, kh: (b, 0, kh, 0, 0))
    in_specs = [
        pl.BlockSpec((t_blk, DK), lambda b, kh: (row_blk0 + b, kh)),
        pl.BlockSpec((t_blk, DK), lambda b, kh: (row_blk0 + b, nq + kh)),
        pl.BlockSpec((t_blk, rep * DV), lambda b, kh: (row_blk0 + b, kh)),
        pl.BlockSpec((t_blk, rep * DV), lambda b, kh: (row_blk0 + b, z_blk0 + kh)),
        pl.BlockSpec((None, t_blk, 8), lambda b, kh: (kh, row_blk0 + b, 0)),
        pl.BlockSpec((None, c_blk, 8, 2 * CHUNK), lambda b, kh: (kh, row_blk0 + b, 0, 0)),
        pl.BlockSpec((1, DV), lambda b, kh: (0, 0)),
        pl.BlockSpec((n_sub, rep, DK, DV), lambda b, kh: (0 if zero_init else b, kh, 0, 0)),
        pl.BlockSpec((n_sub, rep, DK, DV), lambda b, kh: (0 if zero_init else b, kh, 0, 0)),
        pl.BlockSpec(memory_space=pl.ANY),
    ]
    out_specs = [pl.BlockSpec((t_blk, rep * DV), lambda b, kh: (row_blk0 + b, kh)), state_spec, state_spec]
    out_shape = [
        jax.ShapeDtypeStruct((qk.shape[0], V_W), BF16),
        jax.ShapeDtypeStruct((n_seq, DEPTH, NV_HEADS, DK, DV), F32),
        jax.ShapeDtypeStruct((n_seq, DEPTH, NV_HEADS, DK, DV), F32),
    ]
    args = [qk, qk, v, proj, cols, rows, norm_w.reshape(1, DV), s0f, s0b, o_prev]
    return pl.pallas_call(
        functools.partial(_delta_body, n_sub=n_sub, n_chunks=n_chunks, zero_init=zero_init),
        grid=(n_seq // n_sub, NK_HEADS),
        input_output_aliases={len(args) - 1: 0},
        name="deltanet",
        in_specs=in_specs,
        out_specs=out_specs,
        out_shape=out_shape,
        scratch_shapes=[pltpu.VMEM((t_blk, rep * DV), F32), pltpu.VMEM((2 * rep * n_sub, DK, DV), F32),
                        pltpu.VMEM((2, c_blk, CHUNK, 2 * rep * DV), BF16),
                        pltpu.VMEM((2, c_blk, CHUNK, 2 * rep * DV), BF16),
                        pltpu.VMEM((c_blk, DK, CHUNK), BF16)],
        compiler_params=pltpu.CompilerParams(dimension_semantics=("parallel", "parallel"),
                                             vmem_limit_bytes=VMEM_LIMIT_BYTES),
    )(*args)


def _decay_tables(ab, a_log, dt_bias):
    n = ab.shape[0]
    nh2 = 2 * NV_HEADS
    rep = NV_HEADS // NK_HEADS
    beta = jax.nn.sigmoid(ab[:, :nh2])
    g = -jnp.exp(a_log.reshape(nh2)) * jax.nn.softplus(ab[:, nh2:] + dt_bias.reshape(nh2))
    gch = g.reshape(n // CHUNK, CHUNK, 2, NV_HEADS)
    gcf = jnp.cumsum(gch[:, :, 0], axis=1)
    gcb = jnp.flip(jnp.cumsum(jnp.flip(gch[:, :, 1], 1), axis=1), 1)
    gc = jnp.stack([gcf, gcb], axis=2)
    gl = jnp.stack([gcf[:, -1], gcb[:, 0]], axis=1)

    def per_kh(a):
        a = a.reshape(a.shape[:-2] + (2, NK_HEADS, rep))
        return jnp.moveaxis(a, -3, -2).reshape(a.shape[:-3] + (NK_HEADS, 2 * rep))
    gc_k = per_kh(gc)
    cols = jnp.concatenate([gc_k.reshape(n, NK_HEADS, 2 * rep), per_kh(beta.reshape(n, 2, NV_HEADS))], axis=-1)
    cols = jnp.transpose(cols, (1, 0, 2))
    gr = jnp.transpose(gc_k, (2, 0, 3, 1))
    gr = jnp.concatenate([gr, gr], axis=-1)
    glr = jnp.broadcast_to(jnp.transpose(per_kh(gl), (1, 0, 2))[..., None], gr.shape)
    rows = jnp.concatenate([gr, glr], axis=2)
    return cols, rows


TOK_TILE = 512
MIX_TILE = 256
COMBINE_TILE = 128


def _group_of_tile(i, tile, n_prompt, t_sample):
    first = n_prompt // tile
    return jnp.where(i < first, 0, 1 + (i - first) // (t_sample // tile))


def _token_specs(tile, n_prompt, t_sample, d):
    first = n_prompt // tile
    per_seq = t_sample // tile
    return [
        pl.BlockSpec((tile, d), lambda i, *_: (jnp.minimum(i, first - 1), 0)),
        pl.BlockSpec((tile, d), lambda i, *_: (jnp.maximum(i - first, 0), 0)),
        pl.BlockSpec((tile, d), lambda i, *_: (jnp.maximum(i - first, 0) % per_seq, 0)),
        pl.BlockSpec((None, 1, 6 * d), lambda i, *_: (_group_of_tile(i, tile, n_prompt, t_sample), 0, 0)),
    ]


def _token_rows(i, first, xp_ref, xs_ref, pos_ref):
    return jnp.where(i < first, xp_ref[...], xs_ref[...] + pos_ref[...])


def _modulate_body(xp_ref, xs_ref, pos_ref, mod_ref, h_ref, *, first):
    d = h_ref.shape[-1]
    x = _token_rows(pl.program_id(0), first, xp_ref, xs_ref, pos_ref)
    h_ref[...] = (x * (1.0 + mod_ref[:, d:2 * d]) + mod_ref[:, 0:d]).astype(h_ref.dtype)


def _modulate(xp, xs, pos, mod, *, t_sample):
    n_p, d = xp.shape
    n = n_p + xs.shape[0]
    return pl.pallas_call(
        functools.partial(_modulate_body, first=n_p // TOK_TILE),
        grid=(n // TOK_TILE,),
        in_specs=_token_specs(TOK_TILE, n_p, t_sample, d),
        out_specs=pl.BlockSpec((TOK_TILE, d), lambda i: (i, 0)),
        out_shape=jax.ShapeDtypeStruct((n, d), BF16),
        compiler_params=pltpu.CompilerParams(dimension_semantics=("parallel",), vmem_limit_bytes=VMEM_LIMIT_BYTES),
        name="modulate",
    )(xp, xs, pos, mod)


def _gconv_body(ub_ref, uc_ref, ux_ref, w_ref, o_ref, *, t_first, t_rest):
    t_seq = jnp.where(pl.program_id(0) == 0, t_first, t_rest)
    x = uc_ref[...] * ux_ref[...]
    n = x.shape[0]
    tpos = lax.broadcasted_iota(jnp.int32, x.shape, 0) & (t_seq - 1)
    xm = jnp.where(tpos == 0, 0.0, pltpu.roll(x, 1, axis=0))
    xp = jnp.where(tpos == t_seq - 1, 0.0, pltpu.roll(x, n - 1, axis=0))
    w = w_ref[...]
    o_ref[...] = (ub_ref[...] * (xm * w[0:1] + x * w[1:2] + xp * w[2:3])).astype(o_ref.dtype)


def _gated_conv(proj, conv_w, col0, *, t_first, t_rest):
    n = proj.shape[0]
    nblk = CONV_CH // DK
    b0 = col0 // DK
    return pl.pallas_call(
        functools.partial(_gconv_body, t_first=t_first, t_rest=t_rest),
        grid=(n // PREP_ROWS, nblk),
        in_specs=[pl.BlockSpec((PREP_ROWS, DK), lambda i, j: (i, b0 + j)),
                  pl.BlockSpec((PREP_ROWS, DK), lambda i, j: (i, b0 + nblk + j)),
                  pl.BlockSpec((PREP_ROWS, DK), lambda i, j: (i, b0 + 2 * nblk + j)),
                  pl.BlockSpec((3, DK), lambda i, j: (0, j))],
        out_specs=pl.BlockSpec((PREP_ROWS, DK), lambda i, j: (i, j)),
        out_shape=jax.ShapeDtypeStruct((n, CONV_CH), BF16),
        compiler_params=pltpu.CompilerParams(dimension_semantics=("parallel", "parallel"),
                                             vmem_limit_bytes=VMEM_LIMIT_BYTES),
        name="gated_conv",
    )(proj, proj, proj, conv_w)


BR_TM = 512
BR_TN = 512


def _branch_body(o_ref, g_ref, wa_ref, wb_ref, ra_ref, rb_ref, m_ref, wa_bf, wb_bf):
    @pl.when(pl.program_id(1) == 0)
    def _():
        wa_bf[...] = wa_ref[...].astype(BF16)
        wb_bf[...] = wb_ref[...].astype(BF16)
    ya = jnp.dot(o_ref[...], wa_bf[...], preferred_element_type=F32)
    yb = jnp.dot(g_ref[...], wb_bf[...], preferred_element_type=F32)
    m_ref[...] = (jax.nn.sigmoid(ra_ref[...]) * ya + jax.nn.sigmoid(rb_ref[...]) * yb).astype(m_ref.dtype)


def _branch_merge(o, g, w_a, w_b, proj, ra_col, rb_col):
    n = o.shape[0]
    d = w_a.shape[1]
    el = pl.Element
    return pl.pallas_call(
        _branch_body,
        grid=(d // BR_TN, n // BR_TM),
        in_specs=[pl.BlockSpec((BR_TM, o.shape[1]), lambda j, i: (i, 0)),
                  pl.BlockSpec((BR_TM, g.shape[1]), lambda j, i: (i, 0)),
                  pl.BlockSpec((w_a.shape[0], BR_TN), lambda j, i: (0, j)),
                  pl.BlockSpec((w_b.shape[0], BR_TN), lambda j, i: (0, j)),
                  pl.BlockSpec((el(BR_TM), el(BR_TN)),
                               lambda j, i: (i * BR_TM, pl.multiple_of(ra_col + j * BR_TN, DK))),
                  pl.BlockSpec((el(BR_TM), el(BR_TN)),
                               lambda j, i: (i * BR_TM, pl.multiple_of(rb_col + j * BR_TN, DK)))],
        out_specs=pl.BlockSpec((BR_TM, BR_TN), lambda j, i: (i, j)),
        out_shape=jax.ShapeDtypeStruct((n, d), BF16),
        scratch_shapes=[pltpu.VMEM((w_a.shape[0], BR_TN), BF16), pltpu.VMEM((w_b.shape[0], BR_TN), BF16)],
        compiler_params=pltpu.CompilerParams(dimension_semantics=("arbitrary", "arbitrary"),
                                             vmem_limit_bytes=VMEM_LIMIT_BYTES),
        name="branch_merge",
    )(o, g, w_a, w_b, proj, proj)


def _ln_rows(v, g, b):
    mu = jnp.mean(v, axis=-1, keepdims=True)
    c = v - mu
    return c * lax.rsqrt(jnp.mean(c * c, axis=-1, keepdims=True) + EPS) * g + b


def _mix_body(m_ref, wo_ref, xp_ref, xs_ref, pos_ref, mod_ref, g_ref, b_ref, rw_ref, rb_ref,
              x1_ref, h2_ref, lg_ref, *, first):
    d = x1_ref.shape[-1]
    x = _token_rows(pl.program_id(0), first, xp_ref, xs_ref, pos_ref)
    mixed = jnp.dot(m_ref[...], wo_ref[...], preferred_element_type=F32)
    x1 = _ln_rows(DEEPNORM_ALPHA * x + mod_ref[:, 2 * d:3 * d] * mixed, g_ref[...], b_ref[...])
    h2 = x1 * (1.0 + mod_ref[:, 4 * d:5 * d]) + mod_ref[:, 3 * d:4 * d]
    x1_ref[...] = x1
    h2_ref[...] = _pack_bf16_pairs(h2)
    lg_ref[...] = jnp.dot(h2, rw_ref[...], precision=lax.Precision.HIGHEST, preferred_element_type=F32) + rb_ref[...]


def _mix_norm_route(m, w_o_bf, xp, xs, pos, mod, ln_g, ln_b, router_w, router_b, *, t_sample):
    n, d = m.shape
    n_p = xp.shape[0]
    ne = router_w.shape[1]
    const = lambda i: (0, 0)
    return pl.pallas_call(
        functools.partial(_mix_body, first=n_p // MIX_TILE),
        grid=(n // MIX_TILE,),
        in_specs=[pl.BlockSpec((MIX_TILE, d), lambda i: (i, 0)), pl.BlockSpec((d, d), const)]
        + _token_specs(MIX_TILE, n_p, t_sample, d)
        + [pl.BlockSpec((1, d), const), pl.BlockSpec((1, d), const),
           pl.BlockSpec((d, ne), const), pl.BlockSpec((1, ne), const)],
        out_specs=[pl.BlockSpec((MIX_TILE, d), lambda i: (i, 0)), pl.BlockSpec((MIX_TILE, d // 2), lambda i: (i, 0)),
                   pl.BlockSpec((MIX_TILE, ne), lambda i: (i, 0))],
        out_shape=[jax.ShapeDtypeStruct((n, d), F32), jax.ShapeDtypeStruct((n, d // 2), jnp.uint32),
                   jax.ShapeDtypeStruct((n, ne), F32)],
        compiler_params=pltpu.CompilerParams(dimension_semantics=("parallel",), vmem_limit_bytes=VMEM_LIMIT_BYTES),
        name="mix_norm_route",
    )(m, w_o_bf, xp, xs, pos, mod, ln_g.reshape(1, d), ln_b.reshape(1, d), router_w, router_b.reshape(1, ne))


def _combine_body(slot_ref, y_hbm, p_ref, x1_ref, mod_ref, g_ref, b_ref, op_ref, os_ref, buf, sem, *,
                  n_tiles, first):
    tt = COMBINE_TILE
    d = x1_ref.shape[-1]
    i = pl.program_id(0)

    def row_copy(src_row, dst_row, which):
        return pltpu.make_async_copy(y_hbm.at[pl.ds(src_row, 1), :], buf.at[which, pl.ds(dst_row, 1), :],
                                     sem.at[which])

    def issue(tile, which):
        base = tile * (tt * TOP_K)

        def body(t, carry):
            for k in range(TOP_K):
                row_copy(slot_ref[base + t * TOP_K + k], k * tt + t, which).start(priority=k % 2)
            return carry
        lax.fori_loop(0, tt, body, 0)

    @pl.when(i == 0)
    def _():
        issue(0, 0)

    @pl.when(i + 1 < n_tiles)
    def _():
        issue(i + 1, (i + 1) % 2)

    cur = i % 2
    pltpu.make_async_copy(y_hbm.at[pl.ds(0, TOP_K * tt), :], buf.at[cur], sem.at[cur]).wait()
    p = p_ref[...]
    ff = p[:, 0:1] * buf[cur, 0:tt, :]
    for k in range(1, TOP_K):
        ff = ff + p[:, k:k + 1] * buf[cur, k * tt:(k + 1) * tt, :]
    x2 = _ln_rows(DEEPNORM_ALPHA * x1_ref[...] + mod_ref[:, 5 * d:6 * d] * ff, g_ref[...], b_ref[...])

    @pl.when(i < first)
    def _():
        op_ref[...] = x2

    @pl.when(i >= first)
    def _():
        os_ref[...] = x2


def _combine_norm(slot, y, top_p, x1, mod, ln_g, ln_b, *, n_prompt, t_sample):
    n, d = x1.shape
    tt = COMBINE_TILE
    n_tiles = n // tt
    first = n_prompt // tt
    grid_spec = pltpu.PrefetchScalarGridSpec(
        num_scalar_prefetch=1,
        grid=(n_tiles,),
        in_specs=[pl.BlockSpec(memory_space=pl.ANY),
                  pl.BlockSpec((tt, TOP_K), lambda i, s: (i, 0)),
                  pl.BlockSpec((tt, d), lambda i, s: (i, 0)),
                  pl.BlockSpec((None, 1, 6 * d), lambda i, s: (_group_of_tile(i, tt, n_prompt, t_sample), 0, 0)),
                  pl.BlockSpec((1, d), lambda i, s: (0, 0)),
                  pl.BlockSpec((1, d), lambda i, s: (0, 0))],
        out_specs=[pl.BlockSpec((tt, d), lambda i, s: (jnp.minimum(i, first - 1), 0)),
                   pl.BlockSpec((tt, d), lambda i, s: (jnp.maximum(i - first, 0), 0))],
        scratch_shapes=[pltpu.VMEM((2, TOP_K * tt, d), F32), pltpu.SemaphoreType.DMA((2,))],
    )
    return pl.pallas_call(
        functools.partial(_combine_body, n_tiles=n_tiles, first=first),
        grid_spec=grid_spec,
        out_shape=[jax.ShapeDtypeStruct((n_prompt, d), F32), jax.ShapeDtypeStruct((n - n_prompt, d), F32)],
        compiler_params=pltpu.CompilerParams(dimension_semantics=("arbitrary",), vmem_limit_bytes=VMEM_LIMIT_BYTES),
        name="combine_norm",
    )(slot, y, top_p, x1, mod, ln_g.reshape(1, d), ln_b.reshape(1, d))


def _grid_pos_embed(rows, dim):
    quarter = dim // 4
    freqs = 1.0 / (POS_BASE ** (jnp.arange(quarter, dtype=F32) / quarter))

    def axis_embed(n):
        ang = jnp.arange(n, dtype=F32)[:, None] * freqs[None, :]
        return jnp.concatenate([jnp.sin(ang), jnp.cos(ang)], axis=-1)
    er = axis_embed(rows)
    ec = axis_embed(GRID_W)
    pos = jnp.concatenate([jnp.broadcast_to(er[:, None, :], (rows, GRID_W, dim // 2)),
                           jnp.broadcast_to(ec[None, :, :], (rows, GRID_W, dim // 2))], axis=-1)
    return pos.reshape(rows * GRID_W, dim)


def kernel(x_prompt, x_sample, c, state_fwd, state_bwd, c_ctx, w_mod, b_mod, w_in, conv_qkv, a_log, dt_bias,
           norm_o, w_a_out, conv_b, w_b_out, w_o, ln1_g, ln1_b, router_w, router_b, w_gu, b_gu, w_down, b_down,
           ln2_g, ln2_b):
    nb_p, t_p, d = x_prompt.shape
    nb_s, t_s, _ = x_sample.shape
    n_p = nb_p * t_p
    n_s = nb_s * t_s
    assert n_p == PREP_ROWS and t_s == PREP_ROWS
    xp = x_prompt.reshape(n_p, d)
    xs = x_sample.reshape(n_s, d)
    pos = _grid_pos_embed(t_s // GRID_W, d)
    offs = tuple(int(o) for o in np.cumsum(IN_SIZES)[:-1])

    l = 0
    cvec = jnp.concatenate([c_ctx[None, :], c], axis=0)
    cpad = jnp.zeros((8, d), F32).at[:1 + nb_s].set(jax.nn.silu(cvec))
    mod = (_mm(cpad, w_mod[l], tm=8, tn=512)[:1 + nb_s] + b_mod[l])[:, None, :]

    h = _modulate(xp, xs, pos, mod, t_sample=t_s)
    proj = _mm(h, w_in[l], tm=2048, tn=512)

    qk = _conv_silu(proj, conv_qkv[l], 0, 2 * QK_W, t_first=t_p, t_rest=t_s, unit_norm=True)
    v = _conv_silu(proj, conv_qkv[l], 2 * QK_W, V_W, t_first=t_p, t_rest=t_s, unit_norm=False)
    cols, rws = _decay_tables(proj[:, offs[1]:offs[3]], a_log[l], dt_bias[l])
    zero_state = jnp.zeros((PROMPT_SUB, NV_HEADS, DK, DV), F32)
    o_all = jnp.zeros((n_p + n_s, V_W), BF16)
    o_all, sf, sb = _deltanet(qk, v, proj, cols, rws, norm_o[l], zero_state, zero_state, o_all,
                              n_seq=nb_p, t_seq=t_p, n_sub=PROMPT_SUB, row_blk0=0, zero_init=True)
    o_all, _, _ = _deltanet(qk, v, proj, cols, rws, norm_o[l], state_fwd[:, l], state_bwd[:, l], o_all,
                            n_seq=nb_s, t_seq=t_s, n_sub=1, row_blk0=n_p // t_s, zero_init=False)

    gconv = _gated_conv(proj, conv_b[l], offs[3], t_first=t_p, t_rest=t_s)
    m = _branch_merge(o_all, gconv, w_a_out[l], w_b_out[l], proj, offs[6], offs[7])
    x1, h2, logits = _mix_norm_route(m, w_o[l].astype(BF16), xp, xs, pos, mod, ln1_g[l], ln1_b[l],
                                     router_w[l], router_b[l], t_sample=t_s)
    slot, slot_tok, top_p, block_e, n_valid = _route(logits)
    y = _moe_blocks(h2, slot_tok, block_e, n_valid, w_gu[l], b_gu[l], w_down[l], b_down[l])
    x2_p, x2_s = _combine_norm(slot, y, top_p, x1, mod, ln2_g[l], ln2_b[l], n_prompt=n_p, t_sample=t_s)
    return (x2_p.reshape(nb_p, t_p, d), x2_s.reshape(nb_s, t_s, d), sf, sb)
```

```python
import functools

import jax
import jax.numpy as jnp
import numpy as np
from jax import lax
from jax.experimental import pallas as pl
from jax.experimental.pallas import tpu as pltpu

F32 = jnp.float32
BF16 = jnp.bfloat16

D_MODEL = 2048
GRID_W = 64
NK_HEADS = 16
NV_HEADS = 32
DK = 128
DV = 128
QK_W = NK_HEADS * DK
V_W = NV_HEADS * DV
QKV_W = 2 * QK_W + V_W
CHUNK = 64
CONV_CH = D_MODEL
IN_SIZES = (QKV_W, V_W, 2 * NV_HEADS, 2 * NV_HEADS, CONV_CH, CONV_CH, CONV_CH, D_MODEL, D_MODEL)
N_EXPERTS = 32
TOP_K = 4
D_FF = D_MODEL
SWIGLU_LIMIT = 7.0
SWIGLU_ALPHA = 1.702
DEPTH = 1
DEEPNORM_ALPHA = (2 * DEPTH) ** 0.25
EPS = 1e-6
POS_BASE = 10000.0

VMEM_LIMIT_BYTES = 56 * 1024 * 1024


def _mm_body(x_ref, w_ref, o_ref):
    o_ref[...] = jnp.dot(x_ref[...].astype(BF16), w_ref[...].astype(BF16), preferred_element_type=F32)


def _mm(x, w, *, tm, tn):
    m, k = x.shape
    n = w.shape[1]
    assert m % tm == 0
    return pl.pallas_call(
        _mm_body,
        name="matmul",
        grid=(m // tm, pl.cdiv(n, tn)),
        in_specs=[pl.BlockSpec((tm, k), lambda i, j: (i, 0)),
                  pl.BlockSpec((k, tn), lambda i, j: (0, j))],
        out_specs=pl.BlockSpec((tm, tn), lambda i, j: (i, j)),
        out_shape=jax.ShapeDtypeStruct((m, n), F32),
        compiler_params=pltpu.CompilerParams(dimension_semantics=("parallel", "parallel"),
                                             vmem_limit_bytes=VMEM_LIMIT_BYTES),
    )(x, w)


MOE_TM = 512
MOE_TF = 512


def _pack_bf16_pairs(x):
    n = x.shape[1] // 2
    lo = lax.bitcast_convert_type(x[:, :n].astype(BF16).astype(F32), jnp.uint32)
    hi = lax.bitcast_convert_type(x[:, n:].astype(BF16).astype(F32), jnp.uint32)
    return (lo >> 16) | hi


def _unpack_bf16_pairs(w):
    lo = lax.bitcast_convert_type(w << 16, F32)
    hi = lax.bitcast_convert_type(w & jnp.uint32(0xFFFF0000), F32)
    return jnp.concatenate([lo, hi], axis=1).astype(BF16)


def _moe_body(be_ref, nv_ref, tok_ref, h_hbm, wg_ref, wu_ref, bg_ref, bu_ref, wd_ref, bd_ref, o_ref,
              xbuf, sem):
    i = pl.program_id(0)
    f = pl.program_id(1)
    nf = pl.num_programs(1)
    n_valid = nv_ref[0]
    valid = i < n_valid
    part = MOE_TM // (D_FF // MOE_TF)

    def row_copy(slot_row, buf_row, which):
        return pltpu.make_async_copy(h_hbm.at[pl.ds(tok_ref[slot_row], 1), :], xbuf.at[which, pl.ds(buf_row, 1), :],
                                     sem.at[which])

    def block_wait(which):
        pltpu.make_async_copy(h_hbm.at[pl.ds(0, MOE_TM), :], xbuf.at[which], sem.at[which]).wait()

    @pl.when(jnp.logical_and(jnp.logical_and(i == 0, f == 0), valid))
    def _():
        def body(r, carry):
            row_copy(r, r, 0).start()
            return carry
        lax.fori_loop(0, MOE_TM, body, 0)

    @pl.when(f == 0)
    def _():
        o_ref[...] = jnp.where(valid, jnp.broadcast_to(bd_ref[...], o_ref.shape), 0.0)

    @pl.when(jnp.logical_and(f == 0, valid))
    def _():
        block_wait(i % 2)

    @pl.when(valid)
    def _():
        nxt = jnp.minimum(i + 1, n_valid - 1)
        for r in range(part):
            row_copy(nxt * MOE_TM + f * part + r, f * part + r, (i + 1) % 2).start()
        x = _unpack_bf16_pairs(xbuf[i % 2])
        g = jnp.dot(x, wg_ref[...].astype(BF16), preferred_element_type=F32) + bg_ref[...]
        u = jnp.dot(x, wu_ref[...].astype(BF16), preferred_element_type=F32) + bu_ref[...]
        gate = jnp.minimum(g, SWIGLU_LIMIT)
        up = jnp.clip(u, -SWIGLU_LIMIT, SWIGLU_LIMIT)
        act = (up + 1.0) * gate * jax.nn.sigmoid(SWIGLU_ALPHA * gate)
        o_ref[...] += jnp.dot(act.astype(BF16), wd_ref[...].astype(BF16), preferred_element_type=F32)

    @pl.when(jnp.logical_and(f == nf - 1, i == n_valid - 1))
    def _():
        block_wait((i + 1) % 2)


def _moe_blocks(h_packed, slot_tok, block_e, n_valid, w_gu, b_gu, w_down, b_down):
    n_slots = slot_tok.shape[0]
    d = w_down.shape[-1]
    n_blocks = n_slots // MOE_TM
    nf = D_FF // MOE_TF

    def f_eff(i, f, nv):
        return jnp.where(i < nv[0], f, nf - 1)

    grid_spec = pltpu.PrefetchScalarGridSpec(
        num_scalar_prefetch=3,
        grid=(n_blocks, nf),
        in_specs=[
            pl.BlockSpec(memory_space=pl.ANY),
            pl.BlockSpec((None, d, MOE_TF), lambda i, f, be, nv, tok: (be[i], 0, f_eff(i, f, nv))),
            pl.BlockSpec((None, d, MOE_TF), lambda i, f, be, nv, tok: (be[i], 0, nf + f_eff(i, f, nv))),
            pl.BlockSpec((None, 1, MOE_TF), lambda i, f, be, nv, tok: (be[i], 0, f_eff(i, f, nv))),
            pl.BlockSpec((None, 1, MOE_TF), lambda i, f, be, nv, tok: (be[i], 0, nf + f_eff(i, f, nv))),
            pl.BlockSpec((None, MOE_TF, d), lambda i, f, be, nv, tok: (be[i], f_eff(i, f, nv), 0)),
            pl.BlockSpec((None, 1, d), lambda i, f, be, nv, tok: (be[i], 0, 0)),
        ],
        out_specs=pl.BlockSpec((MOE_TM, d), lambda i, f, be, nv, tok: (i, 0)),
        scratch_shapes=[pltpu.VMEM((2, MOE_TM, h_packed.shape[1]), h_packed.dtype),
                        pltpu.SemaphoreType.DMA((2,))],
    )
    return pl.pallas_call(
        _moe_body,
        name="moe_experts",
        grid_spec=grid_spec,
        out_shape=jax.ShapeDtypeStruct((n_slots, d), F32),
        compiler_params=pltpu.CompilerParams(dimension_semantics=("arbitrary", "arbitrary"),
                                             vmem_limit_bytes=VMEM_LIMIT_BYTES),
    )(block_e, n_valid, slot_tok, h_packed, w_gu, w_gu, b_gu[:, None, :], b_gu[:, None, :], w_down,
      b_down[:, None, :])


def _route(logits):
    n_tok = logits.shape[0]
    n_assign = n_tok * TOP_K
    top_logit, top_e = lax.top_k(logits, TOP_K)
    top_p = jax.nn.softmax(top_logit, axis=-1)
    flat_e = top_e.reshape(-1)
    onehot = (flat_e[:, None] == jnp.arange(N_EXPERTS)[None, :]).astype(jnp.int32)
    csum = jnp.cumsum(onehot, axis=0)
    counts = csum[-1]
    rank = jnp.take_along_axis(csum, flat_e[:, None], axis=1)[:, 0] - 1
    padded = (counts + MOE_TM - 1) // MOE_TM * MOE_TM
    pad_end = jnp.cumsum(padded)
    pad_start = pad_end - padded
    slot = (pad_start[flat_e] + rank).astype(jnp.int32)
    n_blocks = -(-(n_assign + N_EXPERTS * (MOE_TM - 1)) // MOE_TM)
    n_slots = n_blocks * MOE_TM
    n_valid = (pad_end[-1] // MOE_TM).astype(jnp.int32)
    blk = jnp.arange(n_blocks, dtype=jnp.int32)
    block_e = jnp.searchsorted(pad_end, jnp.minimum(blk, n_valid - 1) * MOE_TM, side='right').astype(jnp.int32)
    block_e = jnp.minimum(block_e, N_EXPERTS - 1)
    order = jnp.argsort(flat_e, stable=True).astype(jnp.int32)
    grp_start = jnp.cumsum(counts) - counts
    s = jnp.arange(n_slots, dtype=jnp.int32)
    e_slot = jnp.repeat(block_e, MOE_TM)
    r = s - pad_start[e_slot]
    src = jnp.minimum(grp_start[e_slot] + r, n_assign - 1)
    slot_tok = jnp.where(r < counts[e_slot], order[src] // TOP_K, 0).astype(jnp.int32)
    return slot, slot_tok, top_p, block_e, n_valid.reshape(1)


PREP_ROWS = 4096


def _prep_body(x_ref, w_ref, o_ref, *, t_first, t_rest, unit_norm):
    i = pl.program_id(0)
    j = pl.program_id(1)
    t_seq = jnp.where(i == 0, t_first, t_rest)
    x = x_ref[...]
    n = x.shape[0]
    row = lax.broadcasted_iota(jnp.int32, x.shape, 0)
    tpos = row & (t_seq - 1)
    xm = jnp.where(tpos == 0, 0.0, pltpu.roll(x, 1, axis=0))
    xp = jnp.where(tpos == t_seq - 1, 0.0, pltpu.roll(x, n - 1, axis=0))
    w = w_ref[...]
    y = xm * w[0:1] + x * w[1:2] + xp * w[2:3]
    y = y * jax.nn.sigmoid(y)
    if unit_norm:
        rs = lax.rsqrt(jnp.sum(y * y, axis=-1, keepdims=True) + EPS)
        y = y * (rs * jnp.where(j < QK_W // DK, DK ** -0.5, 1.0))
    o_ref[...] = y.astype(o_ref.dtype)


def _conv_silu(proj, conv_w, col0, width, *, t_first, t_rest, unit_norm):
    n = proj.shape[0]
    b0 = col0 // DK
    return pl.pallas_call(
        functools.partial(_prep_body, t_first=t_first, t_rest=t_rest, unit_norm=unit_norm),
        name="qk_prep" if unit_norm else "v_prep",
        grid=(n // PREP_ROWS, width // DK),
        in_specs=[pl.BlockSpec((PREP_ROWS, DK), lambda i, j: (i, b0 + j)),
                  pl.BlockSpec((3, DK), lambda i, j: (0, b0 + j))],
        out_specs=pl.BlockSpec((PREP_ROWS, DK), lambda i, j: (i, j)),
        out_shape=jax.ShapeDtypeStruct((n, width), BF16),
        compiler_params=pltpu.CompilerParams(dimension_semantics=("parallel", "parallel"),
                                             vmem_limit_bytes=VMEM_LIMIT_BYTES),
    )(proj, conv_w)


PROMPT_SUB = 4
SOLVE_WAYS = 8


def _mmul(a, b):
    return jnp.dot(a, b, preferred_element_type=F32)


def _delta_body(q_ref, k_ref, v_ref, z_ref, cols_ref, rows_ref, nw_ref, s0f_ref, s0b_ref, o_prev_ref,
                o_ref, sf_ref, sb_ref, acc_ref, s_ref, zz_ref, pr_ref, kt_ref, *, n_sub, n_chunks, zero_init):
    del o_prev_ref
    n_all = n_sub * n_chunks
    ways = min(SOLVE_WAYS, n_all)
    lane = lax.broadcasted_iota(jnp.int32, (CHUNK, 2 * CHUNK), 1)
    row = lax.broadcasted_iota(jnp.int32, (CHUNK, 2 * CHUNK), 0)
    pos = lane & (CHUNK - 1)
    left = lane < CHUNK
    left1 = lax.broadcasted_iota(jnp.int32, (1, 2 * CHUNK), 1) < CHUNK
    eye2 = (pos == row).astype(F32)
    incl = (pos <= row, pos >= row)
    strict = (pos < row, pos > row)
    same8 = (row >> 3) == (pos >> 3)
    same16 = (row >> 4) == (pos >> 4)
    same32 = (row >> 5) == (pos >> 5)
    nt = (((1,), (1,)), ((), ()))
    tn = (((0,), (0,)), ((), ()))

    def split(m):
        return jnp.where(left, m, 0.0), jnp.where(left, 0.0, m)

    def bdiag(m):
        return jnp.concatenate(split(m), axis=0).astype(BF16)

    def pmm(a, b):
        return _mmul(a.astype(BF16), bdiag(b))

    def inv_unit_tri(lms):
        d8 = [jnp.where(same8, lm, 0.0) for lm in lms]
        xs = [eye2 - d for d in d8]
        d2 = [pmm(d, d) for d in d8]
        xs = [x + pmm(x, d) for x, d in zip(xs, d2)]
        d4 = [pmm(d, d) for d in d2]
        xs = [x + pmm(x, d) for x, d in zip(xs, d4)]
        for keep in (same16 & ~same8, same32 & ~same16, ~same32):
            ys = [pmm(jnp.where(keep, lm, 0.0), x) for lm, x in zip(lms, xs)]
            xs = [x - pmm(x, y) for x, y in zip(xs, ys)]
        return xs

    def chunk_terms(c, d):
        r0 = pl.multiple_of(c * CHUNK, CHUNK)
        cols = cols_ref[pl.ds(r0, CHUNK), :]
        rows = rows_ref[c]

        def lanes_of(col):
            return jnp.broadcast_to(cols[:, col:col + 1], (CHUNK, 2 * CHUNK))
        gcol = [lanes_of(2 * d + h) for h in (0, 1)]
        bcol = [lanes_of(4 + 2 * d + h) for h in (0, 1)]
        gl = [rows[4 + 2 * d + h:5 + 2 * d + h] for h in (0, 1)]
        return r0, rows, gcol, bcol, gl

    assert n_chunks % 2 == 0
    rep = s0f_ref.shape[1]
    if zero_init:
        s_ref[...] = jnp.zeros_like(s_ref)
    else:
        for sub in range(n_sub):
            for h in range(rep):
                s_ref[2 * rep * sub + h] = s0f_ref[sub, h]
                s_ref[2 * rep * sub + rep + h] = s0b_ref[sub, h]

    def solve(i, carry):
        chunks = [i + part * (n_all // ways) for part in range(ways)]
        qcs, kcs, vcs, kkms, qkms = [], [], [], [], []
        for c in chunks:
            r0 = pl.multiple_of(c * CHUNK, CHUNK)
            kb = k_ref[pl.ds(r0, CHUNK), :]
            qb = q_ref[pl.ds(r0, CHUNK), :]
            kk2 = jnp.concatenate([kb, kb], axis=0)
            kcs.append(kb.astype(F32))
            qcs.append(qb.astype(F32))
            vcs.append(v_ref[pl.ds(r0, CHUNK), :])
            kkms.append(lax.dot_general(kb, kk2, nt, preferred_element_type=F32))
            qkms.append(lax.dot_general(qb, kk2, nt, preferred_element_type=F32))
            kt_ref[c] = jnp.transpose(kcs[-1]).astype(BF16)
        chains = [(j, d) for j in range(ways) for d in (0, 1)]
        lms, rhs2s, a2s, decays = [], [], [], []
        for j, d in chains:
            c = chunks[j]
            _, rows, gcol, bcol, gl = chunk_terms(c, d)
            gcol2 = jnp.where(left, gcol[0], gcol[1])
            bcol2 = jnp.where(left, bcol[0], bcol[1])
            grow2 = jnp.where(left1, rows[2 * d:2 * d + 1], rows[2 * d + 1:2 * d + 2])
            dm = jnp.exp(jnp.where(incl[d], gcol2 - grow2, -jnp.inf))
            a2s.append(qkms[j] * dm)
            eg = [jnp.exp(g) for g in gcol]
            decays.append((eg, [jnp.exp(gl[h] - gcol[h]) for h in (0, 1)]))
            lms.append(jnp.where(strict[d], kkms[j] * dm * bcol2, 0.0))
            rhs = [jnp.concatenate([vcs[j][:, h * DV:(h + 1) * DV] * bcol[h], kcs[j] * (bcol[h] * eg[h])], axis=1)
                   for h in (0, 1)]
            rhs2s.append(jnp.concatenate(rhs, axis=0).astype(BF16))
        t2s = inv_unit_tri(lms)
        sols = [[_mmul(th.astype(BF16), rhs2) for th in split(t2)]
                for t2, rhs2 in zip(t2s, rhs2s)]
        aws = [[_mmul(ah.astype(BF16), jnp.concatenate(sol, axis=0).astype(BF16)) for ah in split(a2)]
               for a2, sol in zip(a2s, sols)]
        for (j, d), sol, aw, (eg, ek) in zip(chains, sols, aws, decays):
            c = chunks[j]
            zz, pr = [], []
            for h in (0, 1):
                zz += [sol[h][:, :DV] * ek[h], sol[h][:, DV:] * ek[h]]
                pr += [qcs[j] * eg[h] - aw[h][:, DV:], aw[h][:, :DV]]
            zz_ref[d, c] = jnp.concatenate(zz, axis=1).astype(BF16)
            pr_ref[d, c] = jnp.concatenate(pr, axis=1).astype(BF16)
        return carry

    lax.fori_loop(0, n_all // ways, solve, 0)

    def step(i, carry, *, second_half):
        lanes = [(sub, d) for sub in range(n_sub) for d in (0, 1)]
        where = [sub * n_chunks + (i if d == 0 else n_chunks - 1 - i) for sub, d in lanes]
        ngs, prs, gls = [], [], []
        for c, (_, d) in zip(where, lanes):
            rows = rows_ref[c]
            gls.append([rows[4 + 2 * d + h:5 + 2 * d + h] for h in (0, 1)])
            ngs.append(_mmul(kt_ref[c], zz_ref[d, c]))
            prs.append(pr_ref[d, c])
        streams = [(n, h) for n in range(len(lanes)) for h in (0, 1)]
        outs = []
        for n, h in streams:
            pg = jnp.concatenate([prs[n][:, 2 * h * DV:(2 * h + 1) * DV],
                                  ngs[n][:, (2 * h + 1) * DV:(2 * h + 2) * DV].astype(BF16)], axis=0)
            outs.append(_mmul(pg, s_ref[2 * n + h].astype(BF16)))
        for (n, h), out in zip(streams, outs):
            s = 2 * n + h
            r0 = pl.multiple_of(where[n] * CHUNK, CHUNK)
            o_dir = out[:CHUNK] + prs[n][:, (2 * h + 1) * DV:(2 * h + 2) * DV].astype(F32)
            if second_half:
                o = acc_ref[pl.ds(r0, CHUNK), h * DV:(h + 1) * DV] + o_dir
                zh = z_ref[pl.ds(r0, CHUNK), h * DV:(h + 1) * DV]
                y = o * lax.rsqrt(jnp.mean(o * o, axis=-1, keepdims=True) + EPS) * nw * (zh * jax.nn.sigmoid(zh))
                o_ref[pl.ds(r0, CHUNK), h * DV:(h + 1) * DV] = y.astype(o_ref.dtype)
            else:
                acc_ref[pl.ds(r0, CHUNK), h * DV:(h + 1) * DV] = o_dir
            s_ref[s] = s_ref[s] * jnp.exp(gls[n][h]) - out[CHUNK:] + ngs[n][:, 2 * h * DV:(2 * h + 1) * DV]
        return carry

    nw = nw_ref[...]
    lax.fori_loop(0, n_chunks // 2, functools.partial(step, second_half=False), 0)
    lax.fori_loop(n_chunks // 2, n_chunks, functools.partial(step, second_half=True), 0)
    for sub in range(n_sub):
        for h in range(rep):
            sf_ref[sub, h] = s_ref[2 * rep * sub + h]
            sb_ref[sub, h] = s_ref[2 * rep * sub + rep + h]


def _deltanet(qk, v, proj, cols, rows, norm_w, s0f, s0b, o_prev, *, n_seq, t_seq, n_sub, row_blk0, zero_init):
    n_chunks = t_seq // CHUNK
    t_blk = n_sub * t_seq
    c_blk = n_sub * n_chunks
    nq = QK_W // DK
    z_blk0 = QKV_W // (2 * DV)
    rep = NV_HEADS // NK_HEADS
    state_spec = pl.BlockSpec((n_sub, None, rep, DK, DV), lambda b, kh: (b, 0, kh, 0, 0))
    in_specs = [
        pl.BlockSpec((t_blk, DK), lambda b, kh: (row_blk0 + b, kh)),
        pl.BlockSpec((t_blk, DK), lambda b, kh: (row_blk0 + b, nq + kh)),
        pl.BlockSpec((t_blk, rep * DV), lambda b, kh: (row_blk0 + b, kh)),
        pl.BlockSpec((t_blk, rep * DV), lambda b, kh: (row_blk0 + b, z_blk0 + kh)),
        pl.BlockSpec((None, t_blk, 8), lambda b, kh: (kh, row_blk0 + b, 0)),
        pl.BlockSpec((None, c_blk, 8, 2 * CHUNK), lambda b, kh: (kh, row_blk0 + b, 0, 0)),
        pl.BlockSpec((1, DV), lambda b, kh: (0, 0)),
        pl.BlockSpec((n_sub, rep, DK, DV), lambda b, kh: (0 if zero_init else b, kh, 0, 0)),
        pl.BlockSpec((n_sub, rep, DK, DV), lambda b, kh: (0 if zero_init else b, kh, 0, 0)),
        pl.BlockSpec(memory_space=pl.ANY),
    ]
    out_specs = [pl.BlockSpec((t_blk, rep * DV), lambda b, kh: (row_blk0 + b, kh)), state_spec, state_spec]
    out_shape = [
        jax.ShapeDtypeStruct((qk.shape[0], V_W), BF16),
        jax.ShapeDtypeStruct((n_seq, DEPTH, NV_HEADS, DK, DV), F32),
        jax.ShapeDtypeStruct((n_seq, DEPTH, NV_HEADS, DK, DV), F32),
    ]
    args = [qk, qk, v, proj, cols, rows, norm_w.reshape(1, DV), s0f, s0b, o_prev]
    return pl.pallas_call(
        functools.partial(_delta_body, n_sub=n_sub, n_chunks=n_chunks, zero_init=zero_init),
        grid=(n_seq // n_sub, NK_HEADS),
        input_output_aliases={len(args) - 1: 0},
        name="deltanet",
        in_specs=in_specs,
        out_specs=out_specs,
        out_shape=out_shape,
        scratch_shapes=[pltpu.VMEM((t_blk, rep * DV), F32), pltpu.VMEM((2 * rep * n_sub, DK, DV), F32),
                        pltpu.VMEM((2, c_blk, CHUNK, 2 * rep * DV), BF16),
                        pltpu.VMEM((2, c_blk, CHUNK, 2 * rep * DV), BF16),
                        pltpu.VMEM((c_blk, DK, CHUNK), BF16)],
        compiler_params=pltpu.CompilerParams(dimension_semantics=("parallel", "parallel"),
                                             vmem_limit_bytes=VMEM_LIMIT_BYTES),
    )(*args)


def _decay_tables(ab, a_log, dt_bias):
    n = ab.shape[0]
    nh2 = 2 * NV_HEADS
    rep = NV_HEADS // NK_HEADS
    beta = jax.nn.sigmoid(ab[:, :nh2])
    g = -jnp.exp(a_log.reshape(nh2)) * jax.nn.softplus(ab[:, nh2:] + dt_bias.reshape(nh2))
    gch = g.reshape(n // CHUNK, CHUNK, 2, NV_HEADS)
    pre = jnp.cumsum(gch, axis=1)
    gl = pre[:, -1]
    gc = jnp.concatenate([pre[:, :, :1], gl[:, None, 1:] - pre[:, :, 1:] + gch[:, :, 1:]], axis=2)

    def per_kh(a):
        a = a.reshape(a.shape[:-2] + (2, NK_HEADS, rep))
        return jnp.moveaxis(a, -3, -2).reshape(a.shape[:-3] + (NK_HEADS, 2 * rep))
    gc_k = per_kh(gc)
    cols = jnp.concatenate([gc_k.reshape(n, NK_HEADS, 2 * rep), per_kh(beta.reshape(n, 2, NV_HEADS))], axis=-1)
    cols = jnp.transpose(cols, (1, 0, 2))
    gr = jnp.transpose(gc_k, (2, 0, 3, 1))
    gr = jnp.concatenate([gr, gr], axis=-1)
    glr = jnp.broadcast_to(jnp.transpose(per_kh(gl), (1, 0, 2))[..., None], gr.shape)
    rows = jnp.concatenate([gr, glr], axis=2)
    return cols, rows


TOK_TILE = 512
MIX_TILE = 256
COMBINE_TILE = 128


def _group_of_tile(i, tile, n_prompt, t_sample):
    first = n_prompt // tile
    return jnp.where(i < first, 0, 1 + (i - first) // (t_sample // tile))


def _token_specs(tile, n_prompt, t_sample, d):
    first = n_prompt // tile
    per_seq = t_sample // tile
    return [
        pl.BlockSpec((tile, d), lambda i, *_: (jnp.minimum(i, first - 1), 0)),
        pl.BlockSpec((tile, d), lambda i, *_: (jnp.maximum(i - first, 0), 0)),
        pl.BlockSpec((tile, d), lambda i, *_: (jnp.maximum(i - first, 0) % per_seq, 0)),
        pl.BlockSpec((None, 1, 6 * d), lambda i, *_: (_group_of_tile(i, tile, n_prompt, t_sample), 0, 0)),
    ]


def _token_rows(i, first, xp_ref, xs_ref, pos_ref):
    return jnp.where(i < first, xp_ref[...], xs_ref[...] + pos_ref[...])


def _modulate_body(xp_ref, xs_ref, pos_ref, mod_ref, h_ref, *, first):
    d = h_ref.shape[-1]
    x = _token_rows(pl.program_id(0), first, xp_ref, xs_ref, pos_ref)
    h_ref[...] = (x * (1.0 + mod_ref[:, d:2 * d]) + mod_ref[:, 0:d]).astype(h_ref.dtype)


def _modulate(xp, xs, pos, mod, *, t_sample):
    n_p, d = xp.shape
    n = n_p + xs.shape[0]
    return pl.pallas_call(
        functools.partial(_modulate_body, first=n_p // TOK_TILE),
        grid=(n // TOK_TILE,),
        in_specs=_token_specs(TOK_TILE, n_p, t_sample, d),
        out_specs=pl.BlockSpec((TOK_TILE, d), lambda i: (i, 0)),
        out_shape=jax.ShapeDtypeStruct((n, d), BF16),
        compiler_params=pltpu.CompilerParams(dimension_semantics=("parallel",), vmem_limit_bytes=VMEM_LIMIT_BYTES),
        name="modulate",
    )(xp, xs, pos, mod)


def _gconv_body(ub_ref, uc_ref, ux_ref, w_ref, o_ref, *, t_first, t_rest):
    t_seq = jnp.where(pl.program_id(0) == 0, t_first, t_rest)
    x = uc_ref[...] * ux_ref[...]
    n = x.shape[0]
    tpos = lax.broadcasted_iota(jnp.int32, x.shape, 0) & (t_seq - 1)
    xm = jnp.where(tpos == 0, 0.0, pltpu.roll(x, 1, axis=0))
    xp = jnp.where(tpos == t_seq - 1, 0.0, pltpu.roll(x, n - 1, axis=0))
    w = w_ref[...]
    o_ref[...] = (ub_ref[...] * (xm * w[0:1] + x * w[1:2] + xp * w[2:3])).astype(o_ref.dtype)


def _gated_conv(proj, conv_w, col0, *, t_first, t_rest):
    n = proj.shape[0]
    nblk = CONV_CH // DK
    b0 = col0 // DK
    return pl.pallas_call(
        functools.partial(_gconv_body, t_first=t_first, t_rest=t_rest),
        grid=(n // PREP_ROWS, nblk),
        in_specs=[pl.BlockSpec((PREP_ROWS, DK), lambda i, j: (i, b0 + j)),
                  pl.BlockSpec((PREP_ROWS, DK), lambda i, j: (i, b0 + nblk + j)),
                  pl.BlockSpec((PREP_ROWS, DK), lambda i, j: (i, b0 + 2 * nblk + j)),
                  pl.BlockSpec((3, DK), lambda i, j: (0, j))],
        out_specs=pl.BlockSpec((PREP_ROWS, DK), lambda i, j: (i, j)),
        out_shape=jax.ShapeDtypeStruct((n, CONV_CH), BF16),
        compiler_params=pltpu.CompilerParams(dimension_semantics=("parallel", "parallel"),
                                             vmem_limit_bytes=VMEM_LIMIT_BYTES),
        name="gated_conv",
    )(proj, proj, proj, conv_w)


BR_TM = 512
BR_TN = 512


def _branch_body(o_ref, g_ref, wa_ref, wb_ref, ra_ref, rb_ref, m_ref, wa_bf, wb_bf):
    @pl.when(pl.program_id(1) == 0)
    def _():
        wa_bf[...] = wa_ref[...].astype(BF16)
        wb_bf[...] = wb_ref[...].astype(BF16)
    ya = jnp.dot(o_ref[...], wa_bf[...], preferred_element_type=F32)
    yb = jnp.dot(g_ref[...], wb_bf[...], preferred_element_type=F32)
    m_ref[...] = (jax.nn.sigmoid(ra_ref[...]) * ya + jax.nn.sigmoid(rb_ref[...]) * yb).astype(m_ref.dtype)


def _branch_merge(o, g, w_a, w_b, proj, ra_col, rb_col):
    n = o.shape[0]
    d = w_a.shape[1]
    el = pl.Element
    return pl.pallas_call(
        _branch_body,
        grid=(d // BR_TN, n // BR_TM),
        in_specs=[pl.BlockSpec((BR_TM, o.shape[1]), lambda j, i: (i, 0)),
                  pl.BlockSpec((BR_TM, g.shape[1]), lambda j, i: (i, 0)),
                  pl.BlockSpec((w_a.shape[0], BR_TN), lambda j, i: (0, j)),
                  pl.BlockSpec((w_b.shape[0], BR_TN), lambda j, i: (0, j)),
                  pl.BlockSpec((el(BR_TM), el(BR_TN)),
                               lambda j, i: (i * BR_TM, pl.multiple_of(ra_col + j * BR_TN, DK))),
                  pl.BlockSpec((el(BR_TM), el(BR_TN)),
                               lambda j, i: (i * BR_TM, pl.multiple_of(rb_col + j * BR_TN, DK)))],
        out_specs=pl.BlockSpec((BR_TM, BR_TN), lambda j, i: (i, j)),
        out_shape=jax.ShapeDtypeStruct((n, d), BF16),
        scratch_shapes=[pltpu.VMEM((w_a.shape[0], BR_TN), BF16), pltpu.VMEM((w_b.shape[0], BR_TN), BF16)],
        compiler_params=pltpu.CompilerParams(dimension_semantics=("arbitrary", "arbitrary"),
                                             vmem_limit_bytes=VMEM_LIMIT_BYTES),
        name="branch_merge",
    )(o, g, w_a, w_b, proj, proj)


def _ln_rows(v, g, b):
    mu = jnp.mean(v, axis=-1, keepdims=True)
    c = v - mu
    return c * lax.rsqrt(jnp.mean(c * c, axis=-1, keepdims=True) + EPS) * g + b


def _mix_body(m_ref, wo_ref, xp_ref, xs_ref, pos_ref, mod_ref, g_ref, b_ref, rw_ref, rb_ref,
              x1_ref, h2_ref, lg_ref, *, first):
    d = x1_ref.shape[-1]
    x = _token_rows(pl.program_id(0), first, xp_ref, xs_ref, pos_ref)
    mixed = jnp.dot(m_ref[...], wo_ref[...], preferred_element_type=F32)
    x1 = _ln_rows(DEEPNORM_ALPHA * x + mod_ref[:, 2 * d:3 * d] * mixed, g_ref[...], b_ref[...])
    h2 = x1 * (1.0 + mod_ref[:, 4 * d:5 * d]) + mod_ref[:, 3 * d:4 * d]
    x1_ref[...] = x1
    h2_ref[...] = _pack_bf16_pairs(h2)
    lg_ref[...] = jnp.dot(h2, rw_ref[...], precision=lax.Precision.HIGHEST, preferred_element_type=F32) + rb_ref[...]


def _mix_norm_route(m, w_o_bf, xp, xs, pos, mod, ln_g, ln_b, router_w, router_b, *, t_sample):
    n, d = m.shape
    n_p = xp.shape[0]
    ne = router_w.shape[1]
    const = lambda i: (0, 0)
    return pl.pallas_call(
        functools.partial(_mix_body, first=n_p // MIX_TILE),
        grid=(n // MIX_TILE,),
        in_specs=[pl.BlockSpec((MIX_TILE, d), lambda i: (i, 0)), pl.BlockSpec((d, d), const)]
        + _token_specs(MIX_TILE, n_p, t_sample, d)
        + [pl.BlockSpec((1, d), const), pl.BlockSpec((1, d), const),
           pl.BlockSpec((d, ne), const), pl.BlockSpec((1, ne), const)],
        out_specs=[pl.BlockSpec((MIX_TILE, d), lambda i: (i, 0)), pl.BlockSpec((MIX_TILE, d // 2), lambda i: (i, 0)),
                   pl.BlockSpec((MIX_TILE, ne), lambda i: (i, 0))],
        out_shape=[jax.ShapeDtypeStruct((n, d), F32), jax.ShapeDtypeStruct((n, d // 2), jnp.uint32),
                   jax.ShapeDtypeStruct((n, ne), F32)],
        compiler_params=pltpu.CompilerParams(dimension_semantics=("parallel",), vmem_limit_bytes=VMEM_LIMIT_BYTES),
        name="mix_norm_route",
    )(m, w_o_bf, xp, xs, pos, mod, ln_g.reshape(1, d), ln_b.reshape(1, d), router_w, router_b.reshape(1, ne))


def _combine_body(slot_ref, y_hbm, p_ref, x1_ref, mod_ref, g_ref, b_ref, op_ref, os_ref, buf, sem, *,
                  n_tiles, first):
    tt = COMBINE_TILE
    d = x1_ref.shape[-1]
    i = pl.program_id(0)

    def row_copy(src_row, dst_row, which):
        return pltpu.make_async_copy(y_hbm.at[pl.ds(src_row, 1), :], buf.at[which, pl.ds(dst_row, 1), :],
                                     sem.at[which])

    def issue(tile, which):
        base = tile * (tt * TOP_K)

        def body(t, carry):
            for k in range(TOP_K):
                row_copy(slot_ref[base + t * TOP_K + k], k * tt + t, which).start(priority=k % 2)
            return carry
        lax.fori_loop(0, tt, body, 0)

    @pl.when(i == 0)
    def _():
        issue(0, 0)

    @pl.when(i + 1 < n_tiles)
    def _():
        issue(i + 1, (i + 1) % 2)

    cur = i % 2
    pltpu.make_async_copy(y_hbm.at[pl.ds(0, TOP_K * tt), :], buf.at[cur], sem.at[cur]).wait()
    p = p_ref[...]
    ff = p[:, 0:1] * buf[cur, 0:tt, :]
    for k in range(1, TOP_K):
        ff = ff + p[:, k:k + 1] * buf[cur, k * tt:(k + 1) * tt, :]
    x2 = _ln_rows(DEEPNORM_ALPHA * x1_ref[...] + mod_ref[:, 5 * d:6 * d] * ff, g_ref[...], b_ref[...])

    @pl.when(i < first)
    def _():
        op_ref[...] = x2

    @pl.when(i >= first)
    def _():
        os_ref[...] = x2


def _combine_norm(slot, y, top_p, x1, mod, ln_g, ln_b, *, n_prompt, t_sample):
    n, d = x1.shape
    tt = COMBINE_TILE
    n_tiles = n // tt
    first = n_prompt // tt
    grid_spec = pltpu.PrefetchScalarGridSpec(
        num_scalar_prefetch=1,
        grid=(n_tiles,),
        in_specs=[pl.BlockSpec(memory_space=pl.ANY),
                  pl.BlockSpec((tt, TOP_K), lambda i, s: (i, 0)),
                  pl.BlockSpec((tt, d), lambda i, s: (i, 0)),
                  pl.BlockSpec((None, 1, 6 * d), lambda i, s: (_group_of_tile(i, tt, n_prompt, t_sample), 0, 0)),
                  pl.BlockSpec((1, d), lambda i, s: (0, 0)),
                  pl.BlockSpec((1, d), lambda i, s: (0, 0))],
        out_specs=[pl.BlockSpec((tt, d), lambda i, s: (jnp.minimum(i, first - 1), 0)),
                   pl.BlockSpec((tt, d), lambda i, s: (jnp.maximum(i - first, 0), 0))],
        scratch_shapes=[pltpu.VMEM((2, TOP_K * tt, d), F32), pltpu.SemaphoreType.DMA((2,))],
    )
    return pl.pallas_call(
        functools.partial(_combine_body, n_tiles=n_tiles, first=first),
        grid_spec=grid_spec,
        out_shape=[jax.ShapeDtypeStruct((n_prompt, d), F32), jax.ShapeDtypeStruct((n - n_prompt, d), F32)],
        compiler_params=pltpu.CompilerParams(dimension_semantics=("arbitrary",), vmem_limit_bytes=VMEM_LIMIT_BYTES),
        name="combine_norm",
    )(slot, y, top_p, x1, mod, ln_g.reshape(1, d), ln_b.reshape(1, d))


def _grid_pos_embed(rows, dim):
    quarter = dim // 4
    freqs = 1.0 / (POS_BASE ** (jnp.arange(quarter, dtype=F32) / quarter))

    def axis_embed(n):
        ang = jnp.arange(n, dtype=F32)[:, None] * freqs[None, :]
        return jnp.concatenate([jnp.sin(ang), jnp.cos(ang)], axis=-1)
    er = axis_embed(rows)
    ec = axis_embed(GRID_W)
    pos = jnp.concatenate([jnp.broadcast_to(er[:, None, :], (rows, GRID_W, dim // 2)),
                           jnp.broadcast_to(ec[None, :, :], (rows, GRID_W, dim // 2))], axis=-1)
    return pos.reshape(rows * GRID_W, dim)


def kernel(x_prompt, x_sample, c, state_fwd, state_bwd, c_ctx, w_mod, b_mod, w_in, conv_qkv, a_log, dt_bias,
           norm_o, w_a_out, conv_b, w_b_out, w_o, ln1_g, ln1_b, router_w, router_b, w_gu, b_gu, w_down, b_down,
           ln2_g, ln2_b):
    nb_p, t_p, d = x_prompt.shape
    nb_s, t_s, _ = x_sample.shape
    n_p = nb_p * t_p
    n_s = nb_s * t_s
    assert n_p == PREP_ROWS and t_s == PREP_ROWS
    xp = x_prompt.reshape(n_p, d)
    xs = x_sample.reshape(n_s, d)
    pos = _grid_pos_embed(t_s // GRID_W, d)
    offs = tuple(int(o) for o in np.cumsum(IN_SIZES)[:-1])

    l = 0
    cvec = jnp.concatenate([c_ctx[None, :], c], axis=0)
    cpad = jnp.zeros((8, d), F32).at[:1 + nb_s].set(jax.nn.silu(cvec))
    mod = (_mm(cpad, w_mod[l], tm=8, tn=512)[:1 + nb_s] + b_mod[l])[:, None, :]

    h = _modulate(xp, xs, pos, mod, t_sample=t_s)
    proj = _mm(h, w_in[l], tm=2048, tn=512)

    qk = _conv_silu(proj, conv_qkv[l], 0, 2 * QK_W, t_first=t_p, t_rest=t_s, unit_norm=True)
    v = _conv_silu(proj, conv_qkv[l], 2 * QK_W, V_W, t_first=t_p, t_rest=t_s, unit_norm=False)
    cols, rws = _decay_tables(proj[:, offs[1]:offs[3]], a_log[l], dt_bias[l])
    zero_state = jnp.zeros((PROMPT_SUB, NV_HEADS, DK, DV), F32)
    o_all = jnp.zeros((n_p + n_s, V_W), BF16)
    o_all, sf, sb = _deltanet(qk, v, proj, cols, rws, norm_o[l], zero_state, zero_state, o_all,
                              n_seq=nb_p, t_seq=t_p, n_sub=PROMPT_SUB, row_blk0=0, zero_init=True)
    o_all, _, _ = _deltanet(qk, v, proj, cols, rws, norm_o[l], state_fwd[:, l], state_bwd[:, l], o_all,
                            n_seq=nb_s, t_seq=t_s, n_sub=1, row_blk0=n_p // t_s, zero_init=False)

    gconv = _gated_conv(proj, conv_b[l], offs[3], t_first=t_p, t_rest=t_s)
    m = _branch_merge(o_all, gconv, w_a_out[l], w_b_out[l], proj, offs[6], offs[7])
    x1, h2, logits = _mix_norm_route(m, w_o[l].astype(BF16), xp, xs, pos, mod, ln1_g[l], ln1_b[l],
                                     router_w[l], router_b[l], t_sample=t_s)
    slot, slot_tok, top_p, block_e, n_valid = _route(logits)
    y = _moe_blocks(h2, slot_tok, block_e, n_valid, w_gu[l], b_gu[l], w_down[l], b_down[l])
    x2_p, x2_s = _combine_norm(slot, y, top_p, x1, mod, ln2_g[l], ln2_b[l], n_prompt=n_p, t_sample=t_s)
    return (x2_p.reshape(nb_p, t_p, d), x2_s.reshape(nb_s, t_s, d), sf, sb)
```

```python
import functools

import jax
import jax.numpy as jnp
import numpy as np
from jax import lax
from jax.experimental import pallas as pl
from jax.experimental.pallas import tpu as pltpu

F32 = jnp.float32
BF16 = jnp.bfloat16

D_MODEL = 2048
GRID_W = 64
NK_HEADS = 16
NV_HEADS = 32
DK = 128
DV = 128
QK_W = NK_HEADS * DK
V_W = NV_HEADS * DV
QKV_W = 2 * QK_W + V_W
CHUNK = 64
CONV_CH = D_MODEL
IN_SIZES = (QKV_W, V_W, 2 * NV_HEADS, 2 * NV_HEADS, CONV_CH, CONV_CH, CONV_CH, D_MODEL, D_MODEL)
N_EXPERTS = 32
TOP_K = 4
D_FF = D_MODEL
SWIGLU_LIMIT = 7.0
SWIGLU_ALPHA = 1.702
DEPTH = 1
DEEPNORM_ALPHA = (2 * DEPTH) ** 0.25
EPS = 1e-6
POS_BASE = 10000.0

VMEM_LIMIT_BYTES = 56 * 1024 * 1024


def _mm_body(x_ref, w_ref, o_ref):
    o_ref[...] = jnp.dot(x_ref[...].astype(BF16), w_ref[...].astype(BF16), preferred_element_type=F32)


def _mm(x, w, *, tm, tn):
    m, k = x.shape
    n = w.shape[1]
    assert m % tm == 0
    return pl.pallas_call(
        _mm_body,
        name="matmul",
        grid=(m // tm, pl.cdiv(n, tn)),
        in_specs=[pl.BlockSpec((tm, k), lambda i, j: (i, 0)),
                  pl.BlockSpec((k, tn), lambda i, j: (0, j))],
        out_specs=pl.BlockSpec((tm, tn), lambda i, j: (i, j)),
        out_shape=jax.ShapeDtypeStruct((m, n), F32),
        compiler_params=pltpu.CompilerParams(dimension_semantics=("parallel", "parallel"),
                                             vmem_limit_bytes=VMEM_LIMIT_BYTES),
    )(x, w)


MOE_TM = 1024
MOE_SUB = 512
MOE_TF = 256


def _pack_bf16_pairs(x):
    n = x.shape[1] // 2
    lo = lax.bitcast_convert_type(x[:, :n].astype(BF16).astype(F32), jnp.uint32)
    hi = lax.bitcast_convert_type(x[:, n:].astype(BF16).astype(F32), jnp.uint32)
    return (lo >> 16) | hi


def _unpack_bf16_pairs(w):
    lo = lax.bitcast_convert_type(w << 16, F32)
    hi = lax.bitcast_convert_type(w & jnp.uint32(0xFFFF0000), F32)
    return jnp.concatenate([lo, hi], axis=1).astype(BF16)


def _moe_body(be_ref, nv_ref, br_ref, tok_ref, h_hbm, wg_ref, wu_ref, bg_ref, bu_ref, wd_ref, bd_ref, o_ref,
              xbuf, sem):
    i = pl.program_id(0)
    f = pl.program_id(1)
    nf = pl.num_programs(1)
    n_valid = nv_ref[0]
    valid = i < n_valid
    part = MOE_TM // (D_FF // MOE_TF)

    def row_copy(slot_row, buf_row, which):
        return pltpu.make_async_copy(h_hbm.at[pl.ds(tok_ref[slot_row], 1), :], xbuf.at[which, pl.ds(buf_row, 1), :],
                                     sem.at[which])

    def block_wait(which):
        pltpu.make_async_copy(h_hbm.at[pl.ds(0, MOE_TM), :], xbuf.at[which], sem.at[which]).wait()

    @pl.when(jnp.logical_and(jnp.logical_and(i == 0, f == 0), valid))
    def _():
        def body(r, carry):
            row_copy(r, r, 0).start()
            return carry
        lax.fori_loop(0, MOE_TM, body, 0)

    @pl.when(f == 0)
    def _():
        o_ref[...] = jnp.where(valid, jnp.broadcast_to(bd_ref[...], o_ref.shape), 0.0)

    @pl.when(jnp.logical_and(f == 0, valid))
    def _():
        block_wait(i % 2)

    @pl.when(valid)
    def _():
        nxt = jnp.minimum(i + 1, n_valid - 1)
        for r in range(part):
            row_copy(nxt * MOE_TM + f * part + r, f * part + r, (i + 1) % 2).start()
        wg = wg_ref[...].astype(BF16)
        wu = wu_ref[...].astype(BF16)
        wd = wd_ref[...].astype(BF16)

        def sub_block(r0):
            x = _unpack_bf16_pairs(xbuf[i % 2, r0:r0 + MOE_SUB, :])
            g = jnp.dot(x, wg, preferred_element_type=F32) + bg_ref[...]
            u = jnp.dot(x, wu, preferred_element_type=F32) + bu_ref[...]
            gate = jnp.minimum(g, SWIGLU_LIMIT)
            up = jnp.clip(u, -SWIGLU_LIMIT, SWIGLU_LIMIT)
            act = (up + 1.0) * gate * jax.nn.sigmoid(SWIGLU_ALPHA * gate)
            o_ref[r0:r0 + MOE_SUB, :] += jnp.dot(act.astype(BF16), wd, preferred_element_type=F32)

        sub_block(0)
        for r0 in range(MOE_SUB, MOE_TM, MOE_SUB):
            pl.when(br_ref[i] > r0)(functools.partial(sub_block, r0))

    @pl.when(jnp.logical_and(f == nf - 1, i == n_valid - 1))
    def _():
        block_wait((i + 1) % 2)


def _moe_blocks(h_packed, slot_tok, block_e, block_rows, n_valid, w_gu, b_gu, w_down, b_down):
    n_slots = slot_tok.shape[0]
    d = w_down.shape[-1]
    n_blocks = n_slots // MOE_TM
    nf = D_FF // MOE_TF

    def f_eff(i, f, nv):
        return jnp.where(i < nv[0], f, nf - 1)

    grid_spec = pltpu.PrefetchScalarGridSpec(
        num_scalar_prefetch=4,
        grid=(n_blocks, nf),
        in_specs=[
            pl.BlockSpec(memory_space=pl.ANY),
            pl.BlockSpec((None, d, MOE_TF), lambda i, f, be, nv, *_: (be[i], 0, f_eff(i, f, nv))),
            pl.BlockSpec((None, d, MOE_TF), lambda i, f, be, nv, *_: (be[i], 0, nf + f_eff(i, f, nv))),
            pl.BlockSpec((None, 1, MOE_TF), lambda i, f, be, nv, *_: (be[i], 0, f_eff(i, f, nv))),
            pl.BlockSpec((None, 1, MOE_TF), lambda i, f, be, nv, *_: (be[i], 0, nf + f_eff(i, f, nv))),
            pl.BlockSpec((None, MOE_TF, d), lambda i, f, be, nv, *_: (be[i], f_eff(i, f, nv), 0)),
            pl.BlockSpec((None, 1, d), lambda i, f, be, nv, *_: (be[i], 0, 0)),
        ],
        out_specs=pl.BlockSpec((MOE_TM, d), lambda i, f, be, nv, *_: (i, 0)),
        scratch_shapes=[pltpu.VMEM((2, MOE_TM, h_packed.shape[1]), h_packed.dtype),
                        pltpu.SemaphoreType.DMA((2,))],
    )
    return pl.pallas_call(
        _moe_body,
        name="moe_experts",
        grid_spec=grid_spec,
        out_shape=jax.ShapeDtypeStruct((n_slots, d), F32),
        compiler_params=pltpu.CompilerParams(dimension_semantics=("arbitrary", "arbitrary"),
                                             vmem_limit_bytes=VMEM_LIMIT_BYTES),
    )(block_e, n_valid, block_rows, slot_tok, h_packed, w_gu, w_gu, b_gu[:, None, :], b_gu[:, None, :], w_down,
      b_down[:, None, :])


def _route(logits):
    n_tok = logits.shape[0]
    n_assign = n_tok * TOP_K
    top_logit, top_e = lax.top_k(logits, TOP_K)
    top_p = jax.nn.softmax(top_logit, axis=-1)
    flat_e = top_e.reshape(-1)
    onehot = (flat_e[:, None] == jnp.arange(N_EXPERTS)[None, :]).astype(jnp.int32)
    csum = jnp.cumsum(onehot, axis=0)
    counts = csum[-1]
    rank = jnp.take_along_axis(csum, flat_e[:, None], axis=1)[:, 0] - 1
    padded = (counts + MOE_TM - 1) // MOE_TM * MOE_TM
    pad_end = jnp.cumsum(padded)
    pad_start = pad_end - padded
    slot = (pad_start[flat_e] + rank).astype(jnp.int32)
    n_blocks = -(-(n_assign + N_EXPERTS * (MOE_TM - 1)) // MOE_TM)
    n_slots = n_blocks * MOE_TM
    n_valid = (pad_end[-1] // MOE_TM).astype(jnp.int32)
    blk = jnp.arange(n_blocks, dtype=jnp.int32)
    block_e = jnp.searchsorted(pad_end, jnp.minimum(blk, n_valid - 1) * MOE_TM, side='right').astype(jnp.int32)
    block_e = jnp.minimum(block_e, N_EXPERTS - 1)
    block_rows = jnp.clip(counts[block_e] - (blk * MOE_TM - pad_start[block_e]), 0, MOE_TM)
    block_rows = jnp.where(blk < n_valid, block_rows, 0).astype(jnp.int32)
    order = jnp.argsort(flat_e, stable=True).astype(jnp.int32)
    grp_start = jnp.cumsum(counts) - counts
    s = jnp.arange(n_slots, dtype=jnp.int32)
    e_slot = jnp.repeat(block_e, MOE_TM)
    r = s - pad_start[e_slot]
    src = jnp.minimum(grp_start[e_slot] + r, n_assign - 1)
    slot_tok = jnp.where(r < counts[e_slot], order[src] // TOP_K, 0).astype(jnp.int32)
    return slot, slot_tok, top_p, block_e, block_rows, n_valid.reshape(1)


PREP_ROWS = 4096


def _prep_body(x_ref, w_ref, o_ref, *, t_first, t_rest, unit_norm):
    i = pl.program_id(0)
    j = pl.program_id(1)
    t_seq = jnp.where(i == 0, t_first, t_rest)
    x = x_ref[...]
    n = x.shape[0]
    row = lax.broadcasted_iota(jnp.int32, x.shape, 0)
    tpos = row & (t_seq - 1)
    xm = jnp.where(tpos == 0, 0.0, pltpu.roll(x, 1, axis=0))
    xp = jnp.where(tpos == t_seq - 1, 0.0, pltpu.roll(x, n - 1, axis=0))
    w = w_ref[...]
    y = xm * w[0:1] + x * w[1:2] + xp * w[2:3]
    y = y * jax.nn.sigmoid(y)
    if unit_norm:
        rs = lax.rsqrt(jnp.sum(y * y, axis=-1, keepdims=True) + EPS)
        y = y * (rs * jnp.where(j < QK_W // DK, DK ** -0.5, 1.0))
    o_ref[...] = y.astype(o_ref.dtype)


def _conv_silu(proj, conv_w, col0, width, *, t_first, t_rest, unit_norm):
    n = proj.shape[0]
    b0 = col0 // DK
    return pl.pallas_call(
        functools.partial(_prep_body, t_first=t_first, t_rest=t_rest, unit_norm=unit_norm),
        name="qk_prep" if unit_norm else "v_prep",
        grid=(n // PREP_ROWS, width // DK),
        in_specs=[pl.BlockSpec((PREP_ROWS, DK), lambda i, j: (i, b0 + j)),
                  pl.BlockSpec((3, DK), lambda i, j: (0, b0 + j))],
        out_specs=pl.BlockSpec((PREP_ROWS, DK), lambda i, j: (i, j)),
        out_shape=jax.ShapeDtypeStruct((n, width), BF16),
        compiler_params=pltpu.CompilerParams(dimension_semantics=("parallel", "parallel"),
                                             vmem_limit_bytes=VMEM_LIMIT_BYTES),
    )(proj, conv_w)


PROMPT_SUB = 4
SOLVE_WAYS = 8


def _mmul(a, b):
    return jnp.dot(a, b, preferred_element_type=F32)


def _delta_body(q_ref, k_ref, v_ref, z_ref, cols_ref, rows_ref, nw_ref, s0f_ref, s0b_ref, o_prev_ref,
                o_ref, sf_ref, sb_ref, acc_ref, s_ref, zz_ref, pr_ref, kt_ref, *, n_sub, n_chunks, zero_init):
    del o_prev_ref
    n_all = n_sub * n_chunks
    ways = min(SOLVE_WAYS, n_all)
    lane = lax.broadcasted_iota(jnp.int32, (CHUNK, 2 * CHUNK), 1)
    row = lax.broadcasted_iota(jnp.int32, (CHUNK, 2 * CHUNK), 0)
    pos = lane & (CHUNK - 1)
    left = lane < CHUNK
    left1 = lax.broadcasted_iota(jnp.int32, (1, 2 * CHUNK), 1) < CHUNK
    eye2 = (pos == row).astype(F32)
    incl = (pos <= row, pos >= row)
    strict = (pos < row, pos > row)
    same8 = (row >> 3) == (pos >> 3)
    same16 = (row >> 4) == (pos >> 4)
    same32 = (row >> 5) == (pos >> 5)
    nt = (((1,), (1,)), ((), ()))
    tn = (((0,), (0,)), ((), ()))

    def split(m):
        return jnp.where(left, m, 0.0), jnp.where(left, 0.0, m)

    def bdiag(m):
        return jnp.concatenate(split(m), axis=0).astype(BF16)

    def pmm(a, b):
        return _mmul(a.astype(BF16), bdiag(b))

    def inv_unit_tri(lms):
        d8 = [jnp.where(same8, lm, 0.0) for lm in lms]
        xs = [eye2 - d for d in d8]
        d2 = [pmm(d, d) for d in d8]
        xs = [x + pmm(x, d) for x, d in zip(xs, d2)]
        d4 = [pmm(d, d) for d in d2]
        xs = [x + pmm(x, d) for x, d in zip(xs, d4)]
        for keep in (same16 & ~same8, same32 & ~same16, ~same32):
            ys = [pmm(jnp.where(keep, lm, 0.0), x) for lm, x in zip(lms, xs)]
            xs = [x - pmm(x, y) for x, y in zip(xs, ys)]
        return xs

    def chunk_terms(c, d):
        r0 = pl.multiple_of(c * CHUNK, CHUNK)
        cols = cols_ref[pl.ds(r0, CHUNK), :]
        rows = rows_ref[c]

        def lanes_of(col):
            return jnp.broadcast_to(cols[:, col:col + 1], (CHUNK, 2 * CHUNK))
        gcol = [lanes_of(2 * d + h) for h in (0, 1)]
        bcol = [lanes_of(4 + 2 * d + h) for h in (0, 1)]
        gl = [rows[4 + 2 * d + h:5 + 2 * d + h] for h in (0, 1)]
        return r0, rows, gcol, bcol, gl

    assert n_chunks % 2 == 0
    rep = s0f_ref.shape[1]
    if zero_init:
        s_ref[...] = jnp.zeros_like(s_ref)
    else:
        for sub in range(n_sub):
            for h in range(rep):
                s_ref[2 * rep * sub + h] = s0f_ref[sub, h]
                s_ref[2 * rep * sub + rep + h] = s0b_ref[sub, h]

    def solve(i, carry):
        chunks = [i + part * (n_all // ways) for part in range(ways)]
        qcs, kcs, vcs, kkms, qkms = [], [], [], [], []
        for c in chunks:
            r0 = pl.multiple_of(c * CHUNK, CHUNK)
            kb = k_ref[pl.ds(r0, CHUNK), :]
            qb = q_ref[pl.ds(r0, CHUNK), :]
            kk2 = jnp.concatenate([kb, kb], axis=0)
            kcs.append(kb.astype(F32))
            qcs.append(qb.astype(F32))
            vcs.append(v_ref[pl.ds(r0, CHUNK), :])
            kkms.append(lax.dot_general(kb, kk2, nt, preferred_element_type=F32))
            qkms.append(lax.dot_general(qb, kk2, nt, preferred_element_type=F32))
            kt_ref[c] = jnp.transpose(kcs[-1]).astype(BF16)
        chains = [(j, d) for j in range(ways) for d in (0, 1)]
        lms, rhs2s, a2s, decays = [], [], [], []
        for j, d in chains:
            c = chunks[j]
            _, rows, gcol, bcol, gl = chunk_terms(c, d)
            gcol2 = jnp.where(left, gcol[0], gcol[1])
            bcol2 = jnp.where(left, bcol[0], bcol[1])
            grow2 = jnp.where(left1, rows[2 * d:2 * d + 1], rows[2 * d + 1:2 * d + 2])
            dm = jnp.exp(jnp.where(incl[d], gcol2 - grow2, -jnp.inf))
            a2s.append(qkms[j] * dm)
            eg = [jnp.exp(g) for g in gcol]
            decays.append((eg, [jnp.exp(gl[h] - gcol[h]) for h in (0, 1)]))
            lms.append(jnp.where(strict[d], kkms[j] * dm * bcol2, 0.0))
            rhs = [jnp.concatenate([vcs[j][:, h * DV:(h + 1) * DV] * bcol[h], kcs[j] * (bcol[h] * eg[h])], axis=1)
                   for h in (0, 1)]
            rhs2s.append(jnp.concatenate(rhs, axis=0).astype(BF16))
        t2s = inv_unit_tri(lms)
        sols = [[_mmul(th.astype(BF16), rhs2) for th in split(t2)]
                for t2, rhs2 in zip(t2s, rhs2s)]
        aws = [[_mmul(ah.astype(BF16), jnp.concatenate(sol, axis=0).astype(BF16)) for ah in split(a2)]
               for a2, sol in zip(a2s, sols)]
        for (j, d), sol, aw, (eg, ek) in zip(chains, sols, aws, decays):
            c = chunks[j]
            zz, pr = [], []
            for h in (0, 1):
                zz += [sol[h][:, :DV] * ek[h], sol[h][:, DV:] * ek[h]]
                pr += [qcs[j] * eg[h] - aw[h][:, DV:], aw[h][:, :DV]]
            zz_ref[d, c] = jnp.concatenate(zz, axis=1).astype(BF16)
            pr_ref[d, c] = jnp.concatenate(pr, axis=1).astype(BF16)
        return carry

    lax.fori_loop(0, n_all // ways, solve, 0)

    def step(i, carry, *, second_half):
        lanes = [(sub, d) for sub in range(n_sub) for d in (0, 1)]
        where = [sub * n_chunks + (i if d == 0 else n_chunks - 1 - i) for sub, d in lanes]
        ngs, prs, gls = [], [], []
        for c, (_, d) in zip(where, lanes):
            rows = rows_ref[c]
            gls.append([rows[4 + 2 * d + h:5 + 2 * d + h] for h in (0, 1)])
            ngs.append(_mmul(kt_ref[c], zz_ref[d, c]))
            prs.append(pr_ref[d, c])
        streams = [(n, h) for n in range(len(lanes)) for h in (0, 1)]
        outs = []
        for n, h in streams:
            pg = jnp.concatenate([prs[n][:, 2 * h * DV:(2 * h + 1) * DV],
                                  ngs[n][:, (2 * h + 1) * DV:(2 * h + 2) * DV].astype(BF16)], axis=0)
            outs.append(_mmul(pg, s_ref[2 * n + h].astype(BF16)))
        for (n, h), out in zip(streams, outs):
            s = 2 * n + h
            r0 = pl.multiple_of(where[n] * CHUNK, CHUNK)
            o_dir = out[:CHUNK] + prs[n][:, (2 * h + 1) * DV:(2 * h + 2) * DV].astype(F32)
            if second_half:
                o = acc_ref[pl.ds(r0, CHUNK), h * DV:(h + 1) * DV] + o_dir
                zh = z_ref[pl.ds(r0, CHUNK), h * DV:(h + 1) * DV]
                y = o * lax.rsqrt(jnp.mean(o * o, axis=-1, keepdims=True) + EPS) * nw * (zh * jax.nn.sigmoid(zh))
                o_ref[pl.ds(r0, CHUNK), h * DV:(h + 1) * DV] = y.astype(o_ref.dtype)
            else:
                acc_ref[pl.ds(r0, CHUNK), h * DV:(h + 1) * DV] = o_dir
            s_ref[s] = s_ref[s] * jnp.exp(gls[n][h]) - out[CHUNK:] + ngs[n][:, 2 * h * DV:(2 * h + 1) * DV]
        return carry

    nw = nw_ref[...]
    lax.fori_loop(0, n_chunks // 2, functools.partial(step, second_half=False), 0)
    lax.fori_loop(n_chunks // 2, n_chunks, functools.partial(step, second_half=True), 0)
    for sub in range(n_sub):
        for h in range(rep):
            sf_ref[sub, h] = s_ref[2 * rep * sub + h]
            sb_ref[sub, h] = s_ref[2 * rep * sub + rep + h]


def _deltanet(qk, v, proj, cols, rows, norm_w, s0f, s0b, o_prev, *, n_seq, t_seq, n_sub, row_blk0, zero_init):
    n_chunks = t_seq // CHUNK
    t_blk = n_sub * t_seq
    c_blk = n_sub * n_chunks
    nq = QK_W // DK
    z_blk0 = QKV_W // (2 * DV)
    rep = NV_HEADS // NK_HEADS
    state_spec = pl.BlockSpec((n_sub, None, rep, DK, DV), lambda b, kh: (b, 0, kh, 0, 0))
    in_specs = [
        pl.BlockSpec((t_blk, DK), lambda b, kh: (row_blk0 + b, kh)),
        pl.BlockSpec((t_blk, DK), lambda b, kh: (row_blk0 + b, nq + kh)),
        pl.BlockSpec((t_blk, rep * DV), lambda b, kh: (row_blk0 + b, kh)),
        pl.BlockSpec((t_blk, rep * DV), lambda b, kh: (row_blk0 + b, z_blk0 + kh)),
        pl.BlockSpec((None, t_blk, 8), lambda b, kh: (kh, row_blk0 + b, 0)),
        pl.BlockSpec((None, c_blk, 8, 2 * CHUNK), lambda b, kh: (kh, row_blk0 + b, 0, 0)),
        pl.BlockSpec((1, DV), lambda b, kh: (0, 0)),
        pl.BlockSpec((n_sub, rep, DK, DV), lambda b, kh: (0 if zero_init else b, kh, 0, 0)),
        pl.BlockSpec((n_sub, rep, DK, DV), lambda b, kh: (0 if zero_init else b, kh, 0, 0)),
        pl.BlockSpec(memory_space=pl.ANY),
    ]
    out_specs = [pl.BlockSpec((t_blk, rep * DV), lambda b, kh: (row_blk0 + b, kh)), state_spec, state_spec]
    out_shape = [
        jax.ShapeDtypeStruct((qk.shape[0], V_W), BF16),
        jax.ShapeDtypeStruct((n_seq, DEPTH, NV_HEADS, DK, DV), F32),
        jax.ShapeDtypeStruct((n_seq, DEPTH, NV_HEADS, DK, DV), F32),
    ]
    args = [qk, qk, v, proj, cols, rows, norm_w.reshape(1, DV), s0f, s0b, o_prev]
    return pl.pallas_call(
        functools.partial(_delta_body, n_sub=n_sub, n_chunks=n_chunks, zero_init=zero_init),
        grid=(n_seq // n_sub, NK_HEADS),
        input_output_aliases={len(args) - 1: 0},
        name="deltanet",
        in_specs=in_specs,
        out_specs=out_specs,
        out_shape=out_shape,
        scratch_shapes=[pltpu.VMEM((t_blk, rep * DV), F32), pltpu.VMEM((2 * rep * n_sub, DK, DV), F32),
                        pltpu.VMEM((2, c_blk, CHUNK, 2 * rep * DV), BF16),
                        pltpu.VMEM((2, c_blk, CHUNK, 2 * rep * DV), BF16),
                        pltpu.VMEM((c_blk, DK, CHUNK), BF16)],
        compiler_params=pltpu.CompilerParams(dimension_semantics=("parallel", "parallel"),
                                             vmem_limit_bytes=VMEM_LIMIT_BYTES),
    )(*args)


def _decay_tables(ab, a_log, dt_bias):
    n = ab.shape[0]
    nh2 = 2 * NV_HEADS
    rep = NV_HEADS // NK_HEADS
    beta = jax.nn.sigmoid(ab[:, :nh2])
    g = -jnp.exp(a_log.reshape(nh2)) * jax.nn.softplus(ab[:, nh2:] + dt_bias.reshape(nh2))
    gch = g.reshape(n // CHUNK, CHUNK, 2, NV_HEADS)
    pre = jnp.cumsum(gch, axis=1)
    gl = pre[:, -1]
    gc = jnp.concatenate([pre[:, :, :1], gl[:, None, 1:] - pre[:, :, 1:] + gch[:, :, 1:]], axis=2)

    def per_kh(a):
        a = a.reshape(a.shape[:-2] + (2, NK_HEADS, rep))
        return jnp.moveaxis(a, -3, -2).reshape(a.shape[:-3] + (NK_HEADS, 2 * rep))
    gc_k = per_kh(gc)
    cols = jnp.concatenate([gc_k.reshape(n, NK_HEADS, 2 * rep), per_kh(beta.reshape(n, 2, NV_HEADS))], axis=-1)
    cols = jnp.transpose(cols, (1, 0, 2))
    gr = jnp.transpose(gc_k, (2, 0, 3, 1))
    gr = jnp.concatenate([gr, gr], axis=-1)
    glr = jnp.broadcast_to(jnp.transpose(per_kh(gl), (1, 0, 2))[..., None], gr.shape)
    rows = jnp.concatenate([gr, glr], axis=2)
    return cols, rows


TOK_TILE = 512
MIX_TILE = 256
COMBINE_TILE = 128


def _group_of_tile(i, tile, n_prompt, t_sample):
    first = n_prompt // tile
    return jnp.where(i < first, 0, 1 + (i - first) // (t_sample // tile))


def _token_specs(tile, n_prompt, t_sample, d):
    first = n_prompt // tile
    per_seq = t_sample // tile
    return [
        pl.BlockSpec((tile, d), lambda i, *_: (jnp.minimum(i, first - 1), 0)),
        pl.BlockSpec((tile, d), lambda i, *_: (jnp.maximum(i - first, 0), 0)),
        pl.BlockSpec((tile, d), lambda i, *_: (jnp.maximum(i - first, 0) % per_seq, 0)),
        pl.BlockSpec((None, 1, 6 * d), lambda i, *_: (_group_of_tile(i, tile, n_prompt, t_sample), 0, 0)),
    ]


def _token_rows(i, first, xp_ref, xs_ref, pos_ref):
    return jnp.where(i < first, xp_ref[...], xs_ref[...] + pos_ref[...])


def _modulate_body(xp_ref, xs_ref, pos_ref, mod_ref, h_ref, *, first):
    d = h_ref.shape[-1]
    x = _token_rows(pl.program_id(0), first, xp_ref, xs_ref, pos_ref)
    h_ref[...] = (x * (1.0 + mod_ref[:, d:2 * d]) + mod_ref[:, 0:d]).astype(h_ref.dtype)


def _modulate(xp, xs, pos, mod, *, t_sample):
    n_p, d = xp.shape
    n = n_p + xs.shape[0]
    return pl.pallas_call(
        functools.partial(_modulate_body, first=n_p // TOK_TILE),
        grid=(n // TOK_TILE,),
        in_specs=_token_specs(TOK_TILE, n_p, t_sample, d),
        out_specs=pl.BlockSpec((TOK_TILE, d), lambda i: (i, 0)),
        out_shape=jax.ShapeDtypeStruct((n, d), BF16),
        compiler_params=pltpu.CompilerParams(dimension_semantics=("parallel",), vmem_limit_bytes=VMEM_LIMIT_BYTES),
        name="modulate",
    )(xp, xs, pos, mod)


def _gconv_body(ub_ref, uc_ref, ux_ref, w_ref, o_ref, *, t_first, t_rest):
    t_seq = jnp.where(pl.program_id(0) == 0, t_first, t_rest)
    x = uc_ref[...] * ux_ref[...]
    n = x.shape[0]
    tpos = lax.broadcasted_iota(jnp.int32, x.shape, 0) & (t_seq - 1)
    xm = jnp.where(tpos == 0, 0.0, pltpu.roll(x, 1, axis=0))
    xp = jnp.where(tpos == t_seq - 1, 0.0, pltpu.roll(x, n - 1, axis=0))
    w = w_ref[...]
    o_ref[...] = (ub_ref[...] * (xm * w[0:1] + x * w[1:2] + xp * w[2:3])).astype(o_ref.dtype)


def _gated_conv(proj, conv_w, col0, *, t_first, t_rest):
    n = proj.shape[0]
    nblk = CONV_CH // DK
    b0 = col0 // DK
    return pl.pallas_call(
        functools.partial(_gconv_body, t_first=t_first, t_rest=t_rest),
        grid=(n // PREP_ROWS, nblk),
        in_specs=[pl.BlockSpec((PREP_ROWS, DK), lambda i, j: (i, b0 + j)),
                  pl.BlockSpec((PREP_ROWS, DK), lambda i, j: (i, b0 + nblk + j)),
                  pl.BlockSpec((PREP_ROWS, DK), lambda i, j: (i, b0 + 2 * nblk + j)),
                  pl.BlockSpec((3, DK), lambda i, j: (0, j))],
        out_specs=pl.BlockSpec((PREP_ROWS, DK), lambda i, j: (i, j)),
        out_shape=jax.ShapeDtypeStruct((n, CONV_CH), BF16),
        compiler_params=pltpu.CompilerParams(dimension_semantics=("parallel", "parallel"),
                                             vmem_limit_bytes=VMEM_LIMIT_BYTES),
        name="gated_conv",
    )(proj, proj, proj, conv_w)


BR_TM = 512
BR_TN = 512


def _branch_body(o_ref, g_ref, wa_ref, wb_ref, ra_ref, rb_ref, m_ref, wa_bf, wb_bf):
    @pl.when(pl.program_id(1) == 0)
    def _():
        wa_bf[...] = wa_ref[...].astype(BF16)
        wb_bf[...] = wb_ref[...].astype(BF16)
    ya = jnp.dot(o_ref[...], wa_bf[...], preferred_element_type=F32)
    yb = jnp.dot(g_ref[...], wb_bf[...], preferred_element_type=F32)
    m_ref[...] = (jax.nn.sigmoid(ra_ref[...]) * ya + jax.nn.sigmoid(rb_ref[...]) * yb).astype(m_ref.dtype)


def _branch_merge(o, g, w_a, w_b, proj, ra_col, rb_col):
    n = o.shape[0]
    d = w_a.shape[1]
    el = pl.Element
    return pl.pallas_call(
        _branch_body,
        grid=(d // BR_TN, n // BR_TM),
        in_specs=[pl.BlockSpec((BR_TM, o.shape[1]), lambda j, i: (i, 0)),
                  pl.BlockSpec((BR_TM, g.shape[1]), lambda j, i: (i, 0)),
                  pl.BlockSpec((w_a.shape[0], BR_TN), lambda j, i: (0, j)),
                  pl.BlockSpec((w_b.shape[0], BR_TN), lambda j, i: (0, j)),
                  pl.BlockSpec((el(BR_TM), el(BR_TN)),
                               lambda j, i: (i * BR_TM, pl.multiple_of(ra_col + j * BR_TN, DK))),
                  pl.BlockSpec((el(BR_TM), el(BR_TN)),
                               lambda j, i: (i * BR_TM, pl.multiple_of(rb_col + j * BR_TN, DK)))],
        out_specs=pl.BlockSpec((BR_TM, BR_TN), lambda j, i: (i, j)),
        out_shape=jax.ShapeDtypeStruct((n, d), BF16),
        scratch_shapes=[pltpu.VMEM((w_a.shape[0], BR_TN), BF16), pltpu.VMEM((w_b.shape[0], BR_TN), BF16)],
        compiler_params=pltpu.CompilerParams(dimension_semantics=("arbitrary", "arbitrary"),
                                             vmem_limit_bytes=VMEM_LIMIT_BYTES),
        name="branch_merge",
    )(o, g, w_a, w_b, proj, proj)


def _ln_rows(v, g, b):
    mu = jnp.mean(v, axis=-1, keepdims=True)
    c = v - mu
    return c * lax.rsqrt(jnp.mean(c * c, axis=-1, keepdims=True) + EPS) * g + b


def _mix_body(m_ref, wo_ref, xp_ref, xs_ref, pos_ref, mod_ref, g_ref, b_ref, rw_ref, rb_ref,
              x1_ref, h2_ref, lg_ref, *, first):
    d = x1_ref.shape[-1]
    x = _token_rows(pl.program_id(0), first, xp_ref, xs_ref, pos_ref)
    mixed = jnp.dot(m_ref[...], wo_ref[...], preferred_element_type=F32)
    x1 = _ln_rows(DEEPNORM_ALPHA * x + mod_ref[:, 2 * d:3 * d] * mixed, g_ref[...], b_ref[...])
    h2 = x1 * (1.0 + mod_ref[:, 4 * d:5 * d]) + mod_ref[:, 3 * d:4 * d]
    x1_ref[...] = x1
    h2_ref[...] = _pack_bf16_pairs(h2)
    lg_ref[...] = jnp.dot(h2, rw_ref[...], precision=lax.Precision.HIGHEST, preferred_element_type=F32) + rb_ref[...]


def _mix_norm_route(m, w_o_bf, xp, xs, pos, mod, ln_g, ln_b, router_w, router_b, *, t_sample):
    n, d = m.shape
    n_p = xp.shape[0]
    ne = router_w.shape[1]
    const = lambda i: (0, 0)
    return pl.pallas_call(
        functools.partial(_mix_body, first=n_p // MIX_TILE),
        grid=(n // MIX_TILE,),
        in_specs=[pl.BlockSpec((MIX_TILE, d), lambda i: (i, 0)), pl.BlockSpec((d, d), const)]
        + _token_specs(MIX_TILE, n_p, t_sample, d)
        + [pl.BlockSpec((1, d), const), pl.BlockSpec((1, d), const),
           pl.BlockSpec((d, ne), const), pl.BlockSpec((1, ne), const)],
        out_specs=[pl.BlockSpec((MIX_TILE, d), lambda i: (i, 0)), pl.BlockSpec((MIX_TILE, d // 2), lambda i: (i, 0)),
                   pl.BlockSpec((MIX_TILE, ne), lambda i: (i, 0))],
        out_shape=[jax.ShapeDtypeStruct((n, d), F32), jax.ShapeDtypeStruct((n, d // 2), jnp.uint32),
                   jax.ShapeDtypeStruct((n, ne), F32)],
        compiler_params=pltpu.CompilerParams(dimension_semantics=("parallel",), vmem_limit_bytes=VMEM_LIMIT_BYTES),
        name="mix_norm_route",
    )(m, w_o_bf, xp, xs, pos, mod, ln_g.reshape(1, d), ln_b.reshape(1, d), router_w, router_b.reshape(1, ne))


def _combine_body(slot_ref, y_hbm, p_ref, x1_ref, mod_ref, g_ref, b_ref, op_ref, os_ref, buf, sem, *,
                  n_tiles, first):
    tt = COMBINE_TILE
    d = x1_ref.shape[-1]
    i = pl.program_id(0)

    def row_copy(src_row, dst_row, which):
        return pltpu.make_async_copy(y_hbm.at[pl.ds(src_row, 1), :], buf.at[which, pl.ds(dst_row, 1), :],
                                     sem.at[which])

    def issue(tile, which):
        base = tile * (tt * TOP_K)

        def body(t, carry):
            for k in range(TOP_K):
                row_copy(slot_ref[base + t * TOP_K + k], k * tt + t, which).start(priority=k % 2)
            return carry
        lax.fori_loop(0, tt, body, 0)

    @pl.when(i == 0)
    def _():
        issue(0, 0)

    @pl.when(i + 1 < n_tiles)
    def _():
        issue(i + 1, (i + 1) % 2)

    cur = i % 2
    pltpu.make_async_copy(y_hbm.at[pl.ds(0, TOP_K * tt), :], buf.at[cur], sem.at[cur]).wait()
    p = p_ref[...]
    ff = p[:, 0:1] * buf[cur, 0:tt, :]
    for k in range(1, TOP_K):
        ff = ff + p[:, k:k + 1] * buf[cur, k * tt:(k + 1) * tt, :]
    x2 = _ln_rows(DEEPNORM_ALPHA * x1_ref[...] + mod_ref[:, 5 * d:6 * d] * ff, g_ref[...], b_ref[...])

    @pl.when(i < first)
    def _():
        op_ref[...] = x2

    @pl.when(i >= first)
    def _():
        os_ref[...] = x2


def _combine_norm(slot, y, top_p, x1, mod, ln_g, ln_b, *, n_prompt, t_sample):
    n, d = x1.shape
    tt = COMBINE_TILE
    n_tiles = n // tt
    first = n_prompt // tt
    grid_spec = pltpu.PrefetchScalarGridSpec(
        num_scalar_prefetch=1,
        grid=(n_tiles,),
        in_specs=[pl.BlockSpec(memory_space=pl.ANY),
                  pl.BlockSpec((tt, TOP_K), lambda i, s: (i, 0)),
                  pl.BlockSpec((tt, d), lambda i, s: (i, 0)),
                  pl.BlockSpec((None, 1, 6 * d), lambda i, s: (_group_of_tile(i, tt, n_prompt, t_sample), 0, 0)),
                  pl.BlockSpec((1, d), lambda i, s: (0, 0)),
                  pl.BlockSpec((1, d), lambda i, s: (0, 0))],
        out_specs=[pl.BlockSpec((tt, d), lambda i, s: (jnp.minimum(i, first - 1), 0)),
                   pl.BlockSpec((tt, d), lambda i, s: (jnp.maximum(i - first, 0), 0))],
        scratch_shapes=[pltpu.VMEM((2, TOP_K * tt, d), F32), pltpu.SemaphoreType.DMA((2,))],
    )
    return pl.pallas_call(
        functools.partial(_combine_body, n_tiles=n_tiles, first=first),
        grid_spec=grid_spec,
        out_shape=[jax.ShapeDtypeStruct((n_prompt, d), F32), jax.ShapeDtypeStruct((n - n_prompt, d), F32)],
        compiler_params=pltpu.CompilerParams(dimension_semantics=("arbitrary",), vmem_limit_bytes=VMEM_LIMIT_BYTES),
        name="combine_norm",
    )(slot, y, top_p, x1, mod, ln_g.reshape(1, d), ln_b.reshape(1, d))


def _grid_pos_embed(rows, dim):
    quarter = dim // 4
    freqs = 1.0 / (POS_BASE ** (jnp.arange(quarter, dtype=F32) / quarter))

    def axis_embed(n):
        ang = jnp.arange(n, dtype=F32)[:, None] * freqs[None, :]
        return jnp.concatenate([jnp.sin(ang), jnp.cos(ang)], axis=-1)
    er = axis_embed(rows)
    ec = axis_embed(GRID_W)
    pos = jnp.concatenate([jnp.broadcast_to(er[:, None, :], (rows, GRID_W, dim // 2)),
                           jnp.broadcast_to(ec[None, :, :], (rows, GRID_W, dim // 2))], axis=-1)
    return pos.reshape(rows * GRID_W, dim)


def kernel(x_prompt, x_sample, c, state_fwd, state_bwd, c_ctx, w_mod, b_mod, w_in, conv_qkv, a_log, dt_bias,
           norm_o, w_a_out, conv_b, w_b_out, w_o, ln1_g, ln1_b, router_w, router_b, w_gu, b_gu, w_down, b_down,
           ln2_g, ln2_b):
    nb_p, t_p, d = x_prompt.shape
    nb_s, t_s, _ = x_sample.shape
    n_p = nb_p * t_p
    n_s = nb_s * t_s
    assert n_p == PREP_ROWS and t_s == PREP_ROWS
    xp = x_prompt.reshape(n_p, d)
    xs = x_sample.reshape(n_s, d)
    pos = _grid_pos_embed(t_s // GRID_W, d)
    offs = tuple(int(o) for o in np.cumsum(IN_SIZES)[:-1])

    l = 0
    cvec = jnp.concatenate([c_ctx[None, :], c], axis=0)
    cpad = jnp.zeros((8, d), F32).at[:1 + nb_s].set(jax.nn.silu(cvec))
    mod = (_mm(cpad, w_mod[l], tm=8, tn=512)[:1 + nb_s] + b_mod[l])[:, None, :]

    h = _modulate(xp, xs, pos, mod, t_sample=t_s)
    proj = _mm(h, w_in[l], tm=2048, tn=512)

    qk = _conv_silu(proj, conv_qkv[l], 0, 2 * QK_W, t_first=t_p, t_rest=t_s, unit_norm=True)
    v = _conv_silu(proj, conv_qkv[l], 2 * QK_W, V_W, t_first=t_p, t_rest=t_s, unit_norm=False)
    cols, rws = _decay_tables(proj[:, offs[1]:offs[3]], a_log[l], dt_bias[l])
    zero_state = jnp.zeros((PROMPT_SUB, NV_HEADS, DK, DV), F32)
    o_all = jnp.zeros((n_p + n_s, V_W), BF16)
    o_all, sf, sb = _deltanet(qk, v, proj, cols, rws, norm_o[l], zero_state, zero_state, o_all,
                              n_seq=nb_p, t_seq=t_p, n_sub=PROMPT_SUB, row_blk0=0, zero_init=True)
    o_all, _, _ = _deltanet(qk, v, proj, cols, rws, norm_o[l], state_fwd[:, l], state_bwd[:, l], o_all,
                            n_seq=nb_s, t_seq=t_s, n_sub=1, row_blk0=n_p // t_s, zero_init=False)

    gconv = _gated_conv(proj, conv_b[l], offs[3], t_first=t_p, t_rest=t_s)
    m = _branch_merge(o_all, gconv, w_a_out[l], w_b_out[l], proj, offs[6], offs[7])
    x1, h2, logits = _mix_norm_route(m, w_o[l].astype(BF16), xp, xs, pos, mod, ln1_g[l], ln1_b[l],
                                     router_w[l], router_b[l], t_sample=t_s)
    slot, slot_tok, top_p, block_e, block_rows, n_valid = _route(logits)
    y = _moe_blocks(h2, slot_tok, block_e, block_rows, n_valid, w_gu[l], b_gu[l], w_down[l], b_down[l])
    x2_p, x2_s = _combine_norm(slot, y, top_p, x1, mod, ln2_g[l], ln2_b[l], n_prompt=n_p, t_sample=t_s)
    return (x2_p.reshape(nb_p, t_p, d), x2_s.reshape(nb_s, t_s, d), sf, sb)
```
